```python
import jax
import jax.numpy as jnp
from jax import lax
import numpy as np

D_MODEL = 1024
BATCH = 2
SEQ = 8192
DEPTH = 4
DEC_BATCH = 8
DEC_SEQ = 2048
PAST_LEN = 128

HEAD_DIM = 64
GRID_W = 64
NA_HEADS = 8
NA_ROWS = 8
NA_COLS = 16
NA_QCOLS = 16
NA_BAND = NA_QCOLS + NA_COLS
GQA_Q_HEADS = 8
GQA_KV_HEADS = 2
AXIAL_THETA = 10000.0
Q_BLOCK = 128
QK_NORM_EPS = 1e-6
DIL_HEADS = 16
DIL_BRANCHES = ((128, 1), (512, 4), (2048, 16))
ROPE_THETA = 500000.0
ROPE_DIMS = HEAD_DIM // 4
EVEN_IN = 3 * NA_HEADS * HEAD_DIM + GQA_Q_HEADS * HEAD_DIM + 2 * GQA_KV_HEADS * HEAD_DIM
EVEN_OUT = (NA_HEADS + GQA_Q_HEADS) * HEAD_DIM
ODD_IN = 3 * DIL_HEADS * HEAD_DIM
ODD_OUT = DIL_HEADS * HEAD_DIM
N_EXPERTS = 32
TOP_K = 4
D_FF = D_MODEL
SWIGLU_LIMIT = 7.0
SWIGLU_ALPHA = 1.702
EXPERT_BLOCK = 128
DN_ALPHA = (2 * DEPTH) ** 0.25
DN_BETA = (8 * DEPTH) ** -0.25
LN_EPS = 1e-5
N_EVEN = (DEPTH + 1) // 2
N_ODD = DEPTH // 2

kernel_name = 'hybrid_natten_gqa_dilated_moe_encoder'


def _layer_norm(x, g, b):
    xf = x.astype(jnp.float32)
    mu = jnp.mean(xf, -1, keepdims=True)
    var = jnp.mean(jnp.square(xf - mu), -1, keepdims=True)
    y = (xf - mu) * lax.rsqrt(var + LN_EPS)
    return (y * g.astype(jnp.float32) + b.astype(jnp.float32)).astype(x.dtype)


def _rms_norm(x, g):
    xf = x.astype(jnp.float32)
    y = xf * lax.rsqrt(jnp.mean(jnp.square(xf), -1, keepdims=True) + QK_NORM_EPS)
    return (y * g.astype(jnp.float32)).astype(x.dtype)


def _rotate(x, ang):
    n = ang.shape[-1]
    c = jnp.cos(ang)[:, None, :].astype(x.dtype)
    s = jnp.sin(ang)[:, None, :].astype(x.dtype)
    x1, x2 = x[..., :n], x[..., n:]
    return jnp.concatenate([x1 * c - x2 * s, x2 * c + x1 * s], axis=-1)


def _axial_angles(s):
    t = jnp.arange(s)
    row = (t // GRID_W).astype(jnp.float32)
    col = (t % GRID_W).astype(jnp.float32)
    n = HEAD_DIM // 4
    inv = AXIAL_THETA ** (-jnp.arange(n, dtype=jnp.float32) / n)
    return row[:, None] * inv, col[:, None] * inv


def _rope_angles(s):
    n = ROPE_DIMS // 2
    inv = ROPE_THETA ** (-jnp.arange(n, dtype=jnp.float32) / n)
    return jnp.arange(s, dtype=jnp.float32)[:, None] * inv


def _neighbourhood_attention(q, k, v, rpb):
    b, s, h, dh = q.shape
    rows = s // GRID_W
    kh = min(NA_ROWS, rows)
    n_cb = GRID_W // NA_QCOLS
    qcol = np.arange(GRID_W).reshape(n_cb, NA_QCOLS)
    band_start = np.clip(np.arange(n_cb) * NA_QCOLS - NA_COLS // 2, 0, GRID_W - NA_BAND)
    band_cols = band_start[:, None] + np.arange(NA_BAND)[None, :]
    win_start = np.clip(qcol - NA_COLS // 2, 0, GRID_W - NA_COLS)
    kc = band_cols[:, None, :]
    col_valid = (kc >= win_start[..., None]) & (kc < win_start[..., None] + NA_COLS)
    col_idx = np.clip(kc - qcol[..., None] + NA_COLS - 1, 0, 2 * NA_COLS - 2)
    mask = np.broadcast_to(col_valid[:, :, None, :], (n_cb, NA_QCOLS, kh, NA_BAND)).reshape(n_cb, NA_QCOLS, kh * NA_BAND)
    qg = (q * (dh ** -0.5)).reshape(b, rows, GRID_W, h, dh)
    kg = k.reshape(b, rows, GRID_W, h, dh)
    vg = v.reshape(b, rows, GRID_W, h, dh)

    def row_block(r):
        rs = jnp.clip(r - kh // 2, 0, rows - kh)
        q_r = lax.dynamic_index_in_dim(qg, r, axis=1, keepdims=False).reshape(b, n_cb, NA_QCOLS, h, dh)

        def band(t):
            t = lax.dynamic_slice_in_dim(t, rs, kh, axis=1)[:, :, band_cols]
            return t.transpose(0, 2, 1, 3, 4, 5).reshape(b, n_cb, kh * NA_BAND, h, dh)

        k_r, v_r = band(kg), band(vg)
        row_idx = rs + jnp.arange(kh) - r + NA_ROWS - 1
        bias = rpb[:, row_idx][:, :, col_idx]
        bias = bias.transpose(0, 2, 3, 1, 4).reshape(h, n_cb, NA_QCOLS, kh * NA_BAND).astype(jnp.float32)
        sc = jnp.einsum('bjqhd,bjkhd->bhjqk', q_r, k_r).astype(jnp.float32) + bias[None]
        sc = jnp.where(mask[None, None], sc, -jnp.inf)
        p = jax.nn.softmax(sc, axis=-1).astype(v.dtype)
        o = jnp.einsum('bhjqk,bjkhd->bjqhd', p, v_r)
        return o.reshape(b, GRID_W, h, dh)

    out = lax.map(row_block, jnp.arange(rows))
    return out.transpose(1, 0, 2, 3, 4).reshape(b, s, h * dh)


def _gqa_attention(q, k, v):
    b, s, hq, dh = q.shape
    hkv = k.shape[2]
    g = hq // hkv
    nb = s // Q_BLOCK
    qb = (q * (dh ** -0.5)).reshape(b, nb, Q_BLOCK, hkv, g, dh).transpose(1, 0, 2, 3, 4, 5)

    def block(qi):
        sc = jnp.einsum('bqhgd,bkhd->bhgqk', qi, k).astype(jnp.float32)
        p = jax.nn.softmax(sc, axis=-1).astype(v.dtype)
        return jnp.einsum('bhgqk,bkhd->bqhgd', p, v)

    o = lax.map(block, qb)
    return o.transpose(1, 0, 2, 3, 4, 5).reshape(b, s, hq * dh)


def _dilated_branch(q, k, v, window, dilation):
    b, s, h, dh = q.shape
    radius = window // (2 * dilation)
    L = s // dilation
    nb = -(-L // radius)
    lq = nb * radius

    def sub(t):
        return t.reshape(b, L, dilation, h, dh).transpose(0, 2, 1, 3, 4)

    qs = jnp.pad(sub(q), ((0, 0), (0, 0), (0, lq - L), (0, 0), (0, 0))).reshape(b, dilation, nb, radius, h, dh)

    def band(t):
        tp = jnp.pad(sub(t), ((0, 0), (0, 0), (radius, radius + lq - L), (0, 0), (0, 0)))
        tp = tp.reshape(b, dilation, nb + 2, radius, h, dh)
        return jnp.concatenate([tp[:, :, :-2], tp[:, :, 1:-1], tp[:, :, 2:]], axis=3)

    kb, vb = band(k), band(v)
    qpos = np.arange(nb)[:, None] * radius + np.arange(radius)[None, :]
    kpos = np.arange(nb)[:, None] * radius - radius + np.arange(3 * radius)[None, :]
    diff = kpos[:, None, :] - qpos[:, :, None]
    valid = (np.abs(diff) <= radius) & (kpos[:, None, :] >= 0) & (kpos[:, None, :] < L)
    sc = jnp.einsum('brnqhd,brnkhd->brnhqk', qs, kb).astype(jnp.float32)
    sc = jnp.where(valid[None, None, :, None], sc, -jnp.inf)
    m = jnp.max(sc, axis=-1, keepdims=True)
    p = jnp.exp(sc - m)
    den = jnp.sum(p, axis=-1, keepdims=True)
    o = jnp.einsum('brnhqk,brnkhd->brnqhd', (p / den).astype(v.dtype), vb)
    lse = (m + jnp.log(den))[..., 0]
    o = o.reshape(b, dilation, lq, h, dh)[:, :, :L].transpose(0, 2, 1, 3, 4).reshape(b, s, h, dh)
    lse = lse.transpose(0, 1, 2, 4, 3).reshape(b, dilation, lq, h)[:, :, :L].transpose(0, 2, 1, 3).reshape(b, s, h)
    return o, lse


def _dilated_attention(q, k, v):
    b, s, h, dh = q.shape
    results = [_dilated_branch(q, k, v, w, r) for (w, r) in DIL_BRANCHES]
    lse = jnp.stack([res[1] for res in results], axis=0)
    wts = jax.nn.softmax(lse, axis=0)
    o = sum(wts[i][..., None] * results[i][0].astype(jnp.float32) for i in range(len(results)))
    return o.astype(q.dtype).reshape(b, s, h * dh)


def _mixer_even(x, w_in, rpb, q_gain, k_gain, w_out):
    b, s, _ = x.shape
    hd = HEAD_DIM
    proj = x @ w_in
    sizes = [NA_HEADS * hd] * 3 + [GQA_Q_HEADS * hd, GQA_KV_HEADS * hd, GQA_KV_HEADS * hd]
    offs = [int(o) for o in np.cumsum([0] + sizes)]
    parts = [proj[..., offs[i]:offs[i + 1]] for i in range(6)]
    qa, ka, va = [t.reshape(b, s, NA_HEADS, hd) for t in parts[:3]]
    qb = parts[3].reshape(b, s, GQA_Q_HEADS, hd)
    kb = parts[4].reshape(b, s, GQA_KV_HEADS, hd)
    vb = parts[5].reshape(b, s, GQA_KV_HEADS, hd)
    ang_row, ang_col = _axial_angles(s)
    half = hd // 2

    def axial(t):
        return jnp.concatenate([_rotate(t[..., :half], ang_row), _rotate(t[..., half:], ang_col)], axis=-1)

    qb = axial(_rms_norm(qb, q_gain))
    kb = axial(_rms_norm(kb, k_gain))
    ya = _neighbourhood_attention(qa, ka, va, rpb)
    yb = _gqa_attention(qb, kb, vb)
    return jnp.concatenate([ya, yb], axis=-1) @ w_out


def _mixer_odd(x, w_in, w_out):
    b, s, _ = x.shape
    q, k, v = [t.reshape(b, s, DIL_HEADS, HEAD_DIM) for t in jnp.split(x @ w_in, 3, axis=-1)]
    ang = _rope_angles(s)

    def rope(t):
        return jnp.concatenate([_rotate(t[..., :ROPE_DIMS], ang), t[..., ROPE_DIMS:]], axis=-1)

    q = rope(q) * (HEAD_DIM ** -0.5)
    k = rope(k)
    return _dilated_attention(q, k, v) @ w_out


def _moe(x, w_r, b_r, w1, b1, w2, b2):
    b, s, d = x.shape
    xf = x.reshape(-1, d)
    n = xf.shape[0]
    f = w2.shape[1]
    logits = (xf @ w_r).astype(jnp.float32) + b_r.astype(jnp.float32)
    top_val, top_idx = lax.top_k(logits, TOP_K)
    gate = jax.nn.softmax(top_val, axis=-1)
    e_flat = top_idx.reshape(-1)
    tok_flat = jnp.arange(n * TOP_K, dtype=jnp.int32) // TOP_K
    g_flat = gate.reshape(-1)
    order = jnp.argsort(e_flat)
    e_sorted = e_flat[order]
    counts = jnp.bincount(e_flat, length=N_EXPERTS)
    starts = jnp.cumsum(counts) - counts
    padded = (counts + EXPERT_BLOCK - 1) // EXPERT_BLOCK * EXPERT_BLOCK
    pad_ends = jnp.cumsum(padded)
    pad_starts = pad_ends - padded
    rank = jnp.arange(n * TOP_K, dtype=jnp.int32) - starts[e_sorted]
    dest = pad_starts[e_sorted] + rank
    n_blocks = -(-(n * TOP_K) // EXPERT_BLOCK) + N_EXPERTS
    n_rows = n_blocks * EXPERT_BLOCK
    row_tok = jnp.full((n_rows,), n, jnp.int32).at[dest].set(tok_flat[order])
    row_gate = jnp.zeros((n_rows,), jnp.float32).at[dest].set(g_flat[order])
    blk_start = jnp.arange(n_blocks, dtype=jnp.int32) * EXPERT_BLOCK
    blk_exp = jnp.minimum(jnp.searchsorted(pad_ends, blk_start, side='right'), N_EXPERTS - 1)
    xpad = jnp.concatenate([xf, jnp.zeros((1, d), xf.dtype)], axis=0)
    xr = xpad[row_tok].reshape(n_blocks, EXPERT_BLOCK, d)

    def expert_block(args):
        xb, e = args
        hcat = xb @ w1[e] + b1[e]
        g, u = hcat[:, :f], hcat[:, f:]
        g = jnp.minimum(g, SWIGLU_LIMIT)
        u = jnp.clip(u, -SWIGLU_LIMIT, SWIGLU_LIMIT)
        act = g * jax.nn.sigmoid(SWIGLU_ALPHA * g) * (u + 1.0)
        return act @ w2[e] + b2[e]

    yr = lax.map(expert_block, (xr, blk_exp)).reshape(n_rows, d)
    y = jax.ops.segment_sum(yr * row_gate[:, None].astype(yr.dtype), row_tok, num_segments=n + 1)[:n]
    return y.reshape(b, s, d)


def _trunk(x, w_in_even, rpb_a, q_gain_b, k_gain_b, w_out_even, w_in_odd, w_out_odd,
           ln1_g, ln1_b, ln2_g, ln2_b, router_w, router_b, moe_w1, moe_b1, moe_w2, moe_b2):
    for l in range(DEPTH):
        i = l // 2
        if l % 2 == 0:
            y = _mixer_even(x, w_in_even[i], rpb_a[i], q_gain_b[i], k_gain_b[i], w_out_even[i])
        else:
            y = _mixer_odd(x, w_in_odd[i], w_out_odd[i])
        x = _layer_norm(DN_ALPHA * x + y, ln1_g[l], ln1_b[l])
        y = _moe(x, router_w[l], router_b[l], moe_w1[l], moe_b1[l], moe_w2[l], moe_b2[l])
        x = _layer_norm(DN_ALPHA * x + y, ln2_g[l], ln2_b[l])
    return x


def _normal(key, shape, scale):
    return jax.random.normal(key, shape, jnp.float32) * scale


def setup_inputs(seed: int = 0) -> dict:
    key = jax.random.key(seed)
    ks = jax.random.split(key, 20)
    return {
        'x_prompt': _normal(ks[0], (BATCH, SEQ, D_MODEL), 1.0),
        'x_sample': _normal(ks[1], (DEC_BATCH, DEC_SEQ, D_MODEL), 1.0),
        'w_in_even': _normal(ks[2], (N_EVEN, D_MODEL, EVEN_IN), D_MODEL ** -0.5),
        'rpb_a': _normal(ks[3], (N_EVEN, NA_HEADS, 2 * NA_ROWS - 1, 2 * NA_COLS - 1), 0.02),
        'q_gain_b': 1.0 + _normal(ks[4], (N_EVEN, HEAD_DIM), 0.05),
        'k_gain_b': 1.0 + _normal(ks[5], (N_EVEN, HEAD_DIM), 0.05),
        'w_out_even': _normal(ks[6], (N_EVEN, EVEN_OUT, D_MODEL), DN_BETA * EVEN_OUT ** -0.5),
        'w_in_odd': _normal(ks[7], (N_ODD, D_MODEL, ODD_IN), D_MODEL ** -0.5),
        'w_out_odd': _normal(ks[8], (N_ODD, ODD_OUT, D_MODEL), DN_BETA * ODD_OUT ** -0.5),
        'ln1_g': 1.0 + _normal(ks[9], (DEPTH, D_MODEL), 0.05),
        'ln1_b': _normal(ks[10], (DEPTH, D_MODEL), 0.02),
        'ln2_g': 1.0 + _normal(ks[11], (DEPTH, D_MODEL), 0.05),
        'ln2_b': _normal(ks[12], (DEPTH, D_MODEL), 0.02),
        'router_w': _normal(ks[13], (DEPTH, D_MODEL, N_EXPERTS), D_MODEL ** -0.5),
        'router_b': _normal(ks[14], (DEPTH, N_EXPERTS), 0.01),
        'moe_w1': _normal(ks[15], (DEPTH, N_EXPERTS, D_MODEL, 2 * D_FF), D_MODEL ** -0.5),
        'moe_b1': _normal(ks[16], (DEPTH, N_EXPERTS, 2 * D_FF), 0.02),
        'moe_w2': _normal(ks[17], (DEPTH, N_EXPERTS, D_FF, D_MODEL), DN_BETA * D_FF ** -0.5),
        'moe_b2': _normal(ks[18], (DEPTH, N_EXPERTS, D_MODEL), 0.02),
    }


def reference(x_prompt, x_sample, w_in_even, rpb_a, q_gain_b, k_gain_b, w_out_even, w_in_odd, w_out_odd,
              ln1_g, ln1_b, ln2_g, ln2_b, router_w, router_b, moe_w1, moe_b1, moe_w2, moe_b2):
    y_prompt = _trunk(x_prompt, w_in_even, rpb_a, q_gain_b, k_gain_b, w_out_even, w_in_odd, w_out_odd,
                      ln1_g, ln1_b, ln2_g, ln2_b, router_w, router_b, moe_w1, moe_b1, moe_w2, moe_b2)
    y_sample = _trunk(x_sample, w_in_even, rpb_a, q_gain_b, k_gain_b, w_out_even, w_in_odd, w_out_odd,
                      ln1_g, ln1_b, ln2_g, ln2_b, router_w, router_b, moe_w1, moe_b1, moe_w2, moe_b2)
    return (y_prompt, y_sample)
```

```python
import functools

import jax
import jax.numpy as jnp
import numpy as np
from jax import lax
from jax.experimental import pallas as pl
from jax.experimental.pallas import tpu as pltpu

F32 = jnp.float32
BF16 = jnp.bfloat16
I32 = jnp.int32

D_MODEL = 1024
DEPTH = 4
HEAD_DIM = 64
GRID_W = 64
NA_HEADS = 8
NA_ROWS = 8
NA_COLS = 16
GQA_Q_HEADS = 8
GQA_KV_HEADS = 2
AXIAL_THETA = 10000.0
QK_NORM_EPS = 1e-6
DIL_HEADS = 16
DIL_BRANCHES = ((128, 1), (512, 4), (2048, 16))
ROPE_THETA = 500000.0
ROPE_DIMS = HEAD_DIM // 4
N_EXPERTS = 32
TOP_K = 4
D_FF = D_MODEL
SWIGLU_LIMIT = 7.0
SWIGLU_ALPHA = 1.702
DN_ALPHA = (2 * DEPTH) ** 0.25
LN_EPS = 1e-5
Q_SCALE = HEAD_DIM ** -0.5

LANES = 128
NEG = -1e30
VMEM_LIMIT = 56 * 1024 * 1024
TOKEN_TILE = 512
EXPERT_ROWS = 256
NA_ROW_BLOCK = 8
GQA_Q_TILE = 256
GQA_K_TILE = 512
WIN_Q_TILE = 128
WIN_RADIUS = 64

EVEN_IN = 3 * NA_HEADS * HEAD_DIM + GQA_Q_HEADS * HEAD_DIM + 2 * GQA_KV_HEADS * HEAD_DIM


def _params(*sem):
    return pltpu.CompilerParams(dimension_semantics=sem, vmem_limit_bytes=VMEM_LIMIT)


def _lane_is_low():
    return lax.broadcasted_iota(I32, (1, LANES), 1) < HEAD_DIM


def _proj_kernel(x_ref, w_ref, c_ref, s1_ref, s2_ref, gain_ref, o_ref, *, slab_modes, shift):
    x = x_ref[...].astype(BF16)
    n_out = o_ref.shape[1]
    chunk = 2 * LANES
    if any(m[0] == "norm_rope" for m in slab_modes):
        r = lax.broadcasted_iota(I32, (LANES, LANES), 0) // HEAD_DIM
        c = lax.broadcasted_iota(I32, (LANES, LANES), 1) // HEAD_DIM
        head_mean = jnp.where(r == c, 1.0 / HEAD_DIM, 0.0).astype(BF16)
    for c0 in range(0, n_out, chunk):
        acc = jnp.dot(x, w_ref[:, c0:c0 + chunk], preferred_element_type=F32)
        for s in range(chunk // LANES):
            slab = c0 // LANES + s
            mode, scale, gidx = slab_modes[slab]
            y = acc[:, s * LANES:(s + 1) * LANES]
            if mode == "norm_rope":
                sq = y * y
                hi = sq.astype(BF16)
                lo = (sq - hi.astype(F32)).astype(BF16)
                ms = (jnp.dot(hi, head_mean, preferred_element_type=F32)
                      + jnp.dot(lo, head_mean, preferred_element_type=F32))
                y = y * lax.rsqrt(ms + QK_NORM_EPS) * gain_ref[gidx:gidx + 1, :]
            if mode in ("rope", "norm_rope"):
                y = (y * c_ref[...] + pltpu.roll(y, LANES - shift, 1) * s2_ref[...]
                     + pltpu.roll(y, shift, 1) * s1_ref[...])
            if scale != 1.0:
                y = y * scale
            o_ref[:, slab * LANES:(slab + 1) * LANES] = y.astype(o_ref.dtype)


def _project(x, w, tabs, gains, slab_modes, shift):
    n, d = x.shape
    m = w.shape[1]
    tm = TOKEN_TILE
    tab_spec = pl.BlockSpec((tm, LANES), lambda i: (i, 0))
    return pl.pallas_call(
        functools.partial(_proj_kernel, slab_modes=tuple(slab_modes), shift=shift),
        grid=(n // tm,),
        in_specs=[pl.BlockSpec((tm, d), lambda i: (i, 0)),
                  pl.BlockSpec((d, m), lambda i: (0, 0)),
                  tab_spec, tab_spec, tab_spec,
                  pl.BlockSpec(gains.shape, lambda i: (0, 0))],
        out_specs=pl.BlockSpec((tm, m), lambda i: (i, 0)),
        out_shape=jax.ShapeDtypeStruct((n, m), BF16),
        compiler_params=_params("parallel"),
        name="in_proj",
    )(x, w, tabs[0], tabs[1], tabs[2], gains)


def _layer_norm_rows(z, g, b):
    mu = jnp.mean(z, axis=-1, keepdims=True)
    zc = z - mu
    var = jnp.mean(zc * zc, axis=-1, keepdims=True)
    return zc * lax.rsqrt(var + LN_EPS) * g + b


def _outproj_ln_kernel(a_ref, w_ref, x_ref, g_ref, b_ref, o_ref):
    y = jnp.dot(a_ref[...], w_ref[...], preferred_element_type=F32)
    o_ref[...] = _layer_norm_rows(DN_ALPHA * x_ref[...] + y, g_ref[...], b_ref[...])


def _outproj_ln(a, w, x, g, b):
    n, d = x.shape
    tm = TOKEN_TILE
    row = lambda i: (i, 0)
    fixed = lambda i: (0, 0)
    return pl.pallas_call(
        _outproj_ln_kernel,
        grid=(n // tm,),
        in_specs=[pl.BlockSpec((tm, a.shape[1]), row), pl.BlockSpec(w.shape, fixed),
                  pl.BlockSpec((tm, d), row), pl.BlockSpec((1, d), fixed), pl.BlockSpec((1, d), fixed)],
        out_specs=pl.BlockSpec((tm, d), row),
        out_shape=jax.ShapeDtypeStruct((n, d), F32),
        compiler_params=_params("parallel"),
        name="out_proj_ln",
    )(a, w, x, g.reshape(1, d), b.reshape(1, d))


def _na_kernel(q_ref, k_ref, v_ref, cc_ref, o_ref, *, rows):
    i = pl.program_id(2)
    low = _lane_is_low()

    def one_row(rr, carry):
        r = i * NA_ROW_BLOCK + rr
        rs = jnp.clip(r - NA_ROWS // 2, 0, rows - NA_ROWS)
        var = r - rs
        q = q_ref[pl.ds(pl.multiple_of(rr * GRID_W, GRID_W), GRID_W), :]
        kstart = pl.multiple_of(rs * GRID_W, GRID_W)
        kwin = k_ref[pl.ds(kstart, NA_ROWS * GRID_W), :]
        vwin = v_ref[pl.ds(kstart, NA_ROWS * GRID_W), :]
        outs = []
        for half in range(2):
            qm = jnp.where(low if half == 0 else jnp.logical_not(low), q, jnp.zeros_like(q))
            s = lax.dot_general(qm, kwin, (((1,), (1,)), ((), ())), preferred_element_type=F32)
            s = s + cc_ref[half, var]
            m = jnp.max(s, axis=-1, keepdims=True)
            p = jnp.exp(s - m)
            l = jnp.sum(p, axis=-1, keepdims=True)
            outs.append(jnp.dot(p.astype(BF16), vwin, preferred_element_type=F32) / l)
        o = jnp.where(low, outs[0], outs[1])
        o_ref[pl.ds(pl.multiple_of(rr * GRID_W, GRID_W), GRID_W), :] = o.astype(o_ref.dtype)
        return carry

    lax.fori_loop(0, NA_ROW_BLOCK, one_row, 0)


def _na_attention(proj, cc, row0, b, s):
    rows = s // GRID_W
    assert rows >= NA_ROWS and rows % NA_ROW_BLOCK == 0 and row0 % s == 0
    tq = NA_ROW_BLOCK * GRID_W
    n_slab = NA_HEADS * HEAD_DIM // LANES
    qb0, sb0 = row0 // tq, row0 // s
    return pl.pallas_call(
        functools.partial(_na_kernel, rows=rows),
        grid=(b, n_slab, s // tq),
        in_specs=[pl.BlockSpec((tq, LANES), lambda bi, j, i: (qb0 + bi * (s // tq) + i, j)),
                  pl.BlockSpec((s, LANES), lambda bi, j, i: (sb0 + bi, n_slab + j)),
                  pl.BlockSpec((s, LANES), lambda bi, j, i: (sb0 + bi, 2 * n_slab + j)),
                  pl.BlockSpec((2, NA_ROWS, GRID_W, NA_ROWS * GRID_W), lambda bi, j, i: (j, 0, 0, 0))],
        out_specs=pl.BlockSpec((tq, LANES), lambda bi, j, i: (bi * (s // tq) + i, j)),
        out_shape=jax.ShapeDtypeStruct((b * s, n_slab * LANES), BF16),
        compiler_params=_params("parallel", "parallel", "parallel"),
        name="na_attn",
    )(proj, proj, proj, cc)


def _na_bias_table(rpb):
    var = np.arange(NA_ROWS)[:, None, None, None]
    qc = np.arange(GRID_W)[None, :, None, None]
    j = np.arange(NA_ROWS)[None, None, :, None]
    kc = np.arange(GRID_W)[None, None, None, :]
    row_idx = np.broadcast_to(j - var + NA_ROWS - 1, (NA_ROWS, GRID_W, NA_ROWS, GRID_W))
    col_idx = np.broadcast_to(np.clip(kc - qc + NA_COLS - 1, 0, 2 * NA_COLS - 2), row_idx.shape)
    win = np.clip(qc - NA_COLS // 2, 0, GRID_W - NA_COLS)
    valid = np.broadcast_to((kc >= win) & (kc < win + NA_COLS), row_idx.shape)
    t = rpb.astype(F32)[:, row_idx, col_idx]
    t = jnp.where(valid[None], t, NEG)
    return t.reshape(rpb.shape[0], NA_ROWS, GRID_W, NA_ROWS * GRID_W)


def _gqa_kernel(q_ref, k_ref, v_ref, o_ref, acc_ref, *, n_kt):
    low = _lane_is_low()
    q = q_ref[...]
    qm = (jnp.where(low, q, jnp.zeros_like(q)), jnp.where(low, jnp.zeros_like(q), q))
    tq = q.shape[0]
    acc_ref[...] = jnp.zeros_like(acc_ref)

    def step(kt, carry):
        start = pl.multiple_of(kt * GQA_K_TILE, GQA_K_TILE)
        kb = k_ref[pl.ds(start, GQA_K_TILE), :]
        vb = v_ref[pl.ds(start, GQA_K_TILE), :]
        new = []
        for half in range(2):
            m_old, l_old = carry[half]
            s = lax.dot_general(qm[half], kb, (((1,), (1,)), ((), ())), preferred_element_type=F32)
            m_new = jnp.maximum(m_old, jnp.max(s, axis=-1, keepdims=True))
            alpha = jnp.exp(m_old - m_new)
            p = jnp.exp(s - m_new)
            l_new = alpha * l_old + jnp.sum(p, axis=-1, keepdims=True)
            acc_ref[half] = alpha * acc_ref[half] + jnp.dot(p.astype(BF16), vb, preferred_element_type=F32)
            new.append((m_new, l_new))
        return tuple(new)

    init = tuple((jnp.full((tq, 1), -jnp.inf, F32), jnp.zeros((tq, 1), F32)) for _ in range(2))
    (_, l0), (_, l1) = lax.fori_loop(0, n_kt, step, init)
    o_ref[...] = jnp.where(low, acc_ref[0] / l0, acc_ref[1] / l1).astype(o_ref.dtype)


def _gqa_attention(proj, row0, b, s):
    tq = GQA_Q_TILE
    assert s % GQA_K_TILE == 0 and row0 % s == 0
    n_slab = GQA_Q_HEADS * HEAD_DIM // LANES
    q_col = 3 * NA_HEADS * HEAD_DIM // LANES
    k_col = q_col + n_slab
    qb0, sb0 = row0 // tq, row0 // s
    return pl.pallas_call(
        functools.partial(_gqa_kernel, n_kt=s // GQA_K_TILE),
        grid=(b, n_slab, s // tq),
        in_specs=[pl.BlockSpec((tq, LANES), lambda bi, j, i: (qb0 + bi * (s // tq) + i, q_col + j)),
                  pl.BlockSpec((s, LANES), lambda bi, j, i: (sb0 + bi, k_col)),
                  pl.BlockSpec((s, LANES), lambda bi, j, i: (sb0 + bi, k_col + 1))],
        out_specs=pl.BlockSpec((tq, LANES), lambda bi, j, i: (bi * (s // tq) + i, j)),
        out_shape=jax.ShapeDtypeStruct((b * s, n_slab * LANES), BF16),
        scratch_shapes=[pltpu.VMEM((2, tq, LANES), F32)],
        compiler_params=_params("parallel", "parallel", "parallel"),
        name="gqa_attn",
    )(proj, proj, proj)


def _win_kernel(q_ref, k_ref, v_ref, o_ref, lse_ref, *, length, win):
    i = pl.program_id(2)
    low = _lane_is_low()
    q = q_ref[0]
    tq = q.shape[0]
    t0 = i * tq
    ks = pl.multiple_of(jnp.clip(t0 - WIN_RADIUS, 0, length - win), WIN_RADIUS)
    kwin = k_ref[0, pl.ds(ks, win), :]
    vwin = v_ref[0, pl.ds(ks, win), :]
    qpos = t0 + lax.broadcasted_iota(I32, (tq, win), 0)
    kpos = ks + lax.broadcasted_iota(I32, (tq, win), 1)
    valid = jnp.abs(kpos - qpos) <= WIN_RADIUS
    outs, lses = [], []
    for half in range(2):
        qm = jnp.where(low if half == 0 else jnp.logical_not(low), q, jnp.zeros_like(q))
        s = lax.dot_general(qm, kwin, (((1,), (1,)), ((), ())), preferred_element_type=F32)
        s = jnp.where(valid, s, NEG)
        m = jnp.max(s, axis=-1, keepdims=True)
        p = jnp.exp(s - m)
        l = jnp.sum(p, axis=-1, keepdims=True)
        outs.append(jnp.dot(p.astype(BF16), vwin, preferred_element_type=F32) / l)
        lses.append(m + jnp.log(l))
    o_ref[0] = jnp.where(low, outs[0], outs[1])
    lse_ref[0] = jnp.where(low, lses[0], lses[1])


def _window_attention(qkv):
    n_seq, length, _ = qkv.shape
    tq = min(WIN_Q_TILE, length)
    win = min(tq + 2 * WIN_RADIUS, length)
    n_slab = DIL_HEADS * HEAD_DIM // LANES
    out = jax.ShapeDtypeStruct((n_seq, length, n_slab * LANES), F32)
    return pl.pallas_call(
        functools.partial(_win_kernel, length=length, win=win),
        grid=(n_seq, n_slab, length // tq),
        in_specs=[pl.BlockSpec((1, tq, LANES), lambda b, j, i: (b, i, j)),
                  pl.BlockSpec((1, length, LANES), lambda b, j, i: (b, 0, n_slab + j)),
                  pl.BlockSpec((1, length, LANES), lambda b, j, i: (b, 0, 2 * n_slab + j))],
        out_specs=[pl.BlockSpec((1, tq, LANES), lambda b, j, i: (b, i, j)),
                   pl.BlockSpec((1, tq, LANES), lambda b, j, i: (b, i, j))],
        out_shape=[out, out],
        compiler_params=_params("parallel", "parallel", "parallel"),
        name="win_attn",
    )(qkv, qkv, qkv)


def _merge_kernel(*refs):
    n = (len(refs) - 1) // 2
    o_refs, l_refs, out_ref = refs[:n], refs[n:2 * n], refs[-1]
    lses = [r[...] for r in l_refs]
    m = functools.reduce(jnp.maximum, lses)
    ws = [jnp.exp(l - m) for l in lses]
    den = functools.reduce(jnp.add, ws)
    num = functools.reduce(jnp.add, [(w / den) * r[...] for w, r in zip(ws, o_refs)])
    out_ref[...] = num.astype(out_ref.dtype)


def _merge_branches(os_, lses):
    n, d = os_[0].shape
    tm = TOKEN_TILE
    spec = pl.BlockSpec((tm, d), lambda i: (i, 0))
    return pl.pallas_call(
        _merge_kernel,
        grid=(n // tm,),
        in_specs=[spec] * (2 * len(os_)),
        out_specs=spec,
        out_shape=jax.ShapeDtypeStruct((n, d), BF16),
        compiler_params=_params("parallel"),
        name="merge_branches",
    )(*os_, *lses)


def _router_kernel(x_ref, wh_ref, wl_ref, b_ref, idx_ref, gate_ref, rank_ref, cnt_ref, base_ref):
    step = pl.program_id(0)

    @pl.when(step == 0)
    def _():
        base_ref[...] = jnp.zeros_like(base_ref)

    x = x_ref[...]
    xh = x.astype(BF16)
    xl = (x - xh.astype(F32)).astype(BF16)
    nt = (((1,), (1,)), ((), ()))
    logits = (lax.dot_general(wh_ref[...], xh, nt, preferred_element_type=F32)
              + lax.dot_general(wl_ref[...], xh, nt, preferred_element_type=F32)
              + lax.dot_general(wh_ref[...], xl, nt, preferred_element_type=F32)) + b_ref[...]
    tm = x.shape[0]
    eid = lax.broadcasted_iota(I32, (N_EXPERTS, tm), 0)
    vals = logits
    top_v, top_i, hots = [], [], []
    for _ in range(TOP_K):
        m = jnp.max(vals, axis=0, keepdims=True)
        idx = jnp.min(jnp.where(vals == m, eid, N_EXPERTS), axis=0, keepdims=True)
        hot = eid == idx
        top_v.append(m)
        top_i.append(idx)
        hots.append(hot)
        vals = jnp.where(hot, -jnp.inf, vals)
    es = [jnp.exp(v - top_v[0]) for v in top_v]
    den = functools.reduce(jnp.add, es)
    chosen = functools.reduce(jnp.logical_or, hots)
    before = (lax.broadcasted_iota(I32, (tm, tm), 0) < lax.broadcasted_iota(I32, (tm, tm), 1))
    prefix = jnp.dot(chosen.astype(BF16), before.astype(BF16), preferred_element_type=F32) + base_ref[...]
    for k in range(TOP_K):
        idx_ref[k:k + 1, :] = top_i[k]
        gate_ref[k:k + 1, :] = es[k] / den
        rank_ref[k:k + 1, :] = jnp.sum(jnp.where(hots[k], prefix, 0.0), axis=0, keepdims=True).astype(I32)
    base_ref[...] = base_ref[...] + jnp.sum(chosen.astype(F32), axis=1, keepdims=True)
    cnt_ref[...] = base_ref[...].astype(I32)


def _router(x, w_hi_t, w_lo_t, bias):
    n, d = x.shape
    tm = TOKEN_TILE
    tok = pl.BlockSpec((TOP_K, tm), lambda i: (0, i))
    fixed = lambda i: (0, 0)
    return pl.pallas_call(
        _router_kernel,
        grid=(n // tm,),
        in_specs=[pl.BlockSpec((tm, d), lambda i: (i, 0)), pl.BlockSpec((N_EXPERTS, d), fixed),
                  pl.BlockSpec((N_EXPERTS, d), fixed), pl.BlockSpec((N_EXPERTS, 1), fixed)],
        out_specs=[tok, tok, tok, pl.BlockSpec((N_EXPERTS, 1), fixed)],
        out_shape=[jax.ShapeDtypeStruct((TOP_K, n), I32), jax.ShapeDtypeStruct((TOP_K, n), F32),
                   jax.ShapeDtypeStruct((TOP_K, n), I32), jax.ShapeDtypeStruct((N_EXPERTS, 1), I32)],
        scratch_shapes=[pltpu.VMEM((N_EXPERTS, 1), F32)],
        compiler_params=_params("arbitrary"),
        name="router",
    )(x, w_hi_t, w_lo_t, bias.reshape(N_EXPERTS, 1).astype(F32))


def _expert_kernel(blk_exp_ref, n_used_ref, x_ref, w1_ref, b1_ref, w2_ref, b2_ref, o_ref):
    blk = pl.program_id(0)

    @pl.when(blk < n_used_ref[0])
    def _():
        h = jnp.dot(x_ref[...], w1_ref[0], preferred_element_type=F32) + b1_ref[0]
        g = jnp.minimum(h[:, :D_FF], SWIGLU_LIMIT)
        u = jnp.clip(h[:, D_FF:], -SWIGLU_LIMIT, SWIGLU_LIMIT)
        act = g * jax.nn.sigmoid(SWIGLU_ALPHA * g) * (u + 1.0)
        o_ref[...] = jnp.dot(act.astype(BF16), w2_ref[0], preferred_element_type=F32) + b2_ref[0]

    @pl.when(blk >= n_used_ref[0])
    def _():
        o_ref[...] = jnp.zeros_like(o_ref)


def _experts(xs, blk_exp, n_used, w1, b1, w2, b2):
    n_rows, d = xs.shape
    bm = EXPERT_ROWS
    f2 = w1.shape[2]
    return pl.pallas_call(
        _expert_kernel,
        grid_spec=pltpu.PrefetchScalarGridSpec(
            num_scalar_prefetch=2,
            grid=(n_rows // bm,),
            in_specs=[pl.BlockSpec((bm, d), lambda i, be, nu: (i, 0)),
                      pl.BlockSpec((1, d, f2), lambda i, be, nu: (be[i], 0, 0)),
                      pl.BlockSpec((1, 1, f2), lambda i, be, nu: (be[i], 0, 0)),
                      pl.BlockSpec((1, f2 // 2, d), lambda i, be, nu: (be[i], 0, 0)),
                      pl.BlockSpec((1, 1, d), lambda i, be, nu: (be[i], 0, 0))],
            out_specs=pl.BlockSpec((bm, d), lambda i, be, nu: (i, 0)),
        ),
        out_shape=jax.ShapeDtypeStruct((n_rows, d), F32),
        compiler_params=_params("arbitrary"),
        name="experts",
    )(blk_exp, n_used, xs, w1, b1, w2, b2)


def _combine_ln_kernel(y_ref, gate_ref, x_ref, g_ref, b_ref, o_ref):
    gate = gate_ref[...]
    y = functools.reduce(jnp.add, [gate[:, k:k + 1] * y_ref[k] for k in range(TOP_K)])
    o_ref[...] = _layer_norm_rows(DN_ALPHA * x_ref[...] + y, g_ref[...], b_ref[...])


def _combine_ln(yg, gate, x, g, b):
    n, d = x.shape
    tm = TOKEN_TILE
    row = lambda i: (i, 0)
    fixed = lambda i: (0, 0)
    return pl.pallas_call(
        _combine_ln_kernel,
        grid=(n // tm,),
        in_specs=[pl.BlockSpec((TOP_K, tm, d), lambda i: (0, i, 0)), pl.BlockSpec((tm, TOP_K), row),
                  pl.BlockSpec((tm, d), row), pl.BlockSpec((1, d), fixed), pl.BlockSpec((1, d), fixed)],
        out_specs=pl.BlockSpec((tm, d), row),
        out_shape=jax.ShapeDtypeStruct((n, d), F32),
        compiler_params=_params("parallel"),
        name="combine_ln",
    )(yg, gate, x, g.reshape(1, d), b.reshape(1, d))


def _moe(x, w_r, b_r, w1, b1, w2, b2):
    n, d = x.shape
    bm = EXPERT_ROWS
    wr_t = w_r.T
    wr_hi = wr_t.astype(BF16)
    wr_lo = (wr_t - wr_hi.astype(F32)).astype(BF16)
    idx, gate, rank, counts = _router(x, wr_hi, wr_lo, b_r)
    counts = counts[:, 0]
    padded = (counts + bm - 1) // bm * bm
    pad_ends = jnp.cumsum(padded)
    pad_starts = pad_ends - padded
    dest = pad_starts[idx] + rank
    n_blocks = n * TOP_K // bm + N_EXPERTS
    blk_start = jnp.arange(n_blocks, dtype=I32) * bm
    blk_exp = jnp.minimum(jnp.searchsorted(pad_ends, blk_start, side="right"), N_EXPERTS - 1).astype(I32)
    n_used = (pad_ends[-1:] // bm).astype(I32)
    tok = jnp.broadcast_to(jnp.arange(n, dtype=I32)[None, :], (TOP_K, n))
    row_tok = jnp.zeros((n_blocks * bm,), I32).at[dest.reshape(-1)].set(tok.reshape(-1))
    xs = jnp.take(x.astype(BF16), row_tok, axis=0)
    ys = _experts(xs, blk_exp, n_used, w1, b1, w2, b2)
    yg = jnp.take(ys, dest.reshape(-1), axis=0).reshape(TOP_K, n, d)
    return yg, gate.T


def _positions(groups):
    return jnp.concatenate([jnp.tile(jnp.arange(s), b) for b, s in groups])


def _axial_tables(groups):
    t = _positions(groups)
    n = HEAD_DIM // 4
    inv = AXIAL_THETA ** (-jnp.arange(n, dtype=F32) / n)
    ar = (t // GRID_W).astype(F32)[:, None] * inv
    ac = (t % GRID_W).astype(F32)[:, None] * inv
    z = jnp.zeros_like(ar)
    cr, sr, cc, sc = jnp.cos(ar), jnp.sin(ar), jnp.cos(ac), jnp.sin(ac)
    c = jnp.concatenate([cr, cr, cc, cc], axis=-1)
    s1 = jnp.concatenate([z, sr, z, sc], axis=-1)
    s2 = jnp.concatenate([-sr, z, -sc, z], axis=-1)
    return tuple(jnp.tile(a, (1, 2)) for a in (c, s1, s2))


def _rope_tables(groups):
    t = _positions(groups)
    n = ROPE_DIMS // 2
    inv = ROPE_THETA ** (-jnp.arange(n, dtype=F32) / n)
    ang = t.astype(F32)[:, None] * inv
    c, s = jnp.cos(ang), jnp.sin(ang)
    z = jnp.zeros_like(c)
    rest = HEAD_DIM - ROPE_DIMS
    pad1 = jnp.ones((t.shape[0], rest), F32)
    pad0 = jnp.zeros((t.shape[0], rest), F32)
    cc = jnp.concatenate([c, c, pad1], axis=-1)
    s1 = jnp.concatenate([z, s, pad0], axis=-1)
    s2 = jnp.concatenate([-s, z, pad0], axis=-1)
    return tuple(jnp.tile(a, (1, 2)) for a in (cc, s1, s2))


def _gqa_head_order():
    g = GQA_Q_HEADS // GQA_KV_HEADS
    return [h for j in range(g) for h in (j, g + j)]


def _mixer_even(x, groups, tabs, w_in, rpb, q_gain, k_gain, w_out):
    hd = HEAD_DIM
    na_w = NA_HEADS * hd
    order = _gqa_head_order()
    q0 = 3 * na_w
    q_cols = np.concatenate([q0 + h * hd + np.arange(hd) for h in order])
    w_in_p = jnp.concatenate([w_in[:, :q0], w_in[:, q_cols], w_in[:, q0 + GQA_Q_HEADS * hd:]], axis=1).astype(BF16)
    out_rows = np.concatenate([na_w + h * hd + np.arange(hd) for h in order])
    w_out_p = jnp.concatenate([w_out[:na_w], w_out[out_rows]], axis=0).astype(BF16)
    n_na = na_w // LANES
    n_q = GQA_Q_HEADS * hd // LANES
    modes = ([("plain", Q_SCALE, 0)] * n_na + [("plain", 1.0, 0)] * (2 * n_na)
             + [("norm_rope", Q_SCALE, 0)] * n_q + [("norm_rope", 1.0, 1)] + [("plain", 1.0, 0)])
    gains = jnp.stack([jnp.tile(q_gain.astype(F32), 2), jnp.tile(k_gain.astype(F32), 2)])
    proj = _project(x, w_in_p, tabs, gains, modes, HEAD_DIM // 4)
    cc = _na_bias_table(rpb)
    ya, yb = [], []
    row0 = 0
    for b, s in groups:
        ya.append(_na_attention(proj, cc, row0, b, s))
        yb.append(_gqa_attention(proj, row0, b, s))
        row0 += b * s
    attn = jnp.concatenate([jnp.concatenate(ya, axis=0), jnp.concatenate(yb, axis=0)], axis=1)
    return attn, w_out_p


def _mixer_odd(x, groups, tabs, w_in, w_out):
    n_slab = DIL_HEADS * HEAD_DIM // LANES
    modes = [("rope", Q_SCALE, 0)] * n_slab + [("rope", 1.0, 0)] * n_slab + [("plain", 1.0, 0)] * n_slab
    gains = jnp.ones((1, LANES), F32)
    qkv = _project(x, w_in.astype(BF16), tabs, gains, modes, ROPE_DIMS // 2)
    width = qkv.shape[1]
    os_, lses = [], []
    for _, dil in DIL_BRANCHES:
        o_parts, l_parts = [], []
        row0 = 0
        for b, s in groups:
            length = s // dil
            part = qkv[row0:row0 + b * s].reshape(b, length, dil, width).transpose(0, 2, 1, 3)
            o, lse = _window_attention(part.reshape(b * dil, length, width))

            def back(t):
                return t.reshape(b, dil, length, -1).transpose(0, 2, 1, 3).reshape(b * s, -1)

            o_parts.append(back(o))
            l_parts.append(back(lse))
            row0 += b * s
        os_.append(jnp.concatenate(o_parts, axis=0))
        lses.append(jnp.concatenate(l_parts, axis=0))
    return _merge_branches(os_, lses), w_out.astype(BF16)


def _trunk(xs, w_in_even, rpb_a, q_gain_b, k_gain_b, w_out_even, w_in_odd, w_out_odd,
           ln1_g, ln1_b, ln2_g, ln2_b, router_w, router_b, moe_w1, moe_b1, moe_w2, moe_b2):
    groups = [(x.shape[0], x.shape[1]) for x in xs]
    x = jnp.concatenate([t.reshape(-1, D_MODEL) for t in xs], axis=0).astype(F32)
    tabs_even = _axial_tables(groups)
    tabs_odd = _rope_tables(groups)
    for l in range(DEPTH):
        i = l // 2
        if l % 2 == 0:
            attn, w_out = _mixer_even(x, groups, tabs_even, w_in_even[i], rpb_a[i], q_gain_b[i], k_gain_b[i],
                                      w_out_even[i])
        else:
            attn, w_out = _mixer_odd(x, groups, tabs_odd, w_in_odd[i], w_out_odd[i])
        x = _outproj_ln(attn, w_out, x, ln1_g[l], ln1_b[l])
        yg, gate = _moe(x, router_w[l], router_b[l], moe_w1[l].astype(BF16), moe_b1[l][:, None, :].astype(F32),
                        moe_w2[l].astype(BF16), moe_b2[l][:, None, :].astype(F32))
        x = _combine_ln(yg, gate, x, ln2_g[l], ln2_b[l])
    outs, row0 = [], 0
    for t in xs:
        n = t.shape[0] * t.shape[1]
        outs.append(x[row0:row0 + n].reshape(t.shape))
        row0 += n
    return tuple(outs)


def kernel(x_prompt, x_sample, w_in_even, rpb_a, q_gain_b, k_gain_b, w_out_even, w_in_odd, w_out_odd, ln1_g, ln1_b,
           ln2_g, ln2_b, router_w, router_b, moe_w1, moe_b1, moe_w2, moe_b2):
    return _trunk((x_prompt, x_sample), w_in_even, rpb_a, q_gain_b, k_gain_b, w_out_even, w_in_odd, w_out_odd,
                  ln1_g, ln1_b, ln2_g, ln2_b, router_w, router_b, moe_w1, moe_b1, moe_w2, moe_b2)
```

```python
import functools

import jax
import jax.numpy as jnp
import numpy as np
from jax import lax
from jax.experimental import pallas as pl
from jax.experimental.pallas import tpu as pltpu
from jax.experimental.pallas import tpu_sc as plsc

F32 = jnp.float32
BF16 = jnp.bfloat16
I32 = jnp.int32

D_MODEL = 1024
DEPTH = 4
HEAD_DIM = 64
GRID_W = 64
NA_HEADS = 8
NA_ROWS = 8
NA_COLS = 16
GQA_Q_HEADS = 8
GQA_KV_HEADS = 2
AXIAL_THETA = 10000.0
QK_NORM_EPS = 1e-6
DIL_HEADS = 16
DIL_BRANCHES = ((128, 1), (512, 4), (2048, 16))
ROPE_THETA = 500000.0
ROPE_DIMS = HEAD_DIM // 4
N_EXPERTS = 32
TOP_K = 4
D_FF = D_MODEL
SWIGLU_LIMIT = 7.0
SWIGLU_ALPHA = 1.702
DN_ALPHA = (2 * DEPTH) ** 0.25
LN_EPS = 1e-5
Q_SCALE = HEAD_DIM ** -0.5

LANES = 128
NEG = -1e30
VMEM_LIMIT = 56 * 1024 * 1024
TOKEN_TILE = 512
EXPERT_ROWS = 256
NA_ROW_BLOCK = 8
GQA_Q_TILE = 256
GQA_K_TILE = 512
WIN_Q_TILE = 128
WIN_RADIUS = 64
SC_INDEX_WINDOW = 128
SC_ROW_WINDOW = 32

EVEN_IN = 3 * NA_HEADS * HEAD_DIM + GQA_Q_HEADS * HEAD_DIM + 2 * GQA_KV_HEADS * HEAD_DIM


def _params(*sem):
    return pltpu.CompilerParams(dimension_semantics=sem, vmem_limit_bytes=VMEM_LIMIT)


def _lane_is_low():
    return lax.broadcasted_iota(I32, (1, LANES), 1) < HEAD_DIM


def _proj_kernel(x_ref, w_ref, c_ref, s1_ref, s2_ref, gain_ref, o_ref, *, slab_modes, shift):
    x = x_ref[...].astype(BF16)
    n_out = o_ref.shape[1]
    chunk = 2 * LANES
    if any(m[0] == "norm_rope" for m in slab_modes):
        r = lax.broadcasted_iota(I32, (LANES, LANES), 0) // HEAD_DIM
        c = lax.broadcasted_iota(I32, (LANES, LANES), 1) // HEAD_DIM
        head_mean = jnp.where(r == c, 1.0 / HEAD_DIM, 0.0).astype(BF16)
    for c0 in range(0, n_out, chunk):
        acc = jnp.dot(x, w_ref[:, c0:c0 + chunk], preferred_element_type=F32)
        for s in range(chunk // LANES):
            slab = c0 // LANES + s
            mode, scale, gidx = slab_modes[slab]
            y = acc[:, s * LANES:(s + 1) * LANES]
            if mode == "norm_rope":
                sq = y * y
                hi = sq.astype(BF16)
                lo = (sq - hi.astype(F32)).astype(BF16)
                ms = (jnp.dot(hi, head_mean, preferred_element_type=F32)
                      + jnp.dot(lo, head_mean, preferred_element_type=F32))
                y = y * lax.rsqrt(ms + QK_NORM_EPS) * gain_ref[gidx:gidx + 1, :]
            if mode in ("rope", "norm_rope"):
                y = (y * c_ref[...] + pltpu.roll(y, LANES - shift, 1) * s2_ref[...]
                     + pltpu.roll(y, shift, 1) * s1_ref[...])
            if scale != 1.0:
                y = y * scale
            o_ref[:, slab * LANES:(slab + 1) * LANES] = y.astype(o_ref.dtype)


def _project(x, w, tabs, gains, slab_modes, shift, out_dtype=BF16):
    n, d = x.shape
    m = w.shape[1]
    tm = TOKEN_TILE
    tab_spec = pl.BlockSpec((tm, LANES), lambda i: (i, 0))
    return pl.pallas_call(
        functools.partial(_proj_kernel, slab_modes=tuple(slab_modes), shift=shift),
        grid=(n // tm,),
        in_specs=[pl.BlockSpec((tm, d), lambda i: (i, 0)),
                  pl.BlockSpec((d, m), lambda i: (0, 0)),
                  tab_spec, tab_spec, tab_spec,
                  pl.BlockSpec(gains.shape, lambda i: (0, 0))],
        out_specs=pl.BlockSpec((tm, m), lambda i: (i, 0)),
        out_shape=jax.ShapeDtypeStruct((n, m), out_dtype),
        compiler_params=_params("parallel"),
        name="in_proj",
    )(x, w, tabs[0], tabs[1], tabs[2], gains)


def _layer_norm_rows(z, g, b):
    mu = jnp.mean(z, axis=-1, keepdims=True)
    zc = z - mu
    var = jnp.mean(zc * zc, axis=-1, keepdims=True)
    return zc * lax.rsqrt(var + LN_EPS) * g + b


def _pack_bf16_pairs(v):
    half = v.shape[1] // 2
    bits = pltpu.bitcast(v.astype(BF16).astype(F32), jnp.uint32)
    return (bits[:, :half] >> 16) | bits[:, half:]


def _unpack_bf16_pairs(w):
    lo = pltpu.bitcast(w << 16, F32).astype(BF16)
    hi = pltpu.bitcast(w & jnp.uint32(0xFFFF0000), F32).astype(BF16)
    return lo, hi


def _outproj_ln_kernel(a_ref, w_ref, x_ref, g_ref, b_ref, o_ref, p_ref):
    y = jnp.dot(a_ref[...], w_ref[...], preferred_element_type=F32)
    out = _layer_norm_rows(DN_ALPHA * x_ref[...] + y, g_ref[...], b_ref[...])
    o_ref[...] = out
    p_ref[...] = _pack_bf16_pairs(out)


def _outproj_ln(a, w, x, g, b):
    n, d = x.shape
    tm = TOKEN_TILE
    row = lambda i: (i, 0)
    fixed = lambda i: (0, 0)
    return pl.pallas_call(
        _outproj_ln_kernel,
        grid=(n // tm,),
        in_specs=[pl.BlockSpec((tm, a.shape[1]), row), pl.BlockSpec(w.shape, fixed),
                  pl.BlockSpec((tm, d), row), pl.BlockSpec((1, d), fixed), pl.BlockSpec((1, d), fixed)],
        out_specs=[pl.BlockSpec((tm, d), row), pl.BlockSpec((tm, d // 2), row)],
        out_shape=[jax.ShapeDtypeStruct((n, d), F32), jax.ShapeDtypeStruct((n, d // 2), jnp.uint32)],
        compiler_params=_params("parallel"),
        name="out_proj_ln",
    )(a, w, x, g.reshape(1, d), b.reshape(1, d))


def _na_kernel(q_ref, k_ref, v_ref, cc_ref, o_ref, *, rows):
    i = pl.program_id(2)
    low = _lane_is_low()

    def one_row(rr, carry):
        r = i * NA_ROW_BLOCK + rr
        rs = jnp.clip(r - NA_ROWS // 2, 0, rows - NA_ROWS)
        var = r - rs
        q = q_ref[pl.ds(pl.multiple_of(rr * GRID_W, GRID_W), GRID_W), :]
        kstart = pl.multiple_of(rs * GRID_W, GRID_W)
        kwin = k_ref[pl.ds(kstart, NA_ROWS * GRID_W), :]
        vwin = v_ref[pl.ds(kstart, NA_ROWS * GRID_W), :]
        outs = []
        for half in range(2):
            qm = jnp.where(low if half == 0 else jnp.logical_not(low), q, jnp.zeros_like(q))
            s = lax.dot_general(qm, kwin, (((1,), (1,)), ((), ())), preferred_element_type=F32)
            s = s + cc_ref[half, var]
            m = jnp.max(s, axis=-1, keepdims=True)
            p = jnp.exp(s - m)
            l = jnp.sum(p, axis=-1, keepdims=True)
            outs.append(jnp.dot(p.astype(BF16), vwin, preferred_element_type=F32) / l)
        o = jnp.where(low, outs[0], outs[1])
        o_ref[pl.ds(pl.multiple_of(rr * GRID_W, GRID_W), GRID_W), :] = o.astype(o_ref.dtype)
        return carry

    lax.fori_loop(0, NA_ROW_BLOCK, one_row, 0)


def _na_attention(proj, cc, row0, b, s):
    rows = s // GRID_W
    assert rows >= NA_ROWS and rows % NA_ROW_BLOCK == 0 and row0 % s == 0
    tq = NA_ROW_BLOCK * GRID_W
    n_slab = NA_HEADS * HEAD_DIM // LANES
    qb0, sb0 = row0 // tq, row0 // s
    return pl.pallas_call(
        functools.partial(_na_kernel, rows=rows),
        grid=(b, n_slab, s // tq),
        in_specs=[pl.BlockSpec((tq, LANES), lambda bi, j, i: (qb0 + bi * (s // tq) + i, j)),
                  pl.BlockSpec((s, LANES), lambda bi, j, i: (sb0 + bi, n_slab + j)),
                  pl.BlockSpec((s, LANES), lambda bi, j, i: (sb0 + bi, 2 * n_slab + j)),
                  pl.BlockSpec((2, NA_ROWS, GRID_W, NA_ROWS * GRID_W), lambda bi, j, i: (j, 0, 0, 0))],
        out_specs=pl.BlockSpec((tq, LANES), lambda bi, j, i: (bi * (s // tq) + i, j)),
        out_shape=jax.ShapeDtypeStruct((b * s, n_slab * LANES), BF16),
        compiler_params=_params("parallel", "parallel", "parallel"),
        name="na_attn",
    )(proj, proj, proj, cc)


def _na_bias_table(rpb):
    var = np.arange(NA_ROWS)[:, None, None, None]
    qc = np.arange(GRID_W)[None, :, None, None]
    j = np.arange(NA_ROWS)[None, None, :, None]
    kc = np.arange(GRID_W)[None, None, None, :]
    row_idx = np.broadcast_to(j - var + NA_ROWS - 1, (NA_ROWS, GRID_W, NA_ROWS, GRID_W))
    col_idx = np.broadcast_to(np.clip(kc - qc + NA_COLS - 1, 0, 2 * NA_COLS - 2), row_idx.shape)
    win = np.clip(qc - NA_COLS // 2, 0, GRID_W - NA_COLS)
    valid = np.broadcast_to((kc >= win) & (kc < win + NA_COLS), row_idx.shape)
    t = rpb.astype(F32)[:, row_idx, col_idx]
    t = jnp.where(valid[None], t, NEG)
    return t.reshape(rpb.shape[0], NA_ROWS, GRID_W, NA_ROWS * GRID_W)


def _gqa_kernel(q_ref, k_ref, v_ref, o_ref, acc_ref, *, n_kt):
    low = _lane_is_low()
    q = q_ref[...]
    qm = (jnp.where(low, q, jnp.zeros_like(q)), jnp.where(low, jnp.zeros_like(q), q))
    tq = q.shape[0]
    acc_ref[...] = jnp.zeros_like(acc_ref)

    def step(kt, carry):
        start = pl.multiple_of(kt * GQA_K_TILE, GQA_K_TILE)
        kb = k_ref[pl.ds(start, GQA_K_TILE), :]
        vb = v_ref[pl.ds(start, GQA_K_TILE), :]
        new = []
        for half in range(2):
            m_old, l_old = carry[half]
            s = lax.dot_general(qm[half], kb, (((1,), (1,)), ((), ())), preferred_element_type=F32)
            m_new = jnp.maximum(m_old, jnp.max(s, axis=-1, keepdims=True))
            alpha = jnp.exp(m_old - m_new)
            p = jnp.exp(s - m_new)
            l_new = alpha * l_old + jnp.sum(p, axis=-1, keepdims=True)
            acc_ref[half] = alpha * acc_ref[half] + jnp.dot(p.astype(BF16), vb, preferred_element_type=F32)
            new.append((m_new, l_new))
        return tuple(new)

    init = tuple((jnp.full((tq, 1), -jnp.inf, F32), jnp.zeros((tq, 1), F32)) for _ in range(2))
    (_, l0), (_, l1) = lax.fori_loop(0, n_kt, step, init)
    o_ref[...] = jnp.where(low, acc_ref[0] / l0, acc_ref[1] / l1).astype(o_ref.dtype)


def _gqa_attention(proj, row0, b, s):
    tq = GQA_Q_TILE
    assert s % GQA_K_TILE == 0 and row0 % s == 0
    n_slab = GQA_Q_HEADS * HEAD_DIM // LANES
    q_col = 3 * NA_HEADS * HEAD_DIM // LANES
    k_col = q_col + n_slab
    qb0, sb0 = row0 // tq, row0 // s
    return pl.pallas_call(
        functools.partial(_gqa_kernel, n_kt=s // GQA_K_TILE),
        grid=(b, n_slab, s // tq),
        in_specs=[pl.BlockSpec((tq, LANES), lambda bi, j, i: (qb0 + bi * (s // tq) + i, q_col + j)),
                  pl.BlockSpec((s, LANES), lambda bi, j, i: (sb0 + bi, k_col)),
                  pl.BlockSpec((s, LANES), lambda bi, j, i: (sb0 + bi, k_col + 1))],
        out_specs=pl.BlockSpec((tq, LANES), lambda bi, j, i: (bi * (s // tq) + i, j)),
        out_shape=jax.ShapeDtypeStruct((b * s, n_slab * LANES), BF16),
        scratch_shapes=[pltpu.VMEM((2, tq, LANES), F32)],
        compiler_params=_params("parallel", "parallel", "parallel"),
        name="gqa_attn",
    )(proj, proj, proj)


def _dil_kernel(q_ref, k_ref, v_ref, o_ref, m_ref, l_ref, acc_ref, *, seq):
    tile = q_ref.shape[0]
    t0 = pl.program_id(2) * tile
    low = _lane_is_low()
    sub = WIN_Q_TILE
    n_sub = tile // sub

    for bi, (_, r) in enumerate(DIL_BRANCHES):
        length = seq // r
        win = min(sub + 2 * WIN_RADIUS, length)

        def sub_block(n, carry, r=r, length=length, win=win, first=(bi == 0)):
            c = n % r
            blk = n // r
            u0 = t0 // r + blk * sub
            ks = jnp.clip(u0 - WIN_RADIUS, 0, length - win)
            if r == 1:
                qrow = pl.multiple_of(n * sub, sub)
                q_rows = pl.ds(qrow, sub)
                k_rows = pl.ds(pl.multiple_of(ks, WIN_RADIUS), win)
            else:
                q_rows = pl.ds(c + r * blk * sub, sub, stride=r)
                k_rows = pl.ds(c + r * ks, win, stride=r)
            q = q_ref[q_rows, :].astype(BF16)
            kwin = k_ref[k_rows, :].astype(BF16)
            vwin = v_ref[k_rows, :].astype(BF16)
            qpos = u0 + lax.broadcasted_iota(I32, (sub, win), 0)
            kpos = ks + lax.broadcasted_iota(I32, (sub, win), 1)
            valid = jnp.abs(kpos - qpos) <= WIN_RADIUS
            ms, ls, pvs = [], [], []
            for half in range(2):
                qm = jnp.where(low if half == 0 else jnp.logical_not(low), q, jnp.zeros_like(q))
                s = lax.dot_general(qm, kwin, (((1,), (1,)), ((), ())), preferred_element_type=F32)
                s = jnp.where(valid, s, NEG)
                m = jnp.max(s, axis=-1, keepdims=True)
                p = jnp.exp(s - m)
                ms.append(m)
                ls.append(jnp.sum(p, axis=-1, keepdims=True))
                pvs.append(jnp.dot(p.astype(BF16), vwin, preferred_element_type=F32))
            m_b = jnp.where(low, ms[0], ms[1])
            l_b = jnp.where(low, ls[0], ls[1])
            pv_b = jnp.where(low, pvs[0], pvs[1])
            if first:
                m_ref[q_rows, :] = m_b
                l_ref[q_rows, :] = l_b
                acc_ref[q_rows, :] = pv_b
            else:
                m_old = m_ref[q_rows, :]
                m_new = jnp.maximum(m_old, m_b)
                a_old = jnp.exp(m_old - m_new)
                a_b = jnp.exp(m_b - m_new)
                l_ref[q_rows, :] = a_old * l_ref[q_rows, :] + a_b * l_b
                acc_ref[q_rows, :] = a_old * acc_ref[q_rows, :] + a_b * pv_b
                m_ref[q_rows, :] = m_new
            return carry

        lax.fori_loop(0, n_sub, sub_block, 0, unroll=4)

    o_ref[...] = (acc_ref[...] / l_ref[...]).astype(o_ref.dtype)


def _dilated_attention(qkv, row0, b, s):
    tile = 16 * WIN_Q_TILE
    assert s % tile == 0 and row0 % s == 0
    n_slab = DIL_HEADS * HEAD_DIM // LANES
    qb0, sb0 = row0 // tile, row0 // s
    state = pltpu.VMEM((tile, LANES), F32)
    return pl.pallas_call(
        functools.partial(_dil_kernel, seq=s),
        grid=(b, n_slab, s // tile),
        in_specs=[pl.BlockSpec((tile, LANES), lambda bi, j, i: (qb0 + bi * (s // tile) + i, j)),
                  pl.BlockSpec((s, LANES), lambda bi, j, i: (sb0 + bi, n_slab + j)),
                  pl.BlockSpec((s, LANES), lambda bi, j, i: (sb0 + bi, 2 * n_slab + j))],
        out_specs=pl.BlockSpec((tile, LANES), lambda bi, j, i: (bi * (s // tile) + i, j)),
        out_shape=jax.ShapeDtypeStruct((b * s, n_slab * LANES), BF16),
        scratch_shapes=[state, state, state],
        compiler_params=_params("parallel", "parallel", "parallel"),
        name="dil_attn",
    )(qkv, qkv, qkv)


def _router_kernel(x_ref, wh_ref, wl_ref, b_ref, idx_ref, gate_ref, rank_ref, cnt_ref, base_ref):
    step = pl.program_id(0)

    @pl.when(step == 0)
    def _():
        base_ref[...] = jnp.zeros_like(base_ref)

    x = x_ref[...]
    xh = x.astype(BF16)
    xl = (x - xh.astype(F32)).astype(BF16)
    nt = (((1,), (1,)), ((), ()))
    logits = (lax.dot_general(wh_ref[...], xh, nt, preferred_element_type=F32)
              + lax.dot_general(wl_ref[...], xh, nt, preferred_element_type=F32)
              + lax.dot_general(wh_ref[...], xl, nt, preferred_element_type=F32)) + b_ref[...]
    tm = x.shape[0]
    eid = lax.broadcasted_iota(I32, (N_EXPERTS, tm), 0)
    vals = logits
    top_v, top_i, hots = [], [], []
    for _ in range(TOP_K):
        m = jnp.max(vals, axis=0, keepdims=True)
        idx = jnp.min(jnp.where(vals == m, eid, N_EXPERTS), axis=0, keepdims=True)
        hot = eid == idx
        top_v.append(m)
        top_i.append(idx)
        hots.append(hot)
        vals = jnp.where(hot, -jnp.inf, vals)
    es = [jnp.exp(v - top_v[0]) for v in top_v]
    den = functools.reduce(jnp.add, es)
    chosen = functools.reduce(jnp.logical_or, hots)
    before = (lax.broadcasted_iota(I32, (tm, tm), 0) < lax.broadcasted_iota(I32, (tm, tm), 1))
    prefix = jnp.dot(chosen.astype(BF16), before.astype(BF16), preferred_element_type=F32) + base_ref[...]
    for k in range(TOP_K):
        idx_ref[k:k + 1, :] = top_i[k]
        gate_ref[k:k + 1, :] = es[k] / den
        rank_ref[k:k + 1, :] = jnp.sum(jnp.where(hots[k], prefix, 0.0), axis=0, keepdims=True).astype(I32)
    base_ref[...] = base_ref[...] + jnp.sum(chosen.astype(F32), axis=1, keepdims=True)
    cnt_ref[...] = base_ref[...].astype(I32)


def _router(x, w_hi_t, w_lo_t, bias):
    n, d = x.shape
    tm = TOKEN_TILE
    tok = pl.BlockSpec((TOP_K, tm), lambda i: (0, i))
    fixed = lambda i: (0, 0)
    return pl.pallas_call(
        _router_kernel,
        grid=(n // tm,),
        in_specs=[pl.BlockSpec((tm, d), lambda i: (i, 0)), pl.BlockSpec((N_EXPERTS, d), fixed),
                  pl.BlockSpec((N_EXPERTS, d), fixed), pl.BlockSpec((N_EXPERTS, 1), fixed)],
        out_specs=[tok, tok, tok, pl.BlockSpec((N_EXPERTS, 1), fixed)],
        out_shape=[jax.ShapeDtypeStruct((TOP_K, n), I32), jax.ShapeDtypeStruct((TOP_K, n), F32),
                   jax.ShapeDtypeStruct((TOP_K, n), I32), jax.ShapeDtypeStruct((N_EXPERTS, 1), I32)],
        scratch_shapes=[pltpu.VMEM((N_EXPERTS, 1), F32)],
        compiler_params=_params("arbitrary"),
        name="router",
    )(x, w_hi_t, w_lo_t, bias.reshape(N_EXPERTS, 1).astype(F32))


def _expert_kernel(blk_exp_ref, n_used_ref, x_ref, w1_ref, b1_ref, w2_ref, b2_ref, o_ref):
    blk = pl.program_id(0)

    @pl.when(blk < n_used_ref[0])
    def _():
        half = D_MODEL // 2
        x_lo, x_hi = _unpack_bf16_pairs(x_ref[...])
        h = (jnp.dot(x_lo, w1_ref[0, :half, :], preferred_element_type=F32)
             + jnp.dot(x_hi, w1_ref[0, half:, :], preferred_element_type=F32)) + b1_ref[0]
        g = jnp.minimum(h[:, :D_FF], SWIGLU_LIMIT)
        u = jnp.clip(h[:, D_FF:], -SWIGLU_LIMIT, SWIGLU_LIMIT)
        act = g * jax.nn.sigmoid(SWIGLU_ALPHA * g) * (u + 1.0)
        o_ref[...] = jnp.dot(act.astype(BF16), w2_ref[0], preferred_element_type=F32) + b2_ref[0]

    @pl.when(blk >= n_used_ref[0])
    def _():
        o_ref[...] = jnp.zeros_like(o_ref)


def _experts(xs, blk_exp, n_used, w1, b1, w2, b2):
    n_rows = xs.shape[0]
    d = D_MODEL
    bm = EXPERT_ROWS
    f2 = w1.shape[2]
    return pl.pallas_call(
        _expert_kernel,
        grid_spec=pltpu.PrefetchScalarGridSpec(
            num_scalar_prefetch=2,
            grid=(n_rows // bm,),
            in_specs=[pl.BlockSpec((bm, d // 2), lambda i, be, nu: (i, 0)),
                      pl.BlockSpec((1, d, f2), lambda i, be, nu: (be[i], 0, 0)),
                      pl.BlockSpec((1, 1, f2), lambda i, be, nu: (be[i], 0, 0)),
                      pl.BlockSpec((1, f2 // 2, d), lambda i, be, nu: (be[i], 0, 0)),
                      pl.BlockSpec((1, 1, d), lambda i, be, nu: (be[i], 0, 0))],
            out_specs=pl.BlockSpec((bm, d), lambda i, be, nu: (i, 0)),
        ),
        out_shape=jax.ShapeDtypeStruct((n_rows, d), F32),
        compiler_params=_params("arbitrary"),
        name="experts",
    )(blk_exp, n_used, xs, w1, b1, w2, b2)


def _combine_ln_kernel(y_ref, gate_ref, x_ref, g_ref, b_ref, o_ref):
    gate = gate_ref[...]
    y = functools.reduce(jnp.add, [gate[:, k:k + 1] * y_ref[k] for k in range(TOP_K)])
    o_ref[...] = _layer_norm_rows(DN_ALPHA * x_ref[...] + y, g_ref[...], b_ref[...])


def _combine_ln(yg, gate, x, g, b):
    n, d = x.shape
    tm = TOKEN_TILE
    row = lambda i: (i, 0)
    fixed = lambda i: (0, 0)
    return pl.pallas_call(
        _combine_ln_kernel,
        grid=(n // tm,),
        in_specs=[pl.BlockSpec((TOP_K, tm, d), lambda i: (0, i, 0)), pl.BlockSpec((tm, TOP_K), row),
                  pl.BlockSpec((tm, d), row), pl.BlockSpec((1, d), fixed), pl.BlockSpec((1, d), fixed)],
        out_specs=pl.BlockSpec((tm, d), row),
        out_shape=jax.ShapeDtypeStruct((n, d), F32),
        compiler_params=_params("parallel"),
        name="combine_ln",
    )(yg, gate, x, g.reshape(1, d), b.reshape(1, d))


def _sc_mesh():
    return plsc.VectorSubcoreMesh(core_axis_name="core", subcore_axis_name="subcore")


def _sc_scatter_rows(x, dest, n_out):
    n, d = x.shape
    mesh = _sc_mesh()
    per_worker = n // (mesh.num_cores * mesh.num_subcores)
    n_sub = SC_INDEX_WINDOW // SC_ROW_WINDOW
    assert per_worker % SC_INDEX_WINDOW == 0 and n_sub >= 2

    @functools.partial(
        pl.kernel, out_type=jax.ShapeDtypeStruct((n_out, d), x.dtype), mesh=mesh,
        scratch_types=[pltpu.VMEM((TOP_K, SC_INDEX_WINDOW), I32), pltpu.VMEM((2, SC_ROW_WINDOW, d), x.dtype),
                       pltpu.SemaphoreType.DMA((2,)), pltpu.SemaphoreType.DMA((2,))])
    def scatter(x_hbm, i_hbm, o_hbm, idx_v, buf, sem_r, sem_s):
        wid = lax.axis_index("core") * mesh.num_subcores + lax.axis_index("subcore")

        @pl.loop(0, per_worker // SC_INDEX_WINDOW)
        def _(it):
            base = wid * per_worker + it * SC_INDEX_WINDOW
            for k in range(TOP_K):
                pltpu.sync_copy(i_hbm.at[k, pl.ds(base, SC_INDEX_WINDOW)], idx_v.at[k])

            def read(j):
                rows = pl.ds(base + j * SC_ROW_WINDOW, SC_ROW_WINDOW)
                return pltpu.make_async_copy(x_hbm.at[rows], buf.at[j % 2], sem_r.at[j % 2])

            def send(j, k):
                rows = idx_v.at[k, pl.ds(j * SC_ROW_WINDOW, SC_ROW_WINDOW)]
                return pltpu.make_async_copy(buf.at[j % 2], o_hbm.at[rows], sem_s.at[j % 2])

            read(0).start()
            for j in range(n_sub):
                read(j).wait()
                for k in range(TOP_K):
                    send(j, k).start()
                if j + 1 < n_sub:
                    if j >= 1:
                        for k in range(TOP_K):
                            send(j - 1, k).wait()
                    read(j + 1).start()
            for j in (n_sub - 2, n_sub - 1):
                for k in range(TOP_K):
                    send(j, k).wait()

    return scatter(x, dest)


def _sc_gather_rows(table, indices):
    num = indices.shape[0]
    d = table.shape[1]
    mesh = _sc_mesh()
    per_worker = num // (mesh.num_cores * mesh.num_subcores)
    n_sub = SC_INDEX_WINDOW // SC_ROW_WINDOW
    assert per_worker % SC_INDEX_WINDOW == 0 and n_sub >= 2

    @functools.partial(
        pl.kernel, out_type=jax.ShapeDtypeStruct((num, d), table.dtype), mesh=mesh,
        scratch_types=[pltpu.VMEM((SC_INDEX_WINDOW,), I32), pltpu.VMEM((2, SC_ROW_WINDOW, d), table.dtype),
                       pltpu.SemaphoreType.DMA((2,)), pltpu.SemaphoreType.DMA((2,))])
    def gather(x_hbm, i_hbm, o_hbm, idx_v, buf, sem_g, sem_w):
        wid = lax.axis_index("core") * mesh.num_subcores + lax.axis_index("subcore")

        @pl.loop(0, per_worker // SC_INDEX_WINDOW)
        def _(it):
            base = wid * per_worker + it * SC_INDEX_WINDOW
            pltpu.sync_copy(i_hbm.at[pl.ds(base, SC_INDEX_WINDOW)], idx_v)

            def fetch(j):
                rows = idx_v.at[pl.ds(j * SC_ROW_WINDOW, SC_ROW_WINDOW)]
                return pltpu.make_async_copy(x_hbm.at[rows], buf.at[j % 2], sem_g.at[j % 2])

            def write(j):
                rows = pl.ds(base + j * SC_ROW_WINDOW, SC_ROW_WINDOW)
                return pltpu.make_async_copy(buf.at[j % 2], o_hbm.at[rows], sem_w.at[j % 2])

            fetch(0).start()
            for j in range(n_sub):
                fetch(j).wait()
                write(j).start()
                if j + 1 < n_sub:
                    if j >= 1:
                        write(j - 1).wait()
                    fetch(j + 1).start()
            write(n_sub - 2).wait()
            write(n_sub - 1).wait()

    return gather(table, indices)


def _moe(x, x_packed, w_r, b_r, w1, b1, w2, b2):
    n, d = x.shape
    bm = EXPERT_ROWS
    wr_t = w_r.T
    wr_hi = wr_t.astype(BF16)
    wr_lo = (wr_t - wr_hi.astype(F32)).astype(BF16)
    idx, gate, rank, counts = _router(x, wr_hi, wr_lo, b_r)
    counts = counts[:, 0]
    padded = (counts + bm - 1) // bm * bm
    pad_ends = jnp.cumsum(padded)
    pad_starts = pad_ends - padded
    dest = pad_starts[idx] + rank
    n_blocks = n * TOP_K // bm + N_EXPERTS
    blk_start = jnp.arange(n_blocks, dtype=I32) * bm
    blk_exp = jnp.minimum(jnp.sum(blk_start[:, None] >= pad_ends[None, :], axis=1), N_EXPERTS - 1).astype(I32)
    n_used = (pad_ends[-1:] // bm).astype(I32)
    xs = _sc_scatter_rows(x_packed, dest, n_blocks * bm)
    ys = _experts(xs, blk_exp, n_used, w1, b1, w2, b2)
    yg = _sc_gather_rows(ys, dest.reshape(-1)).reshape(TOP_K, n, d)
    return yg, gate.T


def _positions(groups):
    return jnp.concatenate([jnp.tile(jnp.arange(s), b) for b, s in groups])


def _axial_tables(groups):
    t = _positions(groups)
    n = HEAD_DIM // 4
    inv = AXIAL_THETA ** (-jnp.arange(n, dtype=F32) / n)
    ar = (t // GRID_W).astype(F32)[:, None] * inv
    ac = (t % GRID_W).astype(F32)[:, None] * inv
    z = jnp.zeros_like(ar)
    cr, sr, cc, sc = jnp.cos(ar), jnp.sin(ar), jnp.cos(ac), jnp.sin(ac)
    c = jnp.concatenate([cr, cr, cc, cc], axis=-1)
    s1 = jnp.concatenate([z, sr, z, sc], axis=-1)
    s2 = jnp.concatenate([-sr, z, -sc, z], axis=-1)
    return tuple(jnp.tile(a, (1, 2)) for a in (c, s1, s2))


def _rope_tables(groups):
    t = _positions(groups)
    n = ROPE_DIMS // 2
    inv = ROPE_THETA ** (-jnp.arange(n, dtype=F32) / n)
    ang = t.astype(F32)[:, None] * inv
    c, s = jnp.cos(ang), jnp.sin(ang)
    z = jnp.zeros_like(c)
    rest = HEAD_DIM - ROPE_DIMS
    pad1 = jnp.ones((t.shape[0], rest), F32)
    pad0 = jnp.zeros((t.shape[0], rest), F32)
    cc = jnp.concatenate([c, c, pad1], axis=-1)
    s1 = jnp.concatenate([z, s, pad0], axis=-1)
    s2 = jnp.concatenate([-s, z, pad0], axis=-1)
    return tuple(jnp.tile(a, (1, 2)) for a in (cc, s1, s2))


def _gqa_head_order():
    g = GQA_Q_HEADS // GQA_KV_HEADS
    return [h for j in range(g) for h in (j, g + j)]


def _mixer_even(x, groups, tabs, w_in, rpb, q_gain, k_gain, w_out):
    hd = HEAD_DIM
    na_w = NA_HEADS * hd
    order = _gqa_head_order()
    q0 = 3 * na_w
    q_cols = np.concatenate([q0 + h * hd + np.arange(hd) for h in order])
    w_in_p = jnp.concatenate([w_in[:, :q0], w_in[:, q_cols], w_in[:, q0 + GQA_Q_HEADS * hd:]], axis=1).astype(BF16)
    out_rows = np.concatenate([na_w + h * hd + np.arange(hd) for h in order])
    w_out_p = jnp.concatenate([w_out[:na_w], w_out[out_rows]], axis=0).astype(BF16)
    n_na = na_w // LANES
    n_q = GQA_Q_HEADS * hd // LANES
    modes = ([("plain", Q_SCALE, 0)] * n_na + [("plain", 1.0, 0)] * (2 * n_na)
             + [("norm_rope", Q_SCALE, 0)] * n_q + [("norm_rope", 1.0, 1)] + [("plain", 1.0, 0)])
    gains = jnp.stack([jnp.tile(q_gain.astype(F32), 2), jnp.tile(k_gain.astype(F32), 2)])
    proj = _project(x, w_in_p, tabs, gains, modes, HEAD_DIM // 4)
    cc = _na_bias_table(rpb)
    ya, yb = [], []
    row0 = 0
    for b, s in groups:
        ya.append(_na_attention(proj, cc, row0, b, s))
        yb.append(_gqa_attention(proj, row0, b, s))
        row0 += b * s
    attn = jnp.concatenate([jnp.concatenate(ya, axis=0), jnp.concatenate(yb, axis=0)], axis=1)
    return attn, w_out_p


def _mixer_odd(x, groups, tabs, w_in, w_out):
    n_slab = DIL_HEADS * HEAD_DIM // LANES
    modes = [("rope", Q_SCALE, 0)] * n_slab + [("rope", 1.0, 0)] * n_slab + [("plain", 1.0, 0)] * n_slab
    gains = jnp.ones((1, LANES), F32)
    qkv = _project(x, w_in.astype(BF16), tabs, gains, modes, ROPE_DIMS // 2, out_dtype=F32)
    parts, row0 = [], 0
    for b, s in groups:
        parts.append(_dilated_attention(qkv, row0, b, s))
        row0 += b * s
    return jnp.concatenate(parts, axis=0), w_out.astype(BF16)


def _trunk(xs, w_in_even, rpb_a, q_gain_b, k_gain_b, w_out_even, w_in_odd, w_out_odd,
           ln1_g, ln1_b, ln2_g, ln2_b, router_w, router_b, moe_w1, moe_b1, moe_w2, moe_b2):
    groups = [(x.shape[0], x.shape[1]) for x in xs]
    x = jnp.concatenate([t.reshape(-1, D_MODEL) for t in xs], axis=0).astype(F32)
    tabs_even = _axial_tables(groups)
    tabs_odd = _rope_tables(groups)
    for l in range(DEPTH):
        i = l // 2
        if l % 2 == 0:
            attn, w_out = _mixer_even(x, groups, tabs_even, w_in_even[i], rpb_a[i], q_gain_b[i], k_gain_b[i],
                                      w_out_even[i])
        else:
            attn, w_out = _mixer_odd(x, groups, tabs_odd, w_in_odd[i], w_out_odd[i])
        x, x_packed = _outproj_ln(attn, w_out, x, ln1_g[l], ln1_b[l])
        yg, gate = _moe(x, x_packed, router_w[l], router_b[l], moe_w1[l].astype(BF16), moe_b1[l][:, None, :].astype(F32),
                        moe_w2[l].astype(BF16), moe_b2[l][:, None, :].astype(F32))
        x = _combine_ln(yg, gate, x, ln2_g[l], ln2_b[l])
    outs, row0 = [], 0
    for t in xs:
        n = t.shape[0] * t.shape[1]
        outs.append(x[row0:row0 + n].reshape(t.shape))
        row0 += n
    return tuple(outs)


def kernel(x_prompt, x_sample, w_in_even, rpb_a, q_gain_b, k_gain_b, w_out_even, w_in_odd, w_out_odd, ln1_g, ln1_b,
           ln2_g, ln2_b, router_w, router_b, moe_w1, moe_b1, moe_w2, moe_b2):
    return _trunk((x_prompt, x_sample), w_in_even, rpb_a, q_gain_b, k_gain_b, w_out_even, w_in_odd, w_out_odd,
                  ln1_g, ln1_b, ln2_g, ln2_b, router_w, router_b, moe_w1, moe_b1, moe_w2, moe_b2)
```

```python
import functools

import jax
import jax.numpy as jnp
import numpy as np
from jax import lax
from jax.experimental import pallas as pl
from jax.experimental.pallas import tpu as pltpu
from jax.experimental.pallas import tpu_sc as plsc

F32 = jnp.float32
BF16 = jnp.bfloat16
I32 = jnp.int32

D_MODEL = 1024
DEPTH = 4
HEAD_DIM = 64
GRID_W = 64
NA_HEADS = 8
NA_ROWS = 8
NA_COLS = 16
GQA_Q_HEADS = 8
GQA_KV_HEADS = 2
AXIAL_THETA = 10000.0
QK_NORM_EPS = 1e-6
DIL_HEADS = 16
DIL_BRANCHES = ((128, 1), (512, 4), (2048, 16))
ROPE_THETA = 500000.0
ROPE_DIMS = HEAD_DIM // 4
N_EXPERTS = 32
TOP_K = 4
D_FF = D_MODEL
SWIGLU_LIMIT = 7.0
SWIGLU_ALPHA = 1.702
DN_ALPHA = (2 * DEPTH) ** 0.25
LN_EPS = 1e-5
Q_SCALE = HEAD_DIM ** -0.5

LANES = 128
NEG = -1e30
VMEM_LIMIT = 56 * 1024 * 1024
TOKEN_TILE = 512
EXPERT_ROWS = 256
NA_ROW_BLOCK = 8
NA_ROW_GROUP = 4
GQA_Q_TILE = 256
GQA_K_TILE = 512
WIN_Q_TILE = 128
WIN_RADIUS = 64
DIL_GROUP = 4
SC_INDEX_WINDOW = 128
SC_ROW_WINDOW = 32

EVEN_IN = 3 * NA_HEADS * HEAD_DIM + GQA_Q_HEADS * HEAD_DIM + 2 * GQA_KV_HEADS * HEAD_DIM


def _params(*sem):
    return pltpu.CompilerParams(dimension_semantics=sem, vmem_limit_bytes=VMEM_LIMIT)


def _lane_is_low():
    return lax.broadcasted_iota(I32, (1, LANES), 1) < HEAD_DIM


def _proj_kernel(x_ref, w_ref, c_ref, s1_ref, s2_ref, gain_ref, o_ref, *, slab_modes, shift):
    x = x_ref[...].astype(BF16)
    n_out = o_ref.shape[1]
    chunk = 2 * LANES
    if any(m[0] == "norm_rope" for m in slab_modes):
        r = lax.broadcasted_iota(I32, (LANES, LANES), 0) // HEAD_DIM
        c = lax.broadcasted_iota(I32, (LANES, LANES), 1) // HEAD_DIM
        head_mean = jnp.where(r == c, 1.0 / HEAD_DIM, 0.0).astype(BF16)
    for c0 in range(0, n_out, chunk):
        acc = jnp.dot(x, w_ref[:, c0:c0 + chunk], preferred_element_type=F32)
        for s in range(chunk // LANES):
            slab = c0 // LANES + s
            mode, scale, gidx = slab_modes[slab]
            y = acc[:, s * LANES:(s + 1) * LANES]
            if mode == "norm_rope":
                sq = y * y
                hi = sq.astype(BF16)
                lo = (sq - hi.astype(F32)).astype(BF16)
                ms = (jnp.dot(hi, head_mean, preferred_element_type=F32)
                      + jnp.dot(lo, head_mean, preferred_element_type=F32))
                y = y * lax.rsqrt(ms + QK_NORM_EPS) * gain_ref[gidx:gidx + 1, :]
            if mode in ("rope", "norm_rope"):
                y = (y * c_ref[...] + pltpu.roll(y, LANES - shift, 1) * s2_ref[...]
                     + pltpu.roll(y, shift, 1) * s1_ref[...])
            if scale != 1.0:
                y = y * scale
            o_ref[:, slab * LANES:(slab + 1) * LANES] = y.astype(o_ref.dtype)


def _project(x, w, tabs, gains, slab_modes, shift, out_dtype=BF16):
    n, d = x.shape
    m = w.shape[1]
    tm = TOKEN_TILE
    tab_spec = pl.BlockSpec((tm, LANES), lambda i: (i, 0))
    return pl.pallas_call(
        functools.partial(_proj_kernel, slab_modes=tuple(slab_modes), shift=shift),
        grid=(n // tm,),
        in_specs=[pl.BlockSpec((tm, d), lambda i: (i, 0)),
                  pl.BlockSpec((d, m), lambda i: (0, 0)),
                  tab_spec, tab_spec, tab_spec,
                  pl.BlockSpec(gains.shape, lambda i: (0, 0))],
        out_specs=pl.BlockSpec((tm, m), lambda i: (i, 0)),
        out_shape=jax.ShapeDtypeStruct((n, m), out_dtype),
        compiler_params=_params("parallel"),
        name="in_proj",
    )(x, w, tabs[0], tabs[1], tabs[2], gains)


def _layer_norm_rows(z, g, b):
    mu = jnp.mean(z, axis=-1, keepdims=True)
    zc = z - mu
    var = jnp.mean(zc * zc, axis=-1, keepdims=True)
    return zc * lax.rsqrt(var + LN_EPS) * g + b


def _pack_bf16_pairs(v):
    half = v.shape[1] // 2
    bits = pltpu.bitcast(v.astype(BF16).astype(F32), jnp.uint32)
    return (bits[:, :half] >> 16) | bits[:, half:]


def _unpack_bf16_pairs(w):
    lo = pltpu.bitcast(w << 16, F32).astype(BF16)
    hi = pltpu.bitcast(w & jnp.uint32(0xFFFF0000), F32).astype(BF16)
    return lo, hi


def _outproj_ln_kernel(a_ref, w_ref, x_ref, g_ref, b_ref, o_ref, p_ref):
    y = jnp.dot(a_ref[...], w_ref[...], preferred_element_type=F32)
    out = _layer_norm_rows(DN_ALPHA * x_ref[...] + y, g_ref[...], b_ref[...])
    o_ref[...] = out
    p_ref[...] = _pack_bf16_pairs(out)


def _outproj_ln(a, w, x, g, b):
    n, d = x.shape
    tm = TOKEN_TILE
    row = lambda i: (i, 0)
    fixed = lambda i: (0, 0)
    return pl.pallas_call(
        _outproj_ln_kernel,
        grid=(n // tm,),
        in_specs=[pl.BlockSpec((tm, a.shape[1]), row), pl.BlockSpec(w.shape, fixed),
                  pl.BlockSpec((tm, d), row), pl.BlockSpec((1, d), fixed), pl.BlockSpec((1, d), fixed)],
        out_specs=[pl.BlockSpec((tm, d), row), pl.BlockSpec((tm, d // 2), row)],
        out_shape=[jax.ShapeDtypeStruct((n, d), F32), jax.ShapeDtypeStruct((n, d // 2), jnp.uint32)],
        compiler_params=_params("parallel"),
        name="out_proj_ln",
    )(a, w, x, g.reshape(1, d), b.reshape(1, d))


def _na_kernel(q_ref, k_ref, v_ref, cc_ref, o_ref, *, rows):
    i = pl.program_id(2)
    low = _lane_is_low()

    for g0 in range(0, NA_ROW_BLOCK, NA_ROW_GROUP):
        wins, scores = [], []
        for rr in range(g0, g0 + NA_ROW_GROUP):
            r = i * NA_ROW_BLOCK + rr
            rs = jnp.clip(r - NA_ROWS // 2, 0, rows - NA_ROWS)
            var = r - rs
            q = q_ref[rr * GRID_W:(rr + 1) * GRID_W, :]
            kstart = pl.multiple_of(rs * GRID_W, GRID_W)
            kwin = k_ref[pl.ds(kstart, NA_ROWS * GRID_W), :]
            wins.append(kstart)
            for half in range(2):
                qm = jnp.where(low if half == 0 else jnp.logical_not(low), q, jnp.zeros_like(q))
                s = lax.dot_general(qm, kwin, (((1,), (1,)), ((), ())), preferred_element_type=F32)
                scores.append(s + cc_ref[half, var])
        probs = []
        for s in scores:
            p = jnp.exp(s - jnp.max(s, axis=-1, keepdims=True))
            probs.append((p.astype(BF16), jnp.sum(p, axis=-1, keepdims=True)))
        for u, rr in enumerate(range(g0, g0 + NA_ROW_GROUP)):
            vwin = v_ref[pl.ds(wins[u], NA_ROWS * GRID_W), :]
            outs = [jnp.dot(p, vwin, preferred_element_type=F32) / l for p, l in probs[2 * u:2 * u + 2]]
            o_ref[rr * GRID_W:(rr + 1) * GRID_W, :] = jnp.where(low, outs[0], outs[1]).astype(o_ref.dtype)


def _na_attention(proj, cc, row0, b, s):
    rows = s // GRID_W
    assert rows >= NA_ROWS and rows % NA_ROW_BLOCK == 0 and row0 % s == 0
    tq = NA_ROW_BLOCK * GRID_W
    n_slab = NA_HEADS * HEAD_DIM // LANES
    qb0, sb0 = row0 // tq, row0 // s
    return pl.pallas_call(
        functools.partial(_na_kernel, rows=rows),
        grid=(b, n_slab, s // tq),
        in_specs=[pl.BlockSpec((tq, LANES), lambda bi, j, i: (qb0 + bi * (s // tq) + i, j)),
                  pl.BlockSpec((s, LANES), lambda bi, j, i: (sb0 + bi, n_slab + j)),
                  pl.BlockSpec((s, LANES), lambda bi, j, i: (sb0 + bi, 2 * n_slab + j)),
                  pl.BlockSpec((2, NA_ROWS, GRID_W, NA_ROWS * GRID_W), lambda bi, j, i: (j, 0, 0, 0))],
        out_specs=pl.BlockSpec((tq, LANES), lambda bi, j, i: (bi * (s // tq) + i, j)),
        out_shape=jax.ShapeDtypeStruct((b * s, n_slab * LANES), BF16),
        compiler_params=_params("parallel", "parallel", "parallel"),
        name="na_attn",
    )(proj, proj, proj, cc)


def _na_bias_table(rpb):
    var = np.arange(NA_ROWS)[:, None]
    j = np.arange(NA_ROWS)[None, :]
    qc = np.arange(GRID_W)[:, None]
    kc = np.arange(GRID_W)[None, :]
    row_sel = ((j - var + NA_ROWS - 1)[..., None] == np.arange(2 * NA_ROWS - 1)).astype(np.float32)
    col_sel = (np.clip(kc - qc + NA_COLS - 1, 0, 2 * NA_COLS - 2)[..., None]
               == np.arange(2 * NA_COLS - 1)).astype(np.float32)
    win = np.clip(qc - NA_COLS // 2, 0, GRID_W - NA_COLS)
    valid = (kc >= win) & (kc < win + NA_COLS)
    t = jnp.einsum("hab,vja,qkb->hvqjk", rpb.astype(F32), row_sel, col_sel, precision=lax.Precision.HIGHEST)
    t = jnp.where(valid[None, None, :, None, :], t, NEG)
    return t.reshape(rpb.shape[0], NA_ROWS, GRID_W, NA_ROWS * GRID_W)


def _gqa_kernel(q_ref, k_ref, vt_ref, o_ref, acc_ref, st_ref, *, n_kt):
    assert n_kt % 2 == 0
    low = _lane_is_low()
    q = q_ref[...]
    tq = q.shape[0]
    zero = jnp.zeros_like(q)
    qs = jnp.concatenate([jnp.where(low, q, zero), jnp.where(low, zero, q)], axis=0)
    acc_ref[...] = jnp.zeros_like(acc_ref)

    def scores(kt, slot):
        start = pl.multiple_of(kt * GQA_K_TILE, GQA_K_TILE)
        kb = k_ref[pl.ds(start, GQA_K_TILE), :]
        st_ref[slot] = lax.dot_general(kb, qs, (((1,), (1,)), ((), ())), preferred_element_type=F32)

    def softmax_pv(kt, slot, m_old, l_old):
        st = st_ref[slot]
        m_new = jnp.maximum(m_old, jnp.max(st, axis=0, keepdims=True))
        alpha = jnp.exp(m_old - m_new)
        pt = jnp.exp(st - m_new)
        l_new = alpha * l_old + jnp.sum(pt, axis=0, keepdims=True)
        acc_ref[...] = alpha * acc_ref[...] + jnp.dot(vt_ref[kt], pt.astype(BF16), preferred_element_type=F32)
        return m_new, l_new

    def step(i, carry):
        kt = 2 * i
        scores(kt + 1, 1)
        carry = softmax_pv(kt, 0, *carry)
        scores(jnp.minimum(kt + 2, n_kt - 1), 0)
        return softmax_pv(kt + 1, 1, *carry)

    scores(0, 0)
    init = (jnp.full((1, 2 * tq), -jnp.inf, F32), jnp.zeros((1, 2 * tq), F32))
    _, l = lax.fori_loop(0, n_kt // 2, step, init)
    out_t = acc_ref[...] / l
    o_t = jnp.concatenate([out_t[:HEAD_DIM, :tq], out_t[HEAD_DIM:, tq:]], axis=0)
    o_ref[...] = o_t.T.astype(o_ref.dtype)


def _gqa_attention(proj, row0, b, s):
    tq, tk = GQA_Q_TILE, GQA_K_TILE
    assert s % tk == 0 and row0 % s == 0
    n_slab = GQA_Q_HEADS * HEAD_DIM // LANES
    q_col = 3 * NA_HEADS * HEAD_DIM // LANES
    k_col = q_col + n_slab
    v0 = (k_col + 1) * LANES
    qb0, sb0 = row0 // tq, row0 // s
    v_t = proj[row0:row0 + b * s, v0:v0 + LANES].reshape(b, s // tk, tk, LANES).transpose(0, 1, 3, 2)
    return pl.pallas_call(
        functools.partial(_gqa_kernel, n_kt=s // tk),
        grid=(b, n_slab, s // tq),
        in_specs=[pl.BlockSpec((tq, LANES), lambda bi, j, i: (qb0 + bi * (s // tq) + i, q_col + j)),
                  pl.BlockSpec((s, LANES), lambda bi, j, i: (sb0 + bi, k_col)),
                  pl.BlockSpec((None, s // tk, LANES, tk), lambda bi, j, i: (bi, 0, 0, 0))],
        out_specs=pl.BlockSpec((tq, LANES), lambda bi, j, i: (bi * (s // tq) + i, j)),
        out_shape=jax.ShapeDtypeStruct((b * s, n_slab * LANES), BF16),
        scratch_shapes=[pltpu.VMEM((LANES, 2 * tq), F32), pltpu.VMEM((2, tk, 2 * tq), F32)],
        compiler_params=_params("parallel", "parallel", "parallel"),
        name="gqa_attn",
    )(proj, proj, v_t)


def _dil_kernel(q_ref, k_ref, v_ref, o_ref, m_ref, l_ref, acc_ref, *, seq):
    tile = q_ref.shape[0]
    t0 = pl.program_id(2) * tile
    low = _lane_is_low()
    sub = WIN_Q_TILE
    n_sub = tile // sub

    for bi, (_, r) in enumerate(DIL_BRANCHES):
        length = seq // r
        win = min(sub + 2 * WIN_RADIUS, length)

        def sub_blocks(g, carry, r=r, length=length, win=win, first=(bi == 0)):
            rows, scores, stats = [], [], []
            for u in range(DIL_GROUP):
                n = g * DIL_GROUP + u
                c = n % r
                blk = n // r
                u0 = t0 // r + blk * sub
                ks = jnp.clip(u0 - WIN_RADIUS, 0, length - win)
                if r == 1:
                    q_rows = pl.ds(pl.multiple_of(n * sub, sub), sub)
                    k_rows = pl.ds(pl.multiple_of(ks, WIN_RADIUS), win)
                else:
                    q_rows = pl.ds(c + r * blk * sub, sub, stride=r)
                    k_rows = pl.ds(c + r * ks, win, stride=r)
                rows.append((q_rows, k_rows))
                q = q_ref[q_rows, :].astype(BF16)
                kwin = k_ref[k_rows, :].astype(BF16)
                qpos = u0 + lax.broadcasted_iota(I32, (sub, win), 0)
                kpos = ks + lax.broadcasted_iota(I32, (sub, win), 1)
                valid = jnp.abs(kpos - qpos) <= WIN_RADIUS
                for half in range(2):
                    qm = jnp.where(low if half == 0 else jnp.logical_not(low), q, jnp.zeros_like(q))
                    s = lax.dot_general(qm, kwin, (((1,), (1,)), ((), ())), preferred_element_type=F32)
                    scores.append(jnp.where(valid, s, NEG))
            for s in scores:
                m = jnp.max(s, axis=-1, keepdims=True)
                p = jnp.exp(s - m)
                stats.append((m, jnp.sum(p, axis=-1, keepdims=True), p.astype(BF16)))
            for u, (q_rows, k_rows) in enumerate(rows):
                vwin = v_ref[k_rows, :].astype(BF16)
                (m0, l0, p0), (m1, l1, p1) = stats[2 * u:2 * u + 2]
                m_b = jnp.where(low, m0, m1)
                l_b = jnp.where(low, l0, l1)
                pv_b = jnp.where(low, jnp.dot(p0, vwin, preferred_element_type=F32),
                                 jnp.dot(p1, vwin, preferred_element_type=F32))
                if first:
                    m_ref[q_rows, :] = m_b
                    l_ref[q_rows, :] = l_b
                    acc_ref[q_rows, :] = pv_b
                else:
                    m_old = m_ref[q_rows, :]
                    m_new = jnp.maximum(m_old, m_b)
                    a_old = jnp.exp(m_old - m_new)
                    a_b = jnp.exp(m_b - m_new)
                    l_ref[q_rows, :] = a_old * l_ref[q_rows, :] + a_b * l_b
                    acc_ref[q_rows, :] = a_old * acc_ref[q_rows, :] + a_b * pv_b
                    m_ref[q_rows, :] = m_new
            return carry

        lax.fori_loop(0, n_sub // DIL_GROUP, sub_blocks, 0)

    o_ref[...] = (acc_ref[...] / l_ref[...]).astype(o_ref.dtype)


def _dilated_attention(qkv, row0, b, s):
    tile = 16 * WIN_Q_TILE
    assert s % tile == 0 and row0 % s == 0
    n_slab = DIL_HEADS * HEAD_DIM // LANES
    qb0, sb0 = row0 // tile, row0 // s
    state = pltpu.VMEM((tile, LANES), F32)
    return pl.pallas_call(
        functools.partial(_dil_kernel, seq=s),
        grid=(b, n_slab, s // tile),
        in_specs=[pl.BlockSpec((tile, LANES), lambda bi, j, i: (qb0 + bi * (s // tile) + i, j)),
                  pl.BlockSpec((s, LANES), lambda bi, j, i: (sb0 + bi, n_slab + j)),
                  pl.BlockSpec((s, LANES), lambda bi, j, i: (sb0 + bi, 2 * n_slab + j))],
        out_specs=pl.BlockSpec((tile, LANES), lambda bi, j, i: (bi * (s // tile) + i, j)),
        out_shape=jax.ShapeDtypeStruct((b * s, n_slab * LANES), BF16),
        scratch_shapes=[state, state, state],
        compiler_params=_params("parallel", "parallel", "parallel"),
        name="dil_attn",
    )(qkv, qkv, qkv)


def _router_kernel(x_ref, wh_ref, wl_ref, b_ref, idx_ref, gate_ref, rank_ref, cnt_ref, base_ref):
    step = pl.program_id(0)

    @pl.when(step == 0)
    def _():
        base_ref[...] = jnp.zeros_like(base_ref)

    x = x_ref[...]
    xh = x.astype(BF16)
    xl = (x - xh.astype(F32)).astype(BF16)
    nt = (((1,), (1,)), ((), ()))
    logits = (lax.dot_general(wh_ref[...], xh, nt, preferred_element_type=F32)
              + lax.dot_general(wl_ref[...], xh, nt, preferred_element_type=F32)
              + lax.dot_general(wh_ref[...], xl, nt, preferred_element_type=F32)) + b_ref[...]
    tm = x.shape[0]
    eid = lax.broadcasted_iota(I32, (N_EXPERTS, tm), 0)
    vals = logits
    top_v, top_i, hots = [], [], []
    for _ in range(TOP_K):
        m = jnp.max(vals, axis=0, keepdims=True)
        idx = jnp.min(jnp.where(vals == m, eid, N_EXPERTS), axis=0, keepdims=True)
        hot = eid == idx
        top_v.append(m)
        top_i.append(idx)
        hots.append(hot)
        vals = jnp.where(hot, -jnp.inf, vals)
    es = [jnp.exp(v - top_v[0]) for v in top_v]
    den = functools.reduce(jnp.add, es)
    chosen = functools.reduce(jnp.logical_or, hots)
    before = (lax.broadcasted_iota(I32, (tm, tm), 0) < lax.broadcasted_iota(I32, (tm, tm), 1))
    prefix = jnp.dot(chosen.astype(BF16), before.astype(BF16), preferred_element_type=F32) + base_ref[...]
    for k in range(TOP_K):
        idx_ref[k:k + 1, :] = top_i[k]
        gate_ref[k:k + 1, :] = es[k] / den
        rank_ref[k:k + 1, :] = jnp.sum(jnp.where(hots[k], prefix, 0.0), axis=0, keepdims=True).astype(I32)
    base_ref[...] = base_ref[...] + jnp.sum(chosen.astype(F32), axis=1, keepdims=True)
    cnt_ref[...] = base_ref[...].astype(I32)


def _router(x, w_hi_t, w_lo_t, bias):
    n, d = x.shape
    tm = TOKEN_TILE
    tok = pl.BlockSpec((TOP_K, tm), lambda i: (0, i))
    fixed = lambda i: (0, 0)
    return pl.pallas_call(
        _router_kernel,
        grid=(n // tm,),
        in_specs=[pl.BlockSpec((tm, d), lambda i: (i, 0)), pl.BlockSpec((N_EXPERTS, d), fixed),
                  pl.BlockSpec((N_EXPERTS, d), fixed), pl.BlockSpec((N_EXPERTS, 1), fixed)],
        out_specs=[tok, tok, tok, pl.BlockSpec((N_EXPERTS, 1), fixed)],
        out_shape=[jax.ShapeDtypeStruct((TOP_K, n), I32), jax.ShapeDtypeStruct((TOP_K, n), F32),
                   jax.ShapeDtypeStruct((TOP_K, n), I32), jax.ShapeDtypeStruct((N_EXPERTS, 1), I32)],
        scratch_shapes=[pltpu.VMEM((N_EXPERTS, 1), F32)],
        compiler_params=_params("arbitrary"),
        name="router",
    )(x, w_hi_t, w_lo_t, bias.reshape(N_EXPERTS, 1).astype(F32))


def _expert_kernel(blk_exp_ref, n_used_ref, x_ref, w1_ref, b1_ref, w2_ref, b2_ref, o_ref):
    blk = pl.program_id(0)

    @pl.when(blk < n_used_ref[0])
    def _():
        half = D_MODEL // 2
        x_lo, x_hi = _unpack_bf16_pairs(x_ref[...])
        h = (jnp.dot(x_lo, w1_ref[0, :half, :], preferred_element_type=F32)
             + jnp.dot(x_hi, w1_ref[0, half:, :], preferred_element_type=F32)) + b1_ref[0]
        g = jnp.minimum(h[:, :D_FF], SWIGLU_LIMIT)
        u = jnp.clip(h[:, D_FF:], -SWIGLU_LIMIT, SWIGLU_LIMIT)
        act = g * jax.nn.sigmoid(SWIGLU_ALPHA * g) * (u + 1.0)
        o_ref[...] = jnp.dot(act.astype(BF16), w2_ref[0], preferred_element_type=F32) + b2_ref[0]

    @pl.when(blk >= n_used_ref[0])
    def _():
        o_ref[...] = jnp.zeros_like(o_ref)


def _experts(xs, blk_exp, n_used, w1, b1, w2, b2):
    n_rows = xs.shape[0]
    d = D_MODEL
    bm = EXPERT_ROWS
    f2 = w1.shape[2]
    return pl.pallas_call(
        _expert_kernel,
        grid_spec=pltpu.PrefetchScalarGridSpec(
            num_scalar_prefetch=2,
            grid=(n_rows // bm,),
            in_specs=[pl.BlockSpec((bm, d // 2), lambda i, be, nu: (i, 0)),
                      pl.BlockSpec((1, d, f2), lambda i, be, nu: (be[i], 0, 0)),
                      pl.BlockSpec((1, 1, f2), lambda i, be, nu: (be[i], 0, 0)),
                      pl.BlockSpec((1, f2 // 2, d), lambda i, be, nu: (be[i], 0, 0)),
                      pl.BlockSpec((1, 1, d), lambda i, be, nu: (be[i], 0, 0))],
            out_specs=pl.BlockSpec((bm, d), lambda i, be, nu: (i, 0)),
        ),
        out_shape=jax.ShapeDtypeStruct((n_rows, d), F32),
        compiler_params=_params("arbitrary"),
        name="experts",
    )(blk_exp, n_used, xs, w1, b1, w2, b2)


def _combine_ln_kernel(y_ref, gate_ref, x_ref, g_ref, b_ref, o_ref):
    gate = gate_ref[...]
    y = functools.reduce(jnp.add, [gate[:, k:k + 1] * y_ref[k] for k in range(TOP_K)])
    o_ref[...] = _layer_norm_rows(DN_ALPHA * x_ref[...] + y, g_ref[...], b_ref[...])


def _combine_ln(yg, gate, x, g, b):
    n, d = x.shape
    tm = TOKEN_TILE
    row = lambda i: (i, 0)
    fixed = lambda i: (0, 0)
    return pl.pallas_call(
        _combine_ln_kernel,
        grid=(n // tm,),
        in_specs=[pl.BlockSpec((TOP_K, tm, d), lambda i: (0, i, 0)), pl.BlockSpec((tm, TOP_K), row),
                  pl.BlockSpec((tm, d), row), pl.BlockSpec((1, d), fixed), pl.BlockSpec((1, d), fixed)],
        out_specs=pl.BlockSpec((tm, d), row),
        out_shape=jax.ShapeDtypeStruct((n, d), F32),
        compiler_params=_params("parallel"),
        name="combine_ln",
    )(yg, gate, x, g.reshape(1, d), b.reshape(1, d))


def _sc_mesh():
    return plsc.VectorSubcoreMesh(core_axis_name="core", subcore_axis_name="subcore")


def _sc_scatter_rows(x, dest, n_out):
    n, d = x.shape
    mesh = _sc_mesh()
    per_worker = n // (mesh.num_cores * mesh.num_subcores)
    n_sub = SC_INDEX_WINDOW // SC_ROW_WINDOW
    assert per_worker % SC_INDEX_WINDOW == 0 and n_sub >= 2

    @functools.partial(
        pl.kernel, out_type=jax.ShapeDtypeStruct((n_out, d), x.dtype), mesh=mesh,
        scratch_types=[pltpu.VMEM((TOP_K, SC_INDEX_WINDOW), I32), pltpu.VMEM((2, SC_ROW_WINDOW, d), x.dtype),
                       pltpu.SemaphoreType.DMA((2,)), pltpu.SemaphoreType.DMA((2,))])
    def scatter(x_hbm, i_hbm, o_hbm, idx_v, buf, sem_r, sem_s):
        wid = lax.axis_index("core") * mesh.num_subcores + lax.axis_index("subcore")

        @pl.loop(0, per_worker // SC_INDEX_WINDOW)
        def _(it):
            base = wid * per_worker + it * SC_INDEX_WINDOW
            for k in range(TOP_K):
                pltpu.sync_copy(i_hbm.at[k, pl.ds(base, SC_INDEX_WINDOW)], idx_v.at[k])

            def read(j):
                rows = pl.ds(base + j * SC_ROW_WINDOW, SC_ROW_WINDOW)
                return pltpu.make_async_copy(x_hbm.at[rows], buf.at[j % 2], sem_r.at[j % 2])

            def send(j, k):
                rows = idx_v.at[k, pl.ds(j * SC_ROW_WINDOW, SC_ROW_WINDOW)]
                return pltpu.make_async_copy(buf.at[j % 2], o_hbm.at[rows], sem_s.at[j % 2])

            read(0).start()
            for j in range(n_sub):
                read(j).wait()
                for k in range(TOP_K):
                    send(j, k).start()
                if j + 1 < n_sub:
                    if j >= 1:
                        for k in range(TOP_K):
                            send(j - 1, k).wait()
                    read(j + 1).start()
            for j in (n_sub - 2, n_sub - 1):
                for k in range(TOP_K):
                    send(j, k).wait()

    return scatter(x, dest)


def _sc_gather_rows(table, indices):
    num = indices.shape[0]
    d = table.shape[1]
    mesh = _sc_mesh()
    per_worker = num // (mesh.num_cores * mesh.num_subcores)
    n_sub = SC_INDEX_WINDOW // SC_ROW_WINDOW
    assert per_worker % SC_INDEX_WINDOW == 0 and n_sub >= 2

    @functools.partial(
        pl.kernel, out_type=jax.ShapeDtypeStruct((num, d), table.dtype), mesh=mesh,
        scratch_types=[pltpu.VMEM((SC_INDEX_WINDOW,), I32), pltpu.VMEM((2, SC_ROW_WINDOW, d), table.dtype),
                       pltpu.SemaphoreType.DMA((2,)), pltpu.SemaphoreType.DMA((2,))])
    def gather(x_hbm, i_hbm, o_hbm, idx_v, buf, sem_g, sem_w):
        wid = lax.axis_index("core") * mesh.num_subcores + lax.axis_index("subcore")

        @pl.loop(0, per_worker // SC_INDEX_WINDOW)
        def _(it):
            base = wid * per_worker + it * SC_INDEX_WINDOW
            pltpu.sync_copy(i_hbm.at[pl.ds(base, SC_INDEX_WINDOW)], idx_v)

            def fetch(j):
                rows = idx_v.at[pl.ds(j * SC_ROW_WINDOW, SC_ROW_WINDOW)]
                return pltpu.make_async_copy(x_hbm.at[rows], buf.at[j % 2], sem_g.at[j % 2])

            def write(j):
                rows = pl.ds(base + j * SC_ROW_WINDOW, SC_ROW_WINDOW)
                return pltpu.make_async_copy(buf.at[j % 2], o_hbm.at[rows], sem_w.at[j % 2])

            fetch(0).start()
            for j in range(n_sub):
                fetch(j).wait()
                write(j).start()
                if j + 1 < n_sub:
                    if j >= 1:
                        write(j - 1).wait()
                    fetch(j + 1).start()
            write(n_sub - 2).wait()
            write(n_sub - 1).wait()

    return gather(table, indices)


def _moe(x, x_packed, w_r, b_r, w1, b1, w2, b2):
    n, d = x.shape
    bm = EXPERT_ROWS
    wr_t = w_r.T
    wr_hi = wr_t.astype(BF16)
    wr_lo = (wr_t - wr_hi.astype(F32)).astype(BF16)
    idx, gate, rank, counts = _router(x, wr_hi, wr_lo, b_r)
    counts = counts[:, 0]
    padded = (counts + bm - 1) // bm * bm
    pad_ends = jnp.cumsum(padded)
    pad_starts = pad_ends - padded
    hot = idx[:, None, :] == jnp.arange(N_EXPERTS, dtype=I32)[None, :, None]
    dest = jnp.sum(jnp.where(hot, pad_starts[None, :, None], 0), axis=1) + rank
    n_blocks = n * TOP_K // bm + N_EXPERTS
    blk_start = jnp.arange(n_blocks, dtype=I32) * bm
    blk_exp = jnp.minimum(jnp.sum(blk_start[:, None] >= pad_ends[None, :], axis=1), N_EXPERTS - 1).astype(I32)
    n_used = (pad_ends[-1:] // bm).astype(I32)
    xs = _sc_scatter_rows(x_packed, dest, n_blocks * bm)
    ys = _experts(xs, blk_exp, n_used, w1, b1, w2, b2)
    yg = _sc_gather_rows(ys, dest.reshape(-1)).reshape(TOP_K, n, d)
    return yg, gate.T


def _positions(groups):
    return jnp.concatenate([jnp.tile(jnp.arange(s), b) for b, s in groups])


def _axial_tables(groups):
    t = _positions(groups)
    n = HEAD_DIM // 4
    inv = AXIAL_THETA ** (-jnp.arange(n, dtype=F32) / n)
    ar = (t // GRID_W).astype(F32)[:, None] * inv
    ac = (t % GRID_W).astype(F32)[:, None] * inv
    z = jnp.zeros_like(ar)
    cr, sr, cc, sc = jnp.cos(ar), jnp.sin(ar), jnp.cos(ac), jnp.sin(ac)
    c = jnp.concatenate([cr, cr, cc, cc], axis=-1)
    s1 = jnp.concatenate([z, sr, z, sc], axis=-1)
    s2 = jnp.concatenate([-sr, z, -sc, z], axis=-1)
    return tuple(jnp.tile(a, (1, 2)) for a in (c, s1, s2))


def _rope_tables(groups):
    t = _positions(groups)
    n = ROPE_DIMS // 2
    inv = ROPE_THETA ** (-jnp.arange(n, dtype=F32) / n)
    ang = t.astype(F32)[:, None] * inv
    c, s = jnp.cos(ang), jnp.sin(ang)
    z = jnp.zeros_like(c)
    rest = HEAD_DIM - ROPE_DIMS
    pad1 = jnp.ones((t.shape[0], rest), F32)
    pad0 = jnp.zeros((t.shape[0], rest), F32)
    cc = jnp.concatenate([c, c, pad1], axis=-1)
    s1 = jnp.concatenate([z, s, pad0], axis=-1)
    s2 = jnp.concatenate([-s, z, pad0], axis=-1)
    return tuple(jnp.tile(a, (1, 2)) for a in (cc, s1, s2))


def _gqa_head_order():
    g = GQA_Q_HEADS // GQA_KV_HEADS
    return [h for j in range(g) for h in (j, g + j)]


def _mixer_even(x, groups, tabs, w_in, rpb, q_gain, k_gain, w_out):
    hd = HEAD_DIM
    na_w = NA_HEADS * hd
    order = _gqa_head_order()
    q0 = 3 * na_w
    q_cols = np.concatenate([q0 + h * hd + np.arange(hd) for h in order])
    w_in_p = jnp.concatenate([w_in[:, :q0], w_in[:, q_cols], w_in[:, q0 + GQA_Q_HEADS * hd:]], axis=1).astype(BF16)
    out_rows = np.concatenate([na_w + h * hd + np.arange(hd) for h in order])
    w_out_p = jnp.concatenate([w_out[:na_w], w_out[out_rows]], axis=0).astype(BF16)
    n_na = na_w // LANES
    n_q = GQA_Q_HEADS * hd // LANES
    modes = ([("plain", Q_SCALE, 0)] * n_na + [("plain", 1.0, 0)] * (2 * n_na)
             + [("norm_rope", Q_SCALE, 0)] * n_q + [("norm_rope", 1.0, 1)] + [("plain", 1.0, 0)])
    gains = jnp.stack([jnp.tile(q_gain.astype(F32), 2), jnp.tile(k_gain.astype(F32), 2)])
    proj = _project(x, w_in_p, tabs, gains, modes, HEAD_DIM // 4)
    cc = _na_bias_table(rpb)
    ya, yb = [], []
    row0 = 0
    for b, s in groups:
        ya.append(_na_attention(proj, cc, row0, b, s))
        yb.append(_gqa_attention(proj, row0, b, s))
        row0 += b * s
    attn = jnp.concatenate([jnp.concatenate(ya, axis=0), jnp.concatenate(yb, axis=0)], axis=1)
    return attn, w_out_p


def _mixer_odd(x, groups, tabs, w_in, w_out):
    n_slab = DIL_HEADS * HEAD_DIM // LANES
    modes = [("rope", Q_SCALE, 0)] * n_slab + [("rope", 1.0, 0)] * n_slab + [("plain", 1.0, 0)] * n_slab
    gains = jnp.ones((1, LANES), F32)
    qkv = _project(x, w_in.astype(BF16), tabs, gains, modes, ROPE_DIMS // 2, out_dtype=F32)
    parts, row0 = [], 0
    for b, s in groups:
        parts.append(_dilated_attention(qkv, row0, b, s))
        row0 += b * s
    return jnp.concatenate(parts, axis=0), w_out.astype(BF16)


def _trunk(xs, w_in_even, rpb_a, q_gain_b, k_gain_b, w_out_even, w_in_odd, w_out_odd,
           ln1_g, ln1_b, ln2_g, ln2_b, router_w, router_b, moe_w1, moe_b1, moe_w2, moe_b2):
    groups = [(x.shape[0], x.shape[1]) for x in xs]
    x = jnp.concatenate([t.reshape(-1, D_MODEL) for t in xs], axis=0).astype(F32)
    tabs_even = _axial_tables(groups)
    tabs_odd = _rope_tables(groups)
    for l in range(DEPTH):
        i = l // 2
        if l % 2 == 0:
            attn, w_out = _mixer_even(x, groups, tabs_even, w_in_even[i], rpb_a[i], q_gain_b[i], k_gain_b[i],
                                      w_out_even[i])
        else:
            attn, w_out = _mixer_odd(x, groups, tabs_odd, w_in_odd[i], w_out_odd[i])
        x, x_packed = _outproj_ln(attn, w_out, x, ln1_g[l], ln1_b[l])
        yg, gate = _moe(x, x_packed, router_w[l], router_b[l], moe_w1[l].astype(BF16), moe_b1[l][:, None, :].astype(F32),
                        moe_w2[l].astype(BF16), moe_b2[l][:, None, :].astype(F32))
        x = _combine_ln(yg, gate, x, ln2_g[l], ln2_b[l])
    outs, row0 = [], 0
    for t in xs:
        n = t.shape[0] * t.shape[1]
        outs.append(x[row0:row0 + n].reshape(t.shape))
        row0 += n
    return tuple(outs)


def kernel(x_prompt, x_sample, w_in_even, rpb_a, q_gain_b, k_gain_b, w_out_even, w_in_odd, w_out_odd, ln1_g, ln1_b,
           ln2_g, ln2_b, router_w, router_b, moe_w1, moe_b1, moe_w2, moe_b2):
    return _trunk((x_prompt, x_sample), w_in_even, rpb_a, q_gain_b, k_gain_b, w_out_even, w_in_odd, w_out_odd,
                  ln1_g, ln1_b, ln2_g, ln2_b, router_w, router_b, moe_w1, moe_b1, moe_w2, moe_b2)
```

```python
import functools

import jax
import jax.numpy as jnp
import numpy as np
from jax import lax
from jax.experimental import pallas as pl
from jax.experimental.pallas import tpu as pltpu
from jax.experimental.pallas import tpu_sc as plsc

F32 = jnp.float32
BF16 = jnp.bfloat16
I32 = jnp.int32

D_MODEL = 1024
DEPTH = 4
HEAD_DIM = 64
GRID_W = 64
NA_HEADS = 8
NA_ROWS = 8
NA_COLS = 16
GQA_Q_HEADS = 8
GQA_KV_HEADS = 2
AXIAL_THETA = 10000.0
QK_NORM_EPS = 1e-6
DIL_HEADS = 16
DIL_BRANCHES = ((128, 1), (512, 4), (2048, 16))
ROPE_THETA = 500000.0
ROPE_DIMS = HEAD_DIM // 4
N_EXPERTS = 32
TOP_K = 4
D_FF = D_MODEL
SWIGLU_LIMIT = 7.0
SWIGLU_ALPHA = 1.702
DN_ALPHA = (2 * DEPTH) ** 0.25
LN_EPS = 1e-5
LOG2E = 1.4426950408889634
Q_SCALE = HEAD_DIM ** -0.5 * LOG2E

LANES = 128
NEG = -1e30
VMEM_LIMIT = 56 * 1024 * 1024
TOKEN_TILE = 512
EXPERT_ROWS = 256
NA_ROW_BLOCK = 8
NA_ROW_GROUP = 4
GQA_Q_TILE = 256
GQA_K_TILE = 512
GQA_DEN_ROWS = 16
WIN_Q_TILE = 128
WIN_RADIUS = 64
DIL_GROUP = 4
SC_INDEX_WINDOW = 128
SC_ROW_WINDOW = 32

EVEN_IN = 3 * NA_HEADS * HEAD_DIM + GQA_Q_HEADS * HEAD_DIM + 2 * GQA_KV_HEADS * HEAD_DIM


def _params(*sem):
    return pltpu.CompilerParams(dimension_semantics=sem, vmem_limit_bytes=VMEM_LIMIT)


def _lane_is_low():
    return lax.broadcasted_iota(I32, (1, LANES), 1) < HEAD_DIM


def _proj_kernel(x_ref, w_ref, c_ref, s1_ref, s2_ref, gain_ref, o_ref, *, slab_modes, shift):
    x = x_ref[...].astype(BF16)
    n_out = o_ref.shape[1]
    chunk = 2 * LANES
    if any(m[0] == "norm_rope" for m in slab_modes):
        r = lax.broadcasted_iota(I32, (LANES, LANES), 0) // HEAD_DIM
        c = lax.broadcasted_iota(I32, (LANES, LANES), 1) // HEAD_DIM
        head_mean = jnp.where(r == c, 1.0 / HEAD_DIM, 0.0).astype(BF16)
    for c0 in range(0, n_out, chunk):
        acc = jnp.dot(x, w_ref[:, c0:c0 + chunk], preferred_element_type=F32)
        for s in range(chunk // LANES):
            slab = c0 // LANES + s
            mode, scale, gidx = slab_modes[slab]
            y = acc[:, s * LANES:(s + 1) * LANES]
            if mode == "norm_rope":
                sq = y * y
                hi = sq.astype(BF16)
                lo = (sq - hi.astype(F32)).astype(BF16)
                ms = (jnp.dot(hi, head_mean, preferred_element_type=F32)
                      + jnp.dot(lo, head_mean, preferred_element_type=F32))
                y = y * lax.rsqrt(ms + QK_NORM_EPS) * gain_ref[gidx:gidx + 1, :]
            if mode in ("rope", "norm_rope"):
                y = (y * c_ref[...] + pltpu.roll(y, LANES - shift, 1) * s2_ref[...]
                     + pltpu.roll(y, shift, 1) * s1_ref[...])
            if scale != 1.0:
                y = y * scale
            o_ref[:, slab * LANES:(slab + 1) * LANES] = y.astype(o_ref.dtype)


def _project(x, w, tabs, gains, slab_modes, shift, out_dtype=BF16):
    n, d = x.shape
    m = w.shape[1]
    tm = TOKEN_TILE
    tab_spec = pl.BlockSpec((tm, LANES), lambda i: (i, 0))
    return pl.pallas_call(
        functools.partial(_proj_kernel, slab_modes=tuple(slab_modes), shift=shift),
        grid=(n // tm,),
        in_specs=[pl.BlockSpec((tm, d), lambda i: (i, 0)),
                  pl.BlockSpec((d, m), lambda i: (0, 0)),
                  tab_spec, tab_spec, tab_spec,
                  pl.BlockSpec(gains.shape, lambda i: (0, 0))],
        out_specs=pl.BlockSpec((tm, m), lambda i: (i, 0)),
        out_shape=jax.ShapeDtypeStruct((n, m), out_dtype),
        compiler_params=_params("parallel"),
        name="in_proj",
    )(x, w, tabs[0], tabs[1], tabs[2], gains)


def _layer_norm_rows(z, g, b):
    mu = jnp.mean(z, axis=-1, keepdims=True)
    zc = z - mu
    var = jnp.mean(zc * zc, axis=-1, keepdims=True)
    return zc * lax.rsqrt(var + LN_EPS) * g + b


def _pack_bf16_pairs(v):
    half = v.shape[1] // 2
    bits = pltpu.bitcast(v.astype(BF16).astype(F32), jnp.uint32)
    return (bits[:, :half] >> 16) | bits[:, half:]


def _unpack_bf16_pairs(w):
    lo = pltpu.bitcast(w << 16, F32).astype(BF16)
    hi = pltpu.bitcast(w & jnp.uint32(0xFFFF0000), F32).astype(BF16)
    return lo, hi


def _outproj_ln_kernel(a_ref, w_ref, x_ref, g_ref, b_ref, o_ref, p_ref):
    y = jnp.dot(a_ref[...], w_ref[...], preferred_element_type=F32)
    out = _layer_norm_rows(DN_ALPHA * x_ref[...] + y, g_ref[...], b_ref[...])
    o_ref[...] = out
    p_ref[...] = _pack_bf16_pairs(out)


def _outproj_ln(a, w, x, g, b):
    n, d = x.shape
    tm = TOKEN_TILE
    row = lambda i: (i, 0)
    fixed = lambda i: (0, 0)
    return pl.pallas_call(
        _outproj_ln_kernel,
        grid=(n // tm,),
        in_specs=[pl.BlockSpec((tm, a.shape[1]), row), pl.BlockSpec(w.shape, fixed),
                  pl.BlockSpec((tm, d), row), pl.BlockSpec((1, d), fixed), pl.BlockSpec((1, d), fixed)],
        out_specs=[pl.BlockSpec((tm, d), row), pl.BlockSpec((tm, d // 2), row)],
        out_shape=[jax.ShapeDtypeStruct((n, d), F32), jax.ShapeDtypeStruct((n, d // 2), jnp.uint32)],
        compiler_params=_params("parallel"),
        name="out_proj_ln",
    )(a, w, x, g.reshape(1, d), b.reshape(1, d))


def _na_kernel(q_ref, k_ref, v_ref, cc_ref, o_ref, *, rows):
    i = pl.program_id(2)
    low = _lane_is_low()

    for g0 in range(0, NA_ROW_BLOCK, NA_ROW_GROUP):
        wins, scores = [], []
        for rr in range(g0, g0 + NA_ROW_GROUP):
            r = i * NA_ROW_BLOCK + rr
            rs = jnp.clip(r - NA_ROWS // 2, 0, rows - NA_ROWS)
            var = r - rs
            q = q_ref[rr * GRID_W:(rr + 1) * GRID_W, :]
            kstart = pl.multiple_of(rs * GRID_W, GRID_W)
            kwin = k_ref[pl.ds(kstart, NA_ROWS * GRID_W), :]
            wins.append(kstart)
            for half in range(2):
                qm = jnp.where(low if half == 0 else jnp.logical_not(low), q, jnp.zeros_like(q))
                s = lax.dot_general(qm, kwin, (((1,), (1,)), ((), ())), preferred_element_type=F32)
                scores.append(s + cc_ref[half, var])
        probs = []
        for s in scores:
            p = jnp.exp2(s - jnp.max(s, axis=-1, keepdims=True))
            probs.append((p.astype(BF16), jnp.sum(p, axis=-1, keepdims=True)))
        for u, rr in enumerate(range(g0, g0 + NA_ROW_GROUP)):
            vwin = v_ref[pl.ds(wins[u], NA_ROWS * GRID_W), :]
            outs = [jnp.dot(p, vwin, preferred_element_type=F32) / l for p, l in probs[2 * u:2 * u + 2]]
            o_ref[rr * GRID_W:(rr + 1) * GRID_W, :] = jnp.where(low, outs[0], outs[1]).astype(o_ref.dtype)


def _na_attention(proj, cc, row0, b, s):
    rows = s // GRID_W
    assert rows >= NA_ROWS and rows % NA_ROW_BLOCK == 0 and row0 % s == 0
    tq = NA_ROW_BLOCK * GRID_W
    n_slab = NA_HEADS * HEAD_DIM // LANES
    qb0, sb0 = row0 // tq, row0 // s
    return pl.pallas_call(
        functools.partial(_na_kernel, rows=rows),
        grid=(b, n_slab, s // tq),
        in_specs=[pl.BlockSpec((tq, LANES), lambda bi, j, i: (qb0 + bi * (s // tq) + i, j)),
                  pl.BlockSpec((s, LANES), lambda bi, j, i: (sb0 + bi, n_slab + j)),
                  pl.BlockSpec((s, LANES), lambda bi, j, i: (sb0 + bi, 2 * n_slab + j)),
                  pl.BlockSpec((2, NA_ROWS, GRID_W, NA_ROWS * GRID_W), lambda bi, j, i: (j, 0, 0, 0))],
        out_specs=pl.BlockSpec((tq, LANES), lambda bi, j, i: (bi * (s // tq) + i, j)),
        out_shape=jax.ShapeDtypeStruct((b * s, n_slab * LANES), BF16),
        compiler_params=_params("parallel", "parallel", "parallel"),
        name="na_attn",
    )(proj, proj, proj, cc)


def _na_bias_table(rpb):
    var = np.arange(NA_ROWS)[:, None]
    j = np.arange(NA_ROWS)[None, :]
    qc = np.arange(GRID_W)[:, None]
    kc = np.arange(GRID_W)[None, :]
    row_sel = ((j - var + NA_ROWS - 1)[..., None] == np.arange(2 * NA_ROWS - 1)).astype(np.float32)
    col_sel = (np.clip(kc - qc + NA_COLS - 1, 0, 2 * NA_COLS - 2)[..., None]
               == np.arange(2 * NA_COLS - 1)).astype(np.float32)
    win = np.clip(qc - NA_COLS // 2, 0, GRID_W - NA_COLS)
    valid = (kc >= win) & (kc < win + NA_COLS)
    t = jnp.einsum("hab,vja,qkb->hvqjk", rpb.astype(F32), row_sel, col_sel, precision=lax.Precision.HIGHEST)
    t = jnp.where(valid[None, None, :, None, :], t * LOG2E, NEG)
    return t.reshape(rpb.shape[0], NA_ROWS, GRID_W, NA_ROWS * GRID_W)


def _gqa_kernel(q_ref, k_ref, vt_ref, o_ref, acc_ref, st_ref, *, n_kt):
    assert n_kt % 2 == 0
    low = _lane_is_low()
    q = q_ref[...]
    tq = q.shape[0]
    zero = jnp.zeros_like(q)
    qs = jnp.concatenate([jnp.where(low, q, zero), jnp.where(low, zero, q)], axis=0)
    acc_ref[...] = jnp.zeros_like(acc_ref)

    def scores(kt, slot):
        start = pl.multiple_of(kt * GQA_K_TILE, GQA_K_TILE)
        kb = k_ref[pl.ds(start, GQA_K_TILE), :]
        st_ref[slot] = lax.dot_general(kb, qs, (((1,), (1,)), ((), ())), preferred_element_type=F32)

    def softmax_pv(kt, slot, m_old):
        st = st_ref[slot]
        m_new = jnp.maximum(m_old, jnp.max(st, axis=0, keepdims=True))
        alpha = jnp.exp2(m_old - m_new)
        pt = jnp.exp2(st - m_new).astype(BF16)
        acc_ref[...] = alpha * acc_ref[...] + jnp.dot(vt_ref[kt], pt, preferred_element_type=F32)
        return m_new

    def step(i, m):
        kt = 2 * i
        scores(kt + 1, 1)
        m = softmax_pv(kt, 0, m)
        scores(jnp.minimum(kt + 2, n_kt - 1), 0)
        return softmax_pv(kt + 1, 1, m)

    scores(0, 0)
    lax.fori_loop(0, n_kt // 2, step, jnp.full((1, 2 * tq), -jnp.inf, F32))
    out_t = acc_ref[:LANES, :] / acc_ref[LANES:LANES + 1, :]
    o_t = jnp.concatenate([out_t[:HEAD_DIM, :tq], out_t[HEAD_DIM:, tq:]], axis=0)
    o_ref[...] = o_t.T.astype(o_ref.dtype)


def _gqa_attention(proj, row0, b, s):
    tq, tk = GQA_Q_TILE, GQA_K_TILE
    assert s % tk == 0 and row0 % s == 0
    n_slab = GQA_Q_HEADS * HEAD_DIM // LANES
    q_col = 3 * NA_HEADS * HEAD_DIM // LANES
    k_col = q_col + n_slab
    v0 = (k_col + 1) * LANES
    qb0, sb0 = row0 // tq, row0 // s
    v_t = proj[row0:row0 + b * s, v0:v0 + LANES].reshape(b, s // tk, tk, LANES).transpose(0, 1, 3, 2)
    ones_rows = jnp.zeros((b, s // tk, GQA_DEN_ROWS, tk), BF16).at[:, :, 0, :].set(1.0)
    v_t = jnp.concatenate([v_t, ones_rows], axis=2)
    vt_rows = LANES + GQA_DEN_ROWS
    return pl.pallas_call(
        functools.partial(_gqa_kernel, n_kt=s // tk),
        grid=(b, n_slab, s // tq),
        in_specs=[pl.BlockSpec((tq, LANES), lambda bi, j, i: (qb0 + bi * (s // tq) + i, q_col + j)),
                  pl.BlockSpec((s, LANES), lambda bi, j, i: (sb0 + bi, k_col)),
                  pl.BlockSpec((None, s // tk, vt_rows, tk), lambda bi, j, i: (bi, 0, 0, 0))],
        out_specs=pl.BlockSpec((tq, LANES), lambda bi, j, i: (bi * (s // tq) + i, j)),
        out_shape=jax.ShapeDtypeStruct((b * s, n_slab * LANES), BF16),
        scratch_shapes=[pltpu.VMEM((vt_rows, 2 * tq), F32), pltpu.VMEM((2, tk, 2 * tq), F32)],
        compiler_params=_params("parallel", "parallel", "parallel"),
        name="gqa_attn",
    )(proj, proj, v_t)


def _dil_kernel(q_ref, k_ref, v_ref, o_ref, m_ref, l_ref, acc_ref, *, seq):
    tile = q_ref.shape[0]
    t0 = pl.program_id(2) * tile
    low = _lane_is_low()
    sub = WIN_Q_TILE
    n_sub = tile // sub

    for bi, (_, r) in enumerate(DIL_BRANCHES):
        length = seq // r
        win = min(sub + 2 * WIN_RADIUS, length)

        def sub_blocks(g, carry, r=r, length=length, win=win, first=(bi == 0)):
            rows, scores, stats = [], [], []
            for u in range(DIL_GROUP):
                n = g * DIL_GROUP + u
                c = n % r
                blk = n // r
                u0 = t0 // r + blk * sub
                ks = jnp.clip(u0 - WIN_RADIUS, 0, length - win)
                if r == 1:
                    q_rows = pl.ds(pl.multiple_of(n * sub, sub), sub)
                    k_rows = pl.ds(pl.multiple_of(ks, WIN_RADIUS), win)
                else:
                    q_rows = pl.ds(c + r * blk * sub, sub, stride=r)
                    k_rows = pl.ds(c + r * ks, win, stride=r)
                rows.append((q_rows, k_rows))
                q = q_ref[q_rows, :].astype(BF16)
                kwin = k_ref[k_rows, :].astype(BF16)
                qpos = u0 + lax.broadcasted_iota(I32, (sub, win), 0)
                kpos = ks + lax.broadcasted_iota(I32, (sub, win), 1)
                valid = jnp.abs(kpos - qpos) <= WIN_RADIUS
                for half in range(2):
                    qm = jnp.where(low if half == 0 else jnp.logical_not(low), q, jnp.zeros_like(q))
                    s = lax.dot_general(qm, kwin, (((1,), (1,)), ((), ())), preferred_element_type=F32)
                    scores.append(jnp.where(valid, s, NEG))
            for s in scores:
                m = jnp.max(s, axis=-1, keepdims=True)
                p = jnp.exp2(s - m)
                stats.append((m, jnp.sum(p, axis=-1, keepdims=True), p.astype(BF16)))
            for u, (q_rows, k_rows) in enumerate(rows):
                vwin = v_ref[k_rows, :].astype(BF16)
                (m0, l0, p0), (m1, l1, p1) = stats[2 * u:2 * u + 2]
                m_b = jnp.where(low, m0, m1)
                l_b = jnp.where(low, l0, l1)
                pv_b = jnp.where(low, jnp.dot(p0, vwin, preferred_element_type=F32),
                                 jnp.dot(p1, vwin, preferred_element_type=F32))
                if first:
                    m_ref[q_rows, :] = m_b
                    l_ref[q_rows, :] = l_b
                    acc_ref[q_rows, :] = pv_b
                else:
                    m_old = m_ref[q_rows, :]
                    m_new = jnp.maximum(m_old, m_b)
                    a_old = jnp.exp2(m_old - m_new)
                    a_b = jnp.exp2(m_b - m_new)
                    l_ref[q_rows, :] = a_old * l_ref[q_rows, :] + a_b * l_b
                    acc_ref[q_rows, :] = a_old * acc_ref[q_rows, :] + a_b * pv_b
                    m_ref[q_rows, :] = m_new
            return carry

        lax.fori_loop(0, n_sub // DIL_GROUP, sub_blocks, 0)

    o_ref[...] = (acc_ref[...] / l_ref[...]).astype(o_ref.dtype)


def _dilated_attention(qkv, row0, b, s):
    tile = 16 * WIN_Q_TILE
    assert s % tile == 0 and row0 % s == 0
    n_slab = DIL_HEADS * HEAD_DIM // LANES
    qb0, sb0 = row0 // tile, row0 // s
    state = pltpu.VMEM((tile, LANES), F32)
    return pl.pallas_call(
        functools.partial(_dil_kernel, seq=s),
        grid=(b, n_slab, s // tile),
        in_specs=[pl.BlockSpec((tile, LANES), lambda bi, j, i: (qb0 + bi * (s // tile) + i, j)),
                  pl.BlockSpec((s, LANES), lambda bi, j, i: (sb0 + bi, n_slab + j)),
                  pl.BlockSpec((s, LANES), lambda bi, j, i: (sb0 + bi, 2 * n_slab + j))],
        out_specs=pl.BlockSpec((tile, LANES), lambda bi, j, i: (bi * (s // tile) + i, j)),
        out_shape=jax.ShapeDtypeStruct((b * s, n_slab * LANES), BF16),
        scratch_shapes=[state, state, state],
        compiler_params=_params("parallel", "parallel", "parallel"),
        name="dil_attn",
    )(qkv, qkv, qkv)


def _router_kernel(x_ref, wh_ref, wl_ref, b_ref, idx_ref, gate_ref, rank_ref, cnt_ref, base_ref):
    step = pl.program_id(0)

    @pl.when(step == 0)
    def _():
        base_ref[...] = jnp.zeros_like(base_ref)

    x = x_ref[...]
    xh = x.astype(BF16)
    xl = (x - xh.astype(F32)).astype(BF16)
    nt = (((1,), (1,)), ((), ()))
    logits = (lax.dot_general(wh_ref[...], xh, nt, preferred_element_type=F32)
              + lax.dot_general(wl_ref[...], xh, nt, preferred_element_type=F32)
              + lax.dot_general(wh_ref[...], xl, nt, preferred_element_type=F32)) + b_ref[...]
    tm = x.shape[0]
    eid = lax.broadcasted_iota(I32, (N_EXPERTS, tm), 0)
    vals = logits
    top_v, top_i, hots = [], [], []
    for _ in range(TOP_K):
        m = jnp.max(vals, axis=0, keepdims=True)
        idx = jnp.min(jnp.where(vals == m, eid, N_EXPERTS), axis=0, keepdims=True)
        hot = eid == idx
        top_v.append(m)
        top_i.append(idx)
        hots.append(hot)
        vals = jnp.where(hot, -jnp.inf, vals)
    es = [jnp.exp(v - top_v[0]) for v in top_v]
    den = functools.reduce(jnp.add, es)
    chosen = functools.reduce(jnp.logical_or, hots)
    before = (lax.broadcasted_iota(I32, (tm, tm), 0) < lax.broadcasted_iota(I32, (tm, tm), 1))
    prefix = jnp.dot(chosen.astype(BF16), before.astype(BF16), preferred_element_type=F32) + base_ref[...]
    for k in range(TOP_K):
        idx_ref[k:k + 1, :] = top_i[k]
        gate_ref[k:k + 1, :] = es[k] / den
        rank_ref[k:k + 1, :] = jnp.sum(jnp.where(hots[k], prefix, 0.0), axis=0, keepdims=True).astype(I32)
    base_ref[...] = base_ref[...] + jnp.sum(chosen.astype(F32), axis=1, keepdims=True)
    cnt_ref[...] = base_ref[...].astype(I32)


def _router(x, w_hi_t, w_lo_t, bias):
    n, d = x.shape
    tm = TOKEN_TILE
    tok = pl.BlockSpec((TOP_K, tm), lambda i: (0, i))
    fixed = lambda i: (0, 0)
    return pl.pallas_call(
        _router_kernel,
        grid=(n // tm,),
        in_specs=[pl.BlockSpec((tm, d), lambda i: (i, 0)), pl.BlockSpec((N_EXPERTS, d), fixed),
                  pl.BlockSpec((N_EXPERTS, d), fixed), pl.BlockSpec((N_EXPERTS, 1), fixed)],
        out_specs=[tok, tok, tok, pl.BlockSpec((N_EXPERTS, 1), fixed)],
        out_shape=[jax.ShapeDtypeStruct((TOP_K, n), I32), jax.ShapeDtypeStruct((TOP_K, n), F32),
                   jax.ShapeDtypeStruct((TOP_K, n), I32), jax.ShapeDtypeStruct((N_EXPERTS, 1), I32)],
        scratch_shapes=[pltpu.VMEM((N_EXPERTS, 1), F32)],
        compiler_params=_params("arbitrary"),
        name="router",
    )(x, w_hi_t, w_lo_t, bias.reshape(N_EXPERTS, 1).astype(F32))


def _expert_kernel(blk_exp_ref, fresh_ref, n_used_ref, x_ref, w1_ref, b1_ref, w2_ref, b2_ref, o_ref, w1b_ref, w2b_ref):
    blk = pl.program_id(0)

    @pl.when(fresh_ref[blk] == 1)
    def _():
        w1b_ref[...] = w1_ref[...].astype(BF16)
        w2b_ref[...] = w2_ref[...].astype(BF16)

    @pl.when(blk < n_used_ref[0])
    def _():
        half = D_MODEL // 2
        x_lo, x_hi = _unpack_bf16_pairs(x_ref[...])
        h = (jnp.dot(x_lo, w1b_ref[:half, :], preferred_element_type=F32)
             + jnp.dot(x_hi, w1b_ref[half:, :], preferred_element_type=F32)) + b1_ref[...]
        g = jnp.minimum(h[:, :D_FF], SWIGLU_LIMIT)
        u = jnp.clip(h[:, D_FF:], -SWIGLU_LIMIT, SWIGLU_LIMIT)
        act = g * jax.nn.sigmoid(SWIGLU_ALPHA * g) * (u + 1.0)
        o_ref[...] = jnp.dot(act.astype(BF16), w2b_ref[...], preferred_element_type=F32) + b2_ref[...]

    @pl.when(blk >= n_used_ref[0])
    def _():
        o_ref[...] = jnp.zeros_like(o_ref)


def _experts(xs, blk_exp, n_used, layer, w1, b1, w2, b2):
    n_rows = xs.shape[0]
    d = D_MODEL
    bm = EXPERT_ROWS
    f2 = w1.shape[3]
    fresh = jnp.concatenate([jnp.ones((1,), I32), (blk_exp[1:] != blk_exp[:-1]).astype(I32)])
    wmap = lambda i, be, fr, nu: (layer, be[i], 0, 0)
    return pl.pallas_call(
        _expert_kernel,
        grid_spec=pltpu.PrefetchScalarGridSpec(
            num_scalar_prefetch=3,
            grid=(n_rows // bm,),
            in_specs=[pl.BlockSpec((bm, d // 2), lambda i, be, fr, nu: (i, 0)),
                      pl.BlockSpec((None, None, d, f2), wmap),
                      pl.BlockSpec((None, None, 1, f2), wmap),
                      pl.BlockSpec((None, None, f2 // 2, d), wmap),
                      pl.BlockSpec((None, None, 1, d), wmap)],
            out_specs=pl.BlockSpec((bm, d), lambda i, be, fr, nu: (i, 0)),
            scratch_shapes=[pltpu.VMEM((d, f2), BF16), pltpu.VMEM((f2 // 2, d), BF16)],
        ),
        out_shape=jax.ShapeDtypeStruct((n_rows, d), F32),
        compiler_params=_params("arbitrary"),
        name="experts",
    )(blk_exp, fresh, n_used, xs, w1, b1, w2, b2)


def _combine_ln_kernel(y_ref, gate_ref, x_ref, g_ref, b_ref, o_ref):
    gate = gate_ref[...]
    y = functools.reduce(jnp.add, [gate[:, k:k + 1] * y_ref[k] for k in range(TOP_K)])
    o_ref[...] = _layer_norm_rows(DN_ALPHA * x_ref[...] + y, g_ref[...], b_ref[...])


def _combine_ln(yg, gate, x, g, b, row0=0, n_rows=None):
    n, d = x.shape
    n_rows = n if n_rows is None else n_rows
    tm = TOKEN_TILE
    assert row0 % tm == 0 and n_rows % tm == 0
    blk0 = row0 // tm
    row = lambda i: (blk0 + i, 0)
    fixed = lambda i: (0, 0)
    return pl.pallas_call(
        _combine_ln_kernel,
        grid=(n_rows // tm,),
        in_specs=[pl.BlockSpec((TOP_K, tm, d), lambda i: (0, blk0 + i, 0)), pl.BlockSpec((tm, TOP_K), row),
                  pl.BlockSpec((tm, d), row), pl.BlockSpec((1, d), fixed), pl.BlockSpec((1, d), fixed)],
        out_specs=pl.BlockSpec((tm, d), lambda i: (i, 0)),
        out_shape=jax.ShapeDtypeStruct((n_rows, d), F32),
        compiler_params=_params("parallel"),
        name="combine_ln",
    )(yg, gate, x, g.reshape(1, d), b.reshape(1, d))


def _sc_mesh():
    return plsc.VectorSubcoreMesh(core_axis_name="core", subcore_axis_name="subcore")


def _sc_scatter_rows(x, dest, n_out):
    n, d = x.shape
    mesh = _sc_mesh()
    per_worker = n // (mesh.num_cores * mesh.num_subcores)
    n_sub = SC_INDEX_WINDOW // SC_ROW_WINDOW
    assert per_worker % SC_INDEX_WINDOW == 0 and n_sub >= 2

    @functools.partial(
        pl.kernel, out_type=jax.ShapeDtypeStruct((n_out, d), x.dtype), mesh=mesh,
        scratch_types=[pltpu.VMEM((TOP_K, SC_INDEX_WINDOW), I32), pltpu.VMEM((2, SC_ROW_WINDOW, d), x.dtype),
                       pltpu.SemaphoreType.DMA((2,)), pltpu.SemaphoreType.DMA((2,))])
    def scatter(x_hbm, i_hbm, o_hbm, idx_v, buf, sem_r, sem_s):
        wid = lax.axis_index("core") * mesh.num_subcores + lax.axis_index("subcore")

        @pl.loop(0, per_worker // SC_INDEX_WINDOW)
        def _(it):
            base = wid * per_worker + it * SC_INDEX_WINDOW
            for k in range(TOP_K):
                pltpu.sync_copy(i_hbm.at[k, pl.ds(base, SC_INDEX_WINDOW)], idx_v.at[k])

            def read(j):
                rows = pl.ds(base + j * SC_ROW_WINDOW, SC_ROW_WINDOW)
                return pltpu.make_async_copy(x_hbm.at[rows], buf.at[j % 2], sem_r.at[j % 2])

            def send(j, k):
                rows = idx_v.at[k, pl.ds(j * SC_ROW_WINDOW, SC_ROW_WINDOW)]
                return pltpu.make_async_copy(buf.at[j % 2], o_hbm.at[rows], sem_s.at[j % 2])

            read(0).start()
            for j in range(n_sub):
                read(j).wait()
                for k in range(TOP_K):
                    send(j, k).start()
                if j + 1 < n_sub:
                    if j >= 1:
                        for k in range(TOP_K):
                            send(j - 1, k).wait()
                    read(j + 1).start()
            for j in (n_sub - 2, n_sub - 1):
                for k in range(TOP_K):
                    send(j, k).wait()

    return scatter(x, dest)


def _sc_gather_rows(table, indices):
    num = indices.shape[0]
    d = table.shape[1]
    mesh = _sc_mesh()
    per_worker = num // (mesh.num_cores * mesh.num_subcores)
    n_sub = SC_INDEX_WINDOW // SC_ROW_WINDOW
    assert per_worker % SC_INDEX_WINDOW == 0 and n_sub >= 2

    @functools.partial(
        pl.kernel, out_type=jax.ShapeDtypeStruct((num, d), table.dtype), mesh=mesh,
        scratch_types=[pltpu.VMEM((SC_INDEX_WINDOW,), I32), pltpu.VMEM((2, SC_ROW_WINDOW, d), table.dtype),
                       pltpu.SemaphoreType.DMA((2,)), pltpu.SemaphoreType.DMA((2,))])
    def gather(x_hbm, i_hbm, o_hbm, idx_v, buf, sem_g, sem_w):
        wid = lax.axis_index("core") * mesh.num_subcores + lax.axis_index("subcore")

        @pl.loop(0, per_worker // SC_INDEX_WINDOW)
        def _(it):
            base = wid * per_worker + it * SC_INDEX_WINDOW
            pltpu.sync_copy(i_hbm.at[pl.ds(base, SC_INDEX_WINDOW)], idx_v)

            def fetch(j):
                rows = idx_v.at[pl.ds(j * SC_ROW_WINDOW, SC_ROW_WINDOW)]
                return pltpu.make_async_copy(x_hbm.at[rows], buf.at[j % 2], sem_g.at[j % 2])

            def write(j):
                rows = pl.ds(base + j * SC_ROW_WINDOW, SC_ROW_WINDOW)
                return pltpu.make_async_copy(buf.at[j % 2], o_hbm.at[rows], sem_w.at[j % 2])

            fetch(0).start()
            for j in range(n_sub):
                fetch(j).wait()
                write(j).start()
                if j + 1 < n_sub:
                    if j >= 1:
                        write(j - 1).wait()
                    fetch(j + 1).start()
            write(n_sub - 2).wait()
            write(n_sub - 1).wait()

    return gather(table, indices)


def _moe(x, x_packed, w_r, b_r, layer, w1, b1, w2, b2):
    n, d = x.shape
    bm = EXPERT_ROWS
    wr_t = w_r.T
    wr_hi = wr_t.astype(BF16)
    wr_lo = (wr_t - wr_hi.astype(F32)).astype(BF16)
    idx, gate, rank, counts = _router(x, wr_hi, wr_lo, b_r)
    counts = counts[:, 0]
    padded = (counts + bm - 1) // bm * bm
    pad_ends = jnp.cumsum(padded)
    pad_starts = pad_ends - padded
    hot = idx[:, None, :] == jnp.arange(N_EXPERTS, dtype=I32)[None, :, None]
    dest = jnp.sum(jnp.where(hot, pad_starts[None, :, None], 0), axis=1) + rank
    n_blocks = n * TOP_K // bm + N_EXPERTS
    blk_start = jnp.arange(n_blocks, dtype=I32) * bm
    blk_exp = jnp.minimum(jnp.sum(blk_start[:, None] >= pad_ends[None, :], axis=1), N_EXPERTS - 1).astype(I32)
    n_used = (pad_ends[-1:] // bm).astype(I32)
    xs = _sc_scatter_rows(x_packed, dest, n_blocks * bm)
    ys = _experts(xs, blk_exp, n_used, layer, w1, b1, w2, b2)
    yg = _sc_gather_rows(ys, dest.reshape(-1)).reshape(TOP_K, n, d)
    return yg, gate.T


def _positions(groups):
    return jnp.concatenate([jnp.tile(jnp.arange(s), b) for b, s in groups])


def _axial_tables(groups):
    t = _positions(groups)
    n = HEAD_DIM // 4
    inv = AXIAL_THETA ** (-jnp.arange(n, dtype=F32) / n)
    ar = (t // GRID_W).astype(F32)[:, None] * inv
    ac = (t % GRID_W).astype(F32)[:, None] * inv
    z = jnp.zeros_like(ar)
    cr, sr, cc, sc = jnp.cos(ar), jnp.sin(ar), jnp.cos(ac), jnp.sin(ac)
    c = jnp.concatenate([cr, cr, cc, cc], axis=-1)
    s1 = jnp.concatenate([z, sr, z, sc], axis=-1)
    s2 = jnp.concatenate([-sr, z, -sc, z], axis=-1)
    return tuple(jnp.tile(a, (1, 2)) for a in (c, s1, s2))


def _rope_tables(groups):
    t = _positions(groups)
    n = ROPE_DIMS // 2
    inv = ROPE_THETA ** (-jnp.arange(n, dtype=F32) / n)
    ang = t.astype(F32)[:, None] * inv
    c, s = jnp.cos(ang), jnp.sin(ang)
    z = jnp.zeros_like(c)
    rest = HEAD_DIM - ROPE_DIMS
    pad1 = jnp.ones((t.shape[0], rest), F32)
    pad0 = jnp.zeros((t.shape[0], rest), F32)
    cc = jnp.concatenate([c, c, pad1], axis=-1)
    s1 = jnp.concatenate([z, s, pad0], axis=-1)
    s2 = jnp.concatenate([-s, z, pad0], axis=-1)
    return tuple(jnp.tile(a, (1, 2)) for a in (cc, s1, s2))


def _gqa_head_order():
    g = GQA_Q_HEADS // GQA_KV_HEADS
    return [h for j in range(g) for h in (j, g + j)]


def _mixer_even(x, groups, tabs, w_in, rpb, q_gain, k_gain, w_out):
    hd = HEAD_DIM
    na_w = NA_HEADS * hd
    order = _gqa_head_order()
    q0 = 3 * na_w
    q_cols = np.concatenate([q0 + h * hd + np.arange(hd) for h in order])
    w_in_p = jnp.concatenate([w_in[:, :q0], w_in[:, q_cols], w_in[:, q0 + GQA_Q_HEADS * hd:]], axis=1).astype(BF16)
    out_rows = np.concatenate([na_w + h * hd + np.arange(hd) for h in order])
    w_out_p = jnp.concatenate([w_out[:na_w], w_out[out_rows]], axis=0).astype(BF16)
    n_na = na_w // LANES
    n_q = GQA_Q_HEADS * hd // LANES
    modes = ([("plain", Q_SCALE, 0)] * n_na + [("plain", 1.0, 0)] * (2 * n_na)
             + [("norm_rope", Q_SCALE, 0)] * n_q + [("norm_rope", 1.0, 1)] + [("plain", 1.0, 0)])
    gains = jnp.stack([jnp.tile(q_gain.astype(F32), 2), jnp.tile(k_gain.astype(F32), 2)])
    proj = _project(x, w_in_p, tabs, gains, modes, HEAD_DIM // 4)
    cc = _na_bias_table(rpb)
    ya, yb = [], []
    row0 = 0
    for b, s in groups:
        ya.append(_na_attention(proj, cc, row0, b, s))
        yb.append(_gqa_attention(proj, row0, b, s))
        row0 += b * s
    attn = jnp.concatenate([jnp.concatenate(ya, axis=0), jnp.concatenate(yb, axis=0)], axis=1)
    return attn, w_out_p


def _mixer_odd(x, groups, tabs, w_in, w_out):
    n_slab = DIL_HEADS * HEAD_DIM // LANES
    modes = [("rope", Q_SCALE, 0)] * n_slab + [("rope", 1.0, 0)] * n_slab + [("plain", 1.0, 0)] * n_slab
    gains = jnp.ones((1, LANES), F32)
    qkv = _project(x, w_in.astype(BF16), tabs, gains, modes, ROPE_DIMS // 2, out_dtype=F32)
    parts, row0 = [], 0
    for b, s in groups:
        parts.append(_dilated_attention(qkv, row0, b, s))
        row0 += b * s
    return jnp.concatenate(parts, axis=0), w_out.astype(BF16)


def _trunk(xs, w_in_even, rpb_a, q_gain_b, k_gain_b, w_out_even, w_in_odd, w_out_odd,
           ln1_g, ln1_b, ln2_g, ln2_b, router_w, router_b, moe_w1, moe_b1, moe_w2, moe_b2):
    groups = [(x.shape[0], x.shape[1]) for x in xs]
    x = jnp.concatenate([t.reshape(-1, D_MODEL) for t in xs], axis=0).astype(F32)
    tabs_even = _axial_tables(groups)
    tabs_odd = _rope_tables(groups)
    b1 = moe_b1.astype(F32)[:, :, None, :]
    b2 = moe_b2.astype(F32)[:, :, None, :]
    for l in range(DEPTH):
        i = l // 2
        if l % 2 == 0:
            attn, w_out = _mixer_even(x, groups, tabs_even, w_in_even[i], rpb_a[i], q_gain_b[i], k_gain_b[i],
                                      w_out_even[i])
        else:
            attn, w_out = _mixer_odd(x, groups, tabs_odd, w_in_odd[i], w_out_odd[i])
        x, x_packed = _outproj_ln(attn, w_out, x, ln1_g[l], ln1_b[l])
        yg, gate = _moe(x, x_packed, router_w[l], router_b[l], l, moe_w1, b1, moe_w2, b2)
        if l + 1 < DEPTH:
            x = _combine_ln(yg, gate, x, ln2_g[l], ln2_b[l])
    outs, row0 = [], 0
    for t in xs:
        n = t.shape[0] * t.shape[1]
        outs.append(_combine_ln(yg, gate, x, ln2_g[-1], ln2_b[-1], row0, n).reshape(t.shape))
        row0 += n
    return tuple(outs)


def kernel(x_prompt, x_sample, w_in_even, rpb_a, q_gain_b, k_gain_b, w_out_even, w_in_odd, w_out_odd, ln1_g, ln1_b,
           ln2_g, ln2_b, router_w, router_b, moe_w1, moe_b1, moe_w2, moe_b2):
    return _trunk((x_prompt, x_sample), w_in_even, rpb_a, q_gain_b, k_gain_b, w_out_even, w_in_odd, w_out_odd,
                  ln1_g, ln1_b, ln2_g, ln2_b, router_w, router_b, moe_w1, moe_b1, moe_w2, moe_b2)
```

```python
import functools

import jax
import jax.numpy as jnp
import numpy as np
from jax import lax
from jax.experimental import pallas as pl
from jax.experimental.pallas import tpu as pltpu
from jax.experimental.pallas import tpu_sc as plsc

F32 = jnp.float32
BF16 = jnp.bfloat16
I32 = jnp.int32

D_MODEL = 1024
DEPTH = 4
HEAD_DIM = 64
GRID_W = 64
NA_HEADS = 8
NA_ROWS = 8
NA_COLS = 16
GQA_Q_HEADS = 8
GQA_KV_HEADS = 2
AXIAL_THETA = 10000.0
QK_NORM_EPS = 1e-6
DIL_HEADS = 16
DIL_BRANCHES = ((128, 1), (512, 4), (2048, 16))
ROPE_THETA = 500000.0
ROPE_DIMS = HEAD_DIM // 4
N_EXPERTS = 32
TOP_K = 4
D_FF = D_MODEL
SWIGLU_LIMIT = 7.0
SWIGLU_ALPHA = 1.702
DN_ALPHA = (2 * DEPTH) ** 0.25
LN_EPS = 1e-5
LOG2E = 1.4426950408889634
Q_SCALE = HEAD_DIM ** -0.5 * LOG2E

LANES = 128
NEG = -1e30
VMEM_LIMIT = 56 * 1024 * 1024
TOKEN_TILE = 512
EXPERT_ROWS = 256
NA_ROW_BLOCK = 8
NA_ROW_GROUP = 4
GQA_Q_TILE = 256
GQA_K_TILE = 512
GQA_DEN_ROWS = 16
WIN_Q_TILE = 128
WIN_RADIUS = 64
DIL_GROUP = 4
SC_INDEX_WINDOW = 128
SC_ROW_WINDOW = 32

EVEN_IN = 3 * NA_HEADS * HEAD_DIM + GQA_Q_HEADS * HEAD_DIM + 2 * GQA_KV_HEADS * HEAD_DIM


def _params(*sem):
    return pltpu.CompilerParams(dimension_semantics=sem, vmem_limit_bytes=VMEM_LIMIT)


def _lane_is_low():
    return lax.broadcasted_iota(I32, (1, LANES), 1) < HEAD_DIM


def _proj_kernel(x_ref, w_ref, c_ref, s1_ref, s2_ref, gain_ref, o_ref, *, slab_modes, shift):
    x = x_ref[...].astype(BF16)
    n_out = o_ref.shape[1]
    chunk = 2 * LANES
    if any(m[0] == "norm_rope" for m in slab_modes):
        r = lax.broadcasted_iota(I32, (LANES, LANES), 0) // HEAD_DIM
        c = lax.broadcasted_iota(I32, (LANES, LANES), 1) // HEAD_DIM
        head_mean = jnp.where(r == c, 1.0 / HEAD_DIM, 0.0).astype(BF16)
    for c0 in range(0, n_out, chunk):
        acc = jnp.dot(x, w_ref[:, c0:c0 + chunk], preferred_element_type=F32)
        for s in range(chunk // LANES):
            slab = c0 // LANES + s
            mode, scale, gidx = slab_modes[slab]
            y = acc[:, s * LANES:(s + 1) * LANES]
            if mode == "norm_rope":
                sq = y * y
                hi = sq.astype(BF16)
                lo = (sq - hi.astype(F32)).astype(BF16)
                ms = (jnp.dot(hi, head_mean, preferred_element_type=F32)
                      + jnp.dot(lo, head_mean, preferred_element_type=F32))
                y = y * lax.rsqrt(ms + QK_NORM_EPS) * gain_ref[gidx:gidx + 1, :]
            if mode in ("rope", "norm_rope"):
                y = (y * c_ref[...] + pltpu.roll(y, LANES - shift, 1) * s2_ref[...]
                     + pltpu.roll(y, shift, 1) * s1_ref[...])
            if scale != 1.0:
                y = y * scale
            o_ref[:, slab * LANES:(slab + 1) * LANES] = y.astype(o_ref.dtype)


def _project(x, w, tabs, gains, slab_modes, shift, out_dtype=BF16):
    n, d = x.shape
    m = w.shape[1]
    tm = TOKEN_TILE
    tab_spec = pl.BlockSpec((tm, LANES), lambda i: (i, 0))
    return pl.pallas_call(
        functools.partial(_proj_kernel, slab_modes=tuple(slab_modes), shift=shift),
        grid=(n // tm,),
        in_specs=[pl.BlockSpec((tm, d), lambda i: (i, 0)),
                  pl.BlockSpec((d, m), lambda i: (0, 0)),
                  tab_spec, tab_spec, tab_spec,
                  pl.BlockSpec(gains.shape, lambda i: (0, 0))],
        out_specs=pl.BlockSpec((tm, m), lambda i: (i, 0)),
        out_shape=jax.ShapeDtypeStruct((n, m), out_dtype),
        compiler_params=_params("parallel"),
        name="in_proj",
    )(x, w, tabs[0], tabs[1], tabs[2], gains)


def _layer_norm_rows(z, g, b):
    mu = jnp.mean(z, axis=-1, keepdims=True)
    zc = z - mu
    var = jnp.mean(zc * zc, axis=-1, keepdims=True)
    return zc * lax.rsqrt(var + LN_EPS) * g + b


def _pack_bf16_pairs(v):
    half = v.shape[1] // 2
    bits = pltpu.bitcast(v.astype(BF16).astype(F32), jnp.uint32)
    return (bits[:, :half] >> 16) | bits[:, half:]


def _unpack_bf16_pairs(w):
    lo = pltpu.bitcast(w << 16, F32).astype(BF16)
    hi = pltpu.bitcast(w & jnp.uint32(0xFFFF0000), F32).astype(BF16)
    return lo, hi


def _outproj_ln_kernel(a_ref, w_ref, x_ref, g_ref, b_ref, o_ref, p_ref):
    y = jnp.dot(a_ref[...], w_ref[...], preferred_element_type=F32)
    out = _layer_norm_rows(DN_ALPHA * x_ref[...] + y, g_ref[...], b_ref[...])
    o_ref[...] = out
    p_ref[...] = _pack_bf16_pairs(out)


def _outproj_ln(a, w, x, g, b):
    n, d = x.shape
    tm = TOKEN_TILE
    row = lambda i: (i, 0)
    fixed = lambda i: (0, 0)
    return pl.pallas_call(
        _outproj_ln_kernel,
        grid=(n // tm,),
        in_specs=[pl.BlockSpec((tm, a.shape[1]), row), pl.BlockSpec(w.shape, fixed),
                  pl.BlockSpec((tm, d), row), pl.BlockSpec((1, d), fixed), pl.BlockSpec((1, d), fixed)],
        out_specs=[pl.BlockSpec((tm, d), row), pl.BlockSpec((tm, d // 2), row)],
        out_shape=[jax.ShapeDtypeStruct((n, d), F32), jax.ShapeDtypeStruct((n, d // 2), jnp.uint32)],
        compiler_params=_params("parallel"),
        name="out_proj_ln",
    )(a, w, x, g.reshape(1, d), b.reshape(1, d))


def _na_kernel(q_ref, k_ref, v_ref, cc_ref, o_ref, *, rows):
    i = pl.program_id(2)
    low = _lane_is_low()

    for g0 in range(0, NA_ROW_BLOCK, NA_ROW_GROUP):
        wins, scores = [], []
        for rr in range(g0, g0 + NA_ROW_GROUP):
            r = i * NA_ROW_BLOCK + rr
            rs = jnp.clip(r - NA_ROWS // 2, 0, rows - NA_ROWS)
            var = r - rs
            q = q_ref[rr * GRID_W:(rr + 1) * GRID_W, :]
            kstart = pl.multiple_of(rs * GRID_W, GRID_W)
            kwin = k_ref[pl.ds(kstart, NA_ROWS * GRID_W), :]
            wins.append(kstart)
            for half in range(2):
                qm = jnp.where(low if half == 0 else jnp.logical_not(low), q, jnp.zeros_like(q))
                s = lax.dot_general(qm, kwin, (((1,), (1,)), ((), ())), preferred_element_type=F32)
                scores.append(s + cc_ref[half, var])
        probs = []
        for s in scores:
            p = jnp.exp2(s - jnp.max(s, axis=-1, keepdims=True))
            probs.append((p.astype(BF16), jnp.sum(p, axis=-1, keepdims=True)))
        for u, rr in enumerate(range(g0, g0 + NA_ROW_GROUP)):
            vwin = v_ref[pl.ds(wins[u], NA_ROWS * GRID_W), :]
            outs = [jnp.dot(p, vwin, preferred_element_type=F32) / l for p, l in probs[2 * u:2 * u + 2]]
            o_ref[rr * GRID_W:(rr + 1) * GRID_W, :] = jnp.where(low, outs[0], outs[1]).astype(o_ref.dtype)


def _na_attention(proj, cc, row0, b, s):
    rows = s // GRID_W
    assert rows >= NA_ROWS and rows % NA_ROW_BLOCK == 0 and row0 % s == 0
    tq = NA_ROW_BLOCK * GRID_W
    n_slab = NA_HEADS * HEAD_DIM // LANES
    qb0, sb0 = row0 // tq, row0 // s
    return pl.pallas_call(
        functools.partial(_na_kernel, rows=rows),
        grid=(b, n_slab, s // tq),
        in_specs=[pl.BlockSpec((tq, LANES), lambda bi, j, i: (qb0 + bi * (s // tq) + i, j)),
                  pl.BlockSpec((s, LANES), lambda bi, j, i: (sb0 + bi, n_slab + j)),
                  pl.BlockSpec((s, LANES), lambda bi, j, i: (sb0 + bi, 2 * n_slab + j)),
                  pl.BlockSpec((2, NA_ROWS, GRID_W, NA_ROWS * GRID_W), lambda bi, j, i: (j, 0, 0, 0))],
        out_specs=pl.BlockSpec((tq, LANES), lambda bi, j, i: (bi * (s // tq) + i, j)),
        out_shape=jax.ShapeDtypeStruct((b * s, n_slab * LANES), BF16),
        compiler_params=_params("parallel", "parallel", "parallel"),
        name="na_attn",
    )(proj, proj, proj, cc)


def _na_bias_table(rpb):
    var = np.arange(NA_ROWS)[:, None]
    j = np.arange(NA_ROWS)[None, :]
    qc = np.arange(GRID_W)[:, None]
    kc = np.arange(GRID_W)[None, :]
    row_sel = ((j - var + NA_ROWS - 1)[..., None] == np.arange(2 * NA_ROWS - 1)).astype(np.float32)
    col_sel = (np.clip(kc - qc + NA_COLS - 1, 0, 2 * NA_COLS - 2)[..., None]
               == np.arange(2 * NA_COLS - 1)).astype(np.float32)
    win = np.clip(qc - NA_COLS // 2, 0, GRID_W - NA_COLS)
    valid = (kc >= win) & (kc < win + NA_COLS)
    t = jnp.einsum("hab,vja,qkb->hvqjk", rpb.astype(F32), row_sel, col_sel, precision=lax.Precision.HIGHEST)
    t = jnp.where(valid[None, None, :, None, :], t * LOG2E, NEG)
    return t.reshape(rpb.shape[0], NA_ROWS, GRID_W, NA_ROWS * GRID_W)


def _gqa_kernel(q_ref, k_ref, vt_ref, o_ref, acc_ref, st_ref, *, n_kt):
    assert n_kt % 2 == 0
    low = _lane_is_low()
    q = q_ref[...]
    tq = q.shape[0]
    zero = jnp.zeros_like(q)
    qs = jnp.concatenate([jnp.where(low, q, zero), jnp.where(low, zero, q)], axis=0)
    acc_ref[...] = jnp.zeros_like(acc_ref)

    def scores(kt, slot):
        start = pl.multiple_of(kt * GQA_K_TILE, GQA_K_TILE)
        kb = k_ref[pl.ds(start, GQA_K_TILE), :]
        st_ref[slot] = lax.dot_general(kb, qs, (((1,), (1,)), ((), ())), preferred_element_type=F32)

    def softmax_pv(kt, slot, m_old):
        st = st_ref[slot]
        m_new = jnp.maximum(m_old, jnp.max(st, axis=0, keepdims=True))
        alpha = jnp.exp2(m_old - m_new)
        pt = jnp.exp2(st - m_new).astype(BF16)
        acc_ref[...] = alpha * acc_ref[...] + jnp.dot(vt_ref[kt], pt, preferred_element_type=F32)
        return m_new

    def step(i, m):
        kt = 2 * i
        scores(kt + 1, 1)
        m = softmax_pv(kt, 0, m)
        scores(jnp.minimum(kt + 2, n_kt - 1), 0)
        return softmax_pv(kt + 1, 1, m)

    scores(0, 0)
    lax.fori_loop(0, n_kt // 2, step, jnp.full((1, 2 * tq), -jnp.inf, F32))
    out_t = acc_ref[:LANES, :] / acc_ref[LANES:LANES + 1, :]
    o_t = jnp.concatenate([out_t[:HEAD_DIM, :tq], out_t[HEAD_DIM:, tq:]], axis=0)
    o_ref[...] = o_t.T.astype(o_ref.dtype)


def _gqa_attention(proj, row0, b, s):
    tq, tk = GQA_Q_TILE, GQA_K_TILE
    assert s % tk == 0 and row0 % s == 0
    n_slab = GQA_Q_HEADS * HEAD_DIM // LANES
    q_col = 3 * NA_HEADS * HEAD_DIM // LANES
    k_col = q_col + n_slab
    v0 = (k_col + 1) * LANES
    qb0, sb0 = row0 // tq, row0 // s
    v_t = proj[row0:row0 + b * s, v0:v0 + LANES].reshape(b, s // tk, tk, LANES).transpose(0, 1, 3, 2)
    ones_rows = jnp.zeros((b, s // tk, GQA_DEN_ROWS, tk), BF16).at[:, :, 0, :].set(1.0)
    v_t = jnp.concatenate([v_t, ones_rows], axis=2)
    vt_rows = LANES + GQA_DEN_ROWS
    return pl.pallas_call(
        functools.partial(_gqa_kernel, n_kt=s // tk),
        grid=(b, n_slab, s // tq),
        in_specs=[pl.BlockSpec((tq, LANES), lambda bi, j, i: (qb0 + bi * (s // tq) + i, q_col + j)),
                  pl.BlockSpec((s, LANES), lambda bi, j, i: (sb0 + bi, k_col)),
                  pl.BlockSpec((None, s // tk, vt_rows, tk), lambda bi, j, i: (bi, 0, 0, 0))],
        out_specs=pl.BlockSpec((tq, LANES), lambda bi, j, i: (bi * (s // tq) + i, j)),
        out_shape=jax.ShapeDtypeStruct((b * s, n_slab * LANES), BF16),
        scratch_shapes=[pltpu.VMEM((vt_rows, 2 * tq), F32), pltpu.VMEM((2, tk, 2 * tq), F32)],
        compiler_params=_params("parallel", "parallel", "parallel"),
        name="gqa_attn",
    )(proj, proj, v_t)


def _dil_kernel(q_ref, k_ref, v_ref, o_ref, m_ref, l_ref, acc_ref, *, seq):
    tile = q_ref.shape[0]
    t0 = pl.program_id(2) * tile
    low = _lane_is_low()
    sub = WIN_Q_TILE
    n_sub = tile // sub

    for bi, (_, r) in enumerate(DIL_BRANCHES):
        length = seq // r
        win = min(sub + 2 * WIN_RADIUS, length)

        def sub_blocks(g, carry, r=r, length=length, win=win, first=(bi == 0)):
            rows, scores, stats = [], [], []
            for u in range(DIL_GROUP):
                n = g * DIL_GROUP + u
                c = n % r
                blk = n // r
                u0 = t0 // r + blk * sub
                ks = jnp.clip(u0 - WIN_RADIUS, 0, length - win)
                if r == 1:
                    q_rows = pl.ds(pl.multiple_of(n * sub, sub), sub)
                    k_rows = pl.ds(pl.multiple_of(ks, WIN_RADIUS), win)
                else:
                    q_rows = pl.ds(c + r * blk * sub, sub, stride=r)
                    k_rows = pl.ds(c + r * ks, win, stride=r)
                rows.append((q_rows, k_rows))
                q = q_ref[q_rows, :].astype(BF16)
                kwin = k_ref[k_rows, :].astype(BF16)
                qpos = u0 + lax.broadcasted_iota(I32, (sub, win), 0)
                kpos = ks + lax.broadcasted_iota(I32, (sub, win), 1)
                valid = jnp.abs(kpos - qpos) <= WIN_RADIUS
                for half in range(2):
                    qm = jnp.where(low if half == 0 else jnp.logical_not(low), q, jnp.zeros_like(q))
                    s = lax.dot_general(qm, kwin, (((1,), (1,)), ((), ())), preferred_element_type=F32)
                    scores.append(jnp.where(valid, s, NEG))
            for s in scores:
                m = jnp.max(s, axis=-1, keepdims=True)
                p = jnp.exp2(s - m)
                stats.append((m, jnp.sum(p, axis=-1, keepdims=True), p.astype(BF16)))
            for u, (q_rows, k_rows) in enumerate(rows):
                vwin = v_ref[k_rows, :].astype(BF16)
                (m0, l0, p0), (m1, l1, p1) = stats[2 * u:2 * u + 2]
                m_b = jnp.where(low, m0, m1)
                l_b = jnp.where(low, l0, l1)
                pv_b = jnp.where(low, jnp.dot(p0, vwin, preferred_element_type=F32),
                                 jnp.dot(p1, vwin, preferred_element_type=F32))
                if first:
                    m_ref[q_rows, :] = m_b
                    l_ref[q_rows, :] = l_b
                    acc_ref[q_rows, :] = pv_b
                else:
                    m_old = m_ref[q_rows, :]
                    m_new = jnp.maximum(m_old, m_b)
                    a_old = jnp.exp2(m_old - m_new)
                    a_b = jnp.exp2(m_b - m_new)
                    l_ref[q_rows, :] = a_old * l_ref[q_rows, :] + a_b * l_b
                    acc_ref[q_rows, :] = a_old * acc_ref[q_rows, :] + a_b * pv_b
                    m_ref[q_rows, :] = m_new
            return carry

        lax.fori_loop(0, n_sub // DIL_GROUP, sub_blocks, 0)

    o_ref[...] = (acc_ref[...] / l_ref[...]).astype(o_ref.dtype)


def _dilated_attention(qkv, row0, b, s):
    tile = 16 * WIN_Q_TILE
    assert s % tile == 0 and row0 % s == 0
    n_slab = DIL_HEADS * HEAD_DIM // LANES
    qb0, sb0 = row0 // tile, row0 // s
    state = pltpu.VMEM((tile, LANES), F32)
    return pl.pallas_call(
        functools.partial(_dil_kernel, seq=s),
        grid=(b, n_slab, s // tile),
        in_specs=[pl.BlockSpec((tile, LANES), lambda bi, j, i: (qb0 + bi * (s // tile) + i, j)),
                  pl.BlockSpec((s, LANES), lambda bi, j, i: (sb0 + bi, n_slab + j)),
                  pl.BlockSpec((s, LANES), lambda bi, j, i: (sb0 + bi, 2 * n_slab + j))],
        out_specs=pl.BlockSpec((tile, LANES), lambda bi, j, i: (bi * (s // tile) + i, j)),
        out_shape=jax.ShapeDtypeStruct((b * s, n_slab * LANES), BF16),
        scratch_shapes=[state, state, state],
        compiler_params=_params("parallel", "parallel", "parallel"),
        name="dil_attn",
    )(qkv, qkv, qkv)


def _router_kernel(x_ref, wh_ref, wl_ref, b_ref, idx_ref, gate_ref, rank_ref, cnt_ref, base_ref):
    step = pl.program_id(0)

    @pl.when(step == 0)
    def _():
        base_ref[...] = jnp.zeros_like(base_ref)

    x = x_ref[...]
    xh = x.astype(BF16)
    xl = (x - xh.astype(F32)).astype(BF16)
    nt = (((1,), (1,)), ((), ()))
    logits = (lax.dot_general(wh_ref[...], xh, nt, preferred_element_type=F32)
              + lax.dot_general(wl_ref[...], xh, nt, preferred_element_type=F32)
              + lax.dot_general(wh_ref[...], xl, nt, preferred_element_type=F32)) + b_ref[...]
    tm = x.shape[0]
    eid = lax.broadcasted_iota(I32, (N_EXPERTS, tm), 0)
    vals = logits
    top_v, top_i, hots = [], [], []
    for _ in range(TOP_K):
        m = jnp.max(vals, axis=0, keepdims=True)
        idx = jnp.min(jnp.where(vals == m, eid, N_EXPERTS), axis=0, keepdims=True)
        hot = eid == idx
        top_v.append(m)
        top_i.append(idx)
        hots.append(hot)
        vals = jnp.where(hot, -jnp.inf, vals)
    es = [jnp.exp(v - top_v[0]) for v in top_v]
    den = functools.reduce(jnp.add, es)
    chosen = functools.reduce(jnp.logical_or, hots)
    before = (lax.broadcasted_iota(I32, (tm, tm), 0) < lax.broadcasted_iota(I32, (tm, tm), 1))
    prefix = jnp.dot(chosen.astype(BF16), before.astype(BF16), preferred_element_type=F32) + base_ref[...]
    for k in range(TOP_K):
        idx_ref[k:k + 1, :] = top_i[k]
        gate_ref[k:k + 1, :] = es[k] / den
        rank_ref[k:k + 1, :] = jnp.sum(jnp.where(hots[k], prefix, 0.0), axis=0, keepdims=True).astype(I32)
    base_ref[...] = base_ref[...] + jnp.sum(chosen.astype(F32), axis=1, keepdims=True)
    cnt_ref[...] = base_ref[...].astype(I32)


def _router(x, w_hi_t, w_lo_t, bias):
    n, d = x.shape
    tm = TOKEN_TILE
    tok = pl.BlockSpec((TOP_K, tm), lambda i: (0, i))
    fixed = lambda i: (0, 0)
    return pl.pallas_call(
        _router_kernel,
        grid=(n // tm,),
        in_specs=[pl.BlockSpec((tm, d), lambda i: (i, 0)), pl.BlockSpec((N_EXPERTS, d), fixed),
                  pl.BlockSpec((N_EXPERTS, d), fixed), pl.BlockSpec((N_EXPERTS, 1), fixed)],
        out_specs=[tok, tok, tok, pl.BlockSpec((N_EXPERTS, 1), fixed)],
        out_shape=[jax.ShapeDtypeStruct((TOP_K, n), I32), jax.ShapeDtypeStruct((TOP_K, n), F32),
                   jax.ShapeDtypeStruct((TOP_K, n), I32), jax.ShapeDtypeStruct((N_EXPERTS, 1), I32)],
        scratch_shapes=[pltpu.VMEM((N_EXPERTS, 1), F32)],
        compiler_params=_params("arbitrary"),
        name="router",
    )(x, w_hi_t, w_lo_t, bias.reshape(N_EXPERTS, 1).astype(F32))


def _expert_kernel(blk_exp_ref, fresh_ref, n_used_ref, x_ref, w1_ref, b1_ref, w2_ref, b2_ref, o_ref, w1b_ref, w2b_ref):
    blk = pl.program_id(0)

    @pl.when(fresh_ref[blk] == 1)
    def _():
        w1b_ref[...] = w1_ref[...].astype(BF16)
        w2b_ref[...] = w2_ref[...].astype(BF16)

    @pl.when(blk < n_used_ref[0])
    def _():
        half = D_MODEL // 2
        x_lo, x_hi = _unpack_bf16_pairs(x_ref[...])
        h = (jnp.dot(x_lo, w1b_ref[:half, :], preferred_element_type=F32)
             + jnp.dot(x_hi, w1b_ref[half:, :], preferred_element_type=F32)) + b1_ref[...]
        g = jnp.minimum(h[:, :D_FF], SWIGLU_LIMIT)
        u = jnp.clip(h[:, D_FF:], -SWIGLU_LIMIT, SWIGLU_LIMIT)
        act = g * jax.nn.sigmoid(SWIGLU_ALPHA * g) * (u + 1.0)
        o_ref[...] = jnp.dot(act.astype(BF16), w2b_ref[...], preferred_element_type=F32) + b2_ref[...]

    @pl.when(blk >= n_used_ref[0])
    def _():
        o_ref[...] = jnp.zeros_like(o_ref)


def _experts(xs, blk_exp, n_used, layer, w1, b1, w2, b2):
    n_rows = xs.shape[0]
    d = D_MODEL
    bm = EXPERT_ROWS
    f2 = w1.shape[3]
    fresh = jnp.concatenate([jnp.ones((1,), I32), (blk_exp[1:] != blk_exp[:-1]).astype(I32)])
    wmap = lambda i, be, fr, nu: (layer, be[i], 0, 0)
    return pl.pallas_call(
        _expert_kernel,
        grid_spec=pltpu.PrefetchScalarGridSpec(
            num_scalar_prefetch=3,
            grid=(n_rows // bm,),
            in_specs=[pl.BlockSpec((bm, d // 2), lambda i, be, fr, nu: (i, 0)),
                      pl.BlockSpec((None, None, d, f2), wmap),
                      pl.BlockSpec((None, None, 1, f2), wmap),
                      pl.BlockSpec((None, None, f2 // 2, d), wmap),
                      pl.BlockSpec((None, None, 1, d), wmap)],
            out_specs=pl.BlockSpec((bm, d), lambda i, be, fr, nu: (i, 0)),
            scratch_shapes=[pltpu.VMEM((d, f2), BF16), pltpu.VMEM((f2 // 2, d), BF16)],
        ),
        out_shape=jax.ShapeDtypeStruct((n_rows, d), F32),
        compiler_params=_params("arbitrary"),
        name="experts",
    )(blk_exp, fresh, n_used, xs, w1, b1, w2, b2)


def _combine_ln_kernel(y_ref, gate_ref, x_ref, g_ref, b_ref, o_ref):
    gate = gate_ref[...]
    y = functools.reduce(jnp.add, [gate[:, k:k + 1] * y_ref[k] for k in range(TOP_K)])
    o_ref[...] = _layer_norm_rows(DN_ALPHA * x_ref[...] + y, g_ref[...], b_ref[...])


def _combine_ln(yg, gate, x, g, b, row0=0, n_rows=None):
    n, d = x.shape
    n_rows = n if n_rows is None else n_rows
    tm = TOKEN_TILE
    assert row0 % tm == 0 and n_rows % tm == 0
    blk0 = row0 // tm
    row = lambda i: (blk0 + i, 0)
    fixed = lambda i: (0, 0)
    return pl.pallas_call(
        _combine_ln_kernel,
        grid=(n_rows // tm,),
        in_specs=[pl.BlockSpec((TOP_K, tm, d), lambda i: (0, blk0 + i, 0)), pl.BlockSpec((tm, TOP_K), row),
                  pl.BlockSpec((tm, d), row), pl.BlockSpec((1, d), fixed), pl.BlockSpec((1, d), fixed)],
        out_specs=pl.BlockSpec((tm, d), lambda i: (i, 0)),
        out_shape=jax.ShapeDtypeStruct((n_rows, d), F32),
        compiler_params=_params("parallel"),
        name="combine_ln",
    )(yg, gate, x, g.reshape(1, d), b.reshape(1, d))


def _sc_mesh():
    return plsc.VectorSubcoreMesh(core_axis_name="core", subcore_axis_name="subcore")


def _sc_scatter_rows(x, dest, n_out):
    n, d = x.shape
    mesh = _sc_mesh()
    per_worker = n // (mesh.num_cores * mesh.num_subcores)
    n_sub = SC_INDEX_WINDOW // SC_ROW_WINDOW
    assert per_worker % SC_INDEX_WINDOW == 0 and n_sub >= 2

    @functools.partial(
        pl.kernel, out_type=jax.ShapeDtypeStruct((n_out, d), x.dtype), mesh=mesh,
        scratch_types=[pltpu.VMEM((TOP_K, SC_INDEX_WINDOW), I32), pltpu.VMEM((2, SC_ROW_WINDOW, d), x.dtype),
                       pltpu.SemaphoreType.DMA((2,)), pltpu.SemaphoreType.DMA((2,))])
    def scatter(x_hbm, i_hbm, o_hbm, idx_v, buf, sem_r, sem_s):
        wid = lax.axis_index("core") * mesh.num_subcores + lax.axis_index("subcore")

        @pl.loop(0, per_worker // SC_INDEX_WINDOW)
        def _(it):
            base = wid * per_worker + it * SC_INDEX_WINDOW
            for k in range(TOP_K):
                pltpu.sync_copy(i_hbm.at[k, pl.ds(base, SC_INDEX_WINDOW)], idx_v.at[k])

            def read(j):
                rows = pl.ds(base + j * SC_ROW_WINDOW, SC_ROW_WINDOW)
                return pltpu.make_async_copy(x_hbm.at[rows], buf.at[j % 2], sem_r.at[j % 2])

            def send(j, k):
                rows = idx_v.at[k, pl.ds(j * SC_ROW_WINDOW, SC_ROW_WINDOW)]
                return pltpu.make_async_copy(buf.at[j % 2], o_hbm.at[rows], sem_s.at[j % 2])

            read(0).start()
            for j in range(n_sub):
                read(j).wait()
                for k in range(TOP_K):
                    send(j, k).start()
                if j + 1 < n_sub:
                    if j >= 1:
                        for k in range(TOP_K):
                            send(j - 1, k).wait()
                    read(j + 1).start()
            for j in (n_sub - 2, n_sub - 1):
                for k in range(TOP_K):
                    send(j, k).wait()

    return scatter(x, dest)


def _sc_gather_rows(table, indices):
    num = indices.shape[0]
    d = table.shape[1]
    mesh = _sc_mesh()
    per_worker = num // (mesh.num_cores * mesh.num_subcores)
    n_sub = SC_INDEX_WINDOW // SC_ROW_WINDOW
    assert per_worker % SC_INDEX_WINDOW == 0 and n_sub >= 2

    @functools.partial(
        pl.kernel, out_type=jax.ShapeDtypeStruct((num, d), table.dtype), mesh=mesh,
        scratch_types=[pltpu.VMEM((SC_INDEX_WINDOW,), I32), pltpu.VMEM((2, SC_ROW_WINDOW, d), table.dtype),
                       pltpu.SemaphoreType.DMA((2,)), pltpu.SemaphoreType.DMA((2,))])
    def gather(x_hbm, i_hbm, o_hbm, idx_v, buf, sem_g, sem_w):
        wid = lax.axis_index("core") * mesh.num_subcores + lax.axis_index("subcore")

        @pl.loop(0, per_worker // SC_INDEX_WINDOW)
        def _(it):
            base = wid * per_worker + it * SC_INDEX_WINDOW
            pltpu.sync_copy(i_hbm.at[pl.ds(base, SC_INDEX_WINDOW)], idx_v)

            def fetch(j):
                rows = idx_v.at[pl.ds(j * SC_ROW_WINDOW, SC_ROW_WINDOW)]
                return pltpu.make_async_copy(x_hbm.at[rows], buf.at[j % 2], sem_g.at[j % 2])

            def write(j):
                rows = pl.ds(base + j * SC_ROW_WINDOW, SC_ROW_WINDOW)
                return pltpu.make_async_copy(buf.at[j % 2], o_hbm.at[rows], sem_w.at[j % 2])

            fetch(0).start()
            for j in range(n_sub):
                fetch(j).wait()
                write(j).start()
                if j + 1 < n_sub:
                    if j >= 1:
                        write(j - 1).wait()
                    fetch(j + 1).start()
            write(n_sub - 2).wait()
            write(n_sub - 1).wait()

    return gather(table, indices)


def _moe(x, x_packed, w_r, b_r, layer, w1, b1, w2, b2):
    n, d = x.shape
    bm = EXPERT_ROWS
    wr_t = w_r.T
    wr_hi = wr_t.astype(BF16)
    wr_lo = (wr_t - wr_hi.astype(F32)).astype(BF16)
    idx, gate, rank, counts = _router(x, wr_hi, wr_lo, b_r)
    counts = counts[:, 0]
    padded = (counts + bm - 1) // bm * bm
    pad_ends = jnp.cumsum(padded)
    pad_starts = pad_ends - padded
    hot = idx[:, None, :] == jnp.arange(N_EXPERTS, dtype=I32)[None, :, None]
    dest = jnp.sum(jnp.where(hot, pad_starts[None, :, None], 0), axis=1) + rank
    n_blocks = n * TOP_K // bm + N_EXPERTS
    blk_start = jnp.arange(n_blocks, dtype=I32) * bm
    blk_exp = jnp.minimum(jnp.sum(blk_start[:, None] >= pad_ends[None, :], axis=1), N_EXPERTS - 1).astype(I32)
    n_used = (pad_ends[-1:] // bm).astype(I32)
    xs = _sc_scatter_rows(x_packed, dest, n_blocks * bm)
    ys = _experts(xs, blk_exp, n_used, layer, w1, b1, w2, b2)
    yg = _sc_gather_rows(ys, dest.reshape(-1)).reshape(TOP_K, n, d)
    return yg, gate.T


def _positions(groups):
    return jnp.concatenate([jnp.tile(jnp.arange(s), b) for b, s in groups])


def _axial_tables(groups):
    t = _positions(groups)
    n = HEAD_DIM // 4
    inv = AXIAL_THETA ** (-jnp.arange(n, dtype=F32) / n)
    ar = (t // GRID_W).astype(F32)[:, None] * inv
    ac = (t % GRID_W).astype(F32)[:, None] * inv
    z = jnp.zeros_like(ar)
    cr, sr, cc, sc = jnp.cos(ar), jnp.sin(ar), jnp.cos(ac), jnp.sin(ac)
    c = jnp.concatenate([cr, cr, cc, cc], axis=-1)
    s1 = jnp.concatenate([z, sr, z, sc], axis=-1)
    s2 = jnp.concatenate([-sr, z, -sc, z], axis=-1)
    return tuple(jnp.tile(a, (1, 2)) for a in (c, s1, s2))


def _rope_tables(groups):
    t = _positions(groups)
    n = ROPE_DIMS // 2
    inv = ROPE_THETA ** (-jnp.arange(n, dtype=F32) / n)
    ang = t.astype(F32)[:, None] * inv
    c, s = jnp.cos(ang), jnp.sin(ang)
    z = jnp.zeros_like(c)
    rest = HEAD_DIM - ROPE_DIMS
    pad1 = jnp.ones((t.shape[0], rest), F32)
    pad0 = jnp.zeros((t.shape[0], rest), F32)
    cc = jnp.concatenate([c, c, pad1], axis=-1)
    s1 = jnp.concatenate([z, s, pad0], axis=-1)
    s2 = jnp.concatenate([-s, z, pad0], axis=-1)
    return tuple(jnp.tile(a, (1, 2)) for a in (cc, s1, s2))


def _gqa_head_order():
    g = GQA_Q_HEADS // GQA_KV_HEADS
    return [h for j in range(g) for h in (j, g + j)]


def _mixer_even(x, groups, tabs, w_in, rpb, q_gain, k_gain, w_out):
    hd = HEAD_DIM
    na_w = NA_HEADS * hd
    order = _gqa_head_order()
    q0 = 3 * na_w
    q_cols = np.concatenate([q0 + h * hd + np.arange(hd) for h in order])
    w_in_p = jnp.concatenate([w_in[:, :q0], w_in[:, q_cols], w_in[:, q0 + GQA_Q_HEADS * hd:]], axis=1).astype(BF16)
    out_rows = np.concatenate([na_w + h * hd + np.arange(hd) for h in order])
    w_out_p = jnp.concatenate([w_out[:na_w], w_out[out_rows]], axis=0).astype(BF16)
    n_na = na_w // LANES
    n_q = GQA_Q_HEADS * hd // LANES
    modes = ([("plain", Q_SCALE, 0)] * n_na + [("plain", 1.0, 0)] * (2 * n_na)
             + [("norm_rope", Q_SCALE, 0)] * n_q + [("norm_rope", 1.0, 1)] + [("plain", 1.0, 0)])
    gains = jnp.stack([jnp.tile(q_gain.astype(F32), 2), jnp.tile(k_gain.astype(F32), 2)])
    proj = _project(x, w_in_p, tabs, gains, modes, HEAD_DIM // 4)
    cc = _na_bias_table(rpb)
    ya, yb = [], []
    row0 = 0
    for b, s in groups:
        ya.append(_na_attention(proj, cc, row0, b, s))
        yb.append(_gqa_attention(proj, row0, b, s))
        row0 += b * s
    attn = jnp.concatenate([jnp.concatenate(ya, axis=0), jnp.concatenate(yb, axis=0)], axis=1)
    return attn, w_out_p


def _mixer_odd(x, groups, tabs, w_in, w_out):
    n_slab = DIL_HEADS * HEAD_DIM // LANES
    modes = [("rope", Q_SCALE, 0)] * n_slab + [("rope", 1.0, 0)] * n_slab + [("plain", 1.0, 0)] * n_slab
    gains = jnp.ones((1, LANES), F32)
    qkv = _project(x, w_in.astype(BF16), tabs, gains, modes, ROPE_DIMS // 2, out_dtype=F32)
    parts, row0 = [], 0
    for b, s in groups:
        parts.append(_dilated_attention(qkv, row0, b, s))
        row0 += b * s
    return jnp.concatenate(parts, axis=0), w_out.astype(BF16)


def _trunk(xs, w_in_even, rpb_a, q_gain_b, k_gain_b, w_out_even, w_in_odd, w_out_odd,
           ln1_g, ln1_b, ln2_g, ln2_b, router_w, router_b, moe_w1, moe_b1, moe_w2, moe_b2):
    groups = [(x.shape[0], x.shape[1]) for x in xs]
    x = jnp.concatenate([t.reshape(-1, D_MODEL) for t in xs], axis=0).astype(F32)
    tabs_even = _axial_tables(groups)
    tabs_odd = _rope_tables(groups)
    b1 = moe_b1.astype(F32)[:, :, None, :]
    b2 = moe_b2.astype(F32)[:, :, None, :]
    for l in range(DEPTH):
        i = l // 2
        if l % 2 == 0:
            attn, w_out = _mixer_even(x, groups, tabs_even, w_in_even[i], rpb_a[i], q_gain_b[i], k_gain_b[i],
                                      w_out_even[i])
        else:
            attn, w_out = _mixer_odd(x, groups, tabs_odd, w_in_odd[i], w_out_odd[i])
        x, x_packed = _outproj_ln(attn, w_out, x, ln1_g[l], ln1_b[l])
        yg, gate = _moe(x, x_packed, router_w[l], router_b[l], l, moe_w1, b1, moe_w2, b2)
        if l + 1 < DEPTH:
            x = _combine_ln(yg, gate, x, ln2_g[l], ln2_b[l])
    outs, row0 = [], 0
    for t in xs:
        n = t.shape[0] * t.shape[1]
        outs.append(_combine_ln(yg, gate, x, ln2_g[-1], ln2_b[-1], row0, n).reshape(t.shape))
        row0 += n
    return tuple(outs)


def kernel(x_prompt, x_sample, w_in_even, rpb_a, q_gain_b, k_gain_b, w_out_even, w_in_odd, w_out_odd, ln1_g, ln1_b,
           ln2_g, ln2_b, router_w, router_b, moe_w1, moe_b1, moe_w2, moe_b2):
    params = (w_in_even, rpb_a, q_gain_b, k_gain_b, w_out_even, w_in_odd, w_out_odd,
              ln1_g, ln1_b, ln2_g, ln2_b, router_w, router_b, moe_w1, moe_b1, moe_w2, moe_b2)
    (y_prompt,) = _trunk((x_prompt,), *params)
    (y_sample,) = _trunk((x_sample,), *params)
    return y_prompt, y_sample
```

```python
import functools

import jax
import jax.numpy as jnp
import numpy as np
from jax import lax
from jax.experimental import pallas as pl
from jax.experimental.pallas import tpu as pltpu
from jax.experimental.pallas import tpu_sc as plsc

F32 = jnp.float32
BF16 = jnp.bfloat16
I32 = jnp.int32

D_MODEL = 1024
DEPTH = 4
HEAD_DIM = 64
GRID_W = 64
NA_HEADS = 8
NA_ROWS = 8
NA_COLS = 16
GQA_Q_HEADS = 8
GQA_KV_HEADS = 2
AXIAL_THETA = 10000.0
QK_NORM_EPS = 1e-6
DIL_HEADS = 16
DIL_BRANCHES = ((128, 1), (512, 4), (2048, 16))
ROPE_THETA = 500000.0
ROPE_DIMS = HEAD_DIM // 4
N_EXPERTS = 32
TOP_K = 4
D_FF = D_MODEL
SWIGLU_LIMIT = 7.0
SWIGLU_ALPHA = 1.702
DN_ALPHA = (2 * DEPTH) ** 0.25
LN_EPS = 1e-5
LOG2E = 1.4426950408889634
Q_SCALE = HEAD_DIM ** -0.5 * LOG2E

LANES = 128
NEG = -1e30
VMEM_LIMIT = 56 * 1024 * 1024
TOKEN_TILE = 512
EXPERT_ROWS = 256
NA_ROW_BLOCK = 8
NA_ROW_GROUP = 4
GQA_Q_TILE = 512
GQA_K_TILE = 512
GQA_DEN_ROWS = 16
WIN_Q_TILE = 128
WIN_RADIUS = 64
DIL_GROUP = 4
SC_INDEX_WINDOW = 128
SC_ROW_WINDOW = 32

EVEN_IN = 3 * NA_HEADS * HEAD_DIM + GQA_Q_HEADS * HEAD_DIM + 2 * GQA_KV_HEADS * HEAD_DIM


def _params(*sem):
    return pltpu.CompilerParams(dimension_semantics=sem, vmem_limit_bytes=VMEM_LIMIT)


def _lane_is_low():
    return lax.broadcasted_iota(I32, (1, LANES), 1) < HEAD_DIM


def _proj_kernel(x_ref, w_ref, c_ref, s1_ref, s2_ref, gain_ref, o_ref, *, slab_modes, shift):
    x = x_ref[...].astype(BF16)
    n_out = o_ref.shape[1]
    chunk = 2 * LANES
    if any(m[0] == "norm_rope" for m in slab_modes):
        r = lax.broadcasted_iota(I32, (LANES, LANES), 0) // HEAD_DIM
        c = lax.broadcasted_iota(I32, (LANES, LANES), 1) // HEAD_DIM
        head_mean = jnp.where(r == c, 1.0 / HEAD_DIM, 0.0).astype(BF16)
    for c0 in range(0, n_out, chunk):
        acc = jnp.dot(x, w_ref[:, c0:c0 + chunk], preferred_element_type=F32)
        for s in range(chunk // LANES):
            slab = c0 // LANES + s
            mode, scale, gidx = slab_modes[slab]
            y = acc[:, s * LANES:(s + 1) * LANES]
            if mode == "norm_rope":
                sq = y * y
                hi = sq.astype(BF16)
                lo = (sq - hi.astype(F32)).astype(BF16)
                ms = (jnp.dot(hi, head_mean, preferred_element_type=F32)
                      + jnp.dot(lo, head_mean, preferred_element_type=F32))
                y = y * lax.rsqrt(ms + QK_NORM_EPS) * gain_ref[gidx:gidx + 1, :]
            if mode in ("rope", "norm_rope"):
                y = (y * c_ref[...] + pltpu.roll(y, LANES - shift, 1) * s2_ref[...]
                     + pltpu.roll(y, shift, 1) * s1_ref[...])
            if scale != 1.0:
                y = y * scale
            o_ref[:, slab * LANES:(slab + 1) * LANES] = y.astype(o_ref.dtype)


def _project(x, w, tabs, gains, slab_modes, shift, out_dtype=BF16):
    n, d = x.shape
    m = w.shape[1]
    tm = TOKEN_TILE
    tab_spec = pl.BlockSpec((tm, LANES), lambda i: (i, 0))
    return pl.pallas_call(
        functools.partial(_proj_kernel, slab_modes=tuple(slab_modes), shift=shift),
        grid=(n // tm,),
        in_specs=[pl.BlockSpec((tm, d), lambda i: (i, 0)),
                  pl.BlockSpec((d, m), lambda i: (0, 0)),
                  tab_spec, tab_spec, tab_spec,
                  pl.BlockSpec(gains.shape, lambda i: (0, 0))],
        out_specs=pl.BlockSpec((tm, m), lambda i: (i, 0)),
        out_shape=jax.ShapeDtypeStruct((n, m), out_dtype),
        compiler_params=_params("parallel"),
        name="in_proj",
    )(x, w, tabs[0], tabs[1], tabs[2], gains)


def _layer_norm_rows(z, g, b):
    mu = jnp.mean(z, axis=-1, keepdims=True)
    zc = z - mu
    var = jnp.mean(zc * zc, axis=-1, keepdims=True)
    return zc * lax.rsqrt(var + LN_EPS) * g + b


def _pack_bf16_pairs(v):
    half = v.shape[1] // 2
    bits = pltpu.bitcast(v.astype(BF16).astype(F32), jnp.uint32)
    return (bits[:, :half] >> 16) | bits[:, half:]


def _unpack_bf16_pairs(w):
    lo = pltpu.bitcast(w << 16, F32).astype(BF16)
    hi = pltpu.bitcast(w & jnp.uint32(0xFFFF0000), F32).astype(BF16)
    return lo, hi


def _outproj_ln_kernel(a_ref, w_ref, x_ref, g_ref, b_ref, o_ref, p_ref):
    y = jnp.dot(a_ref[...], w_ref[...], preferred_element_type=F32)
    out = _layer_norm_rows(DN_ALPHA * x_ref[...] + y, g_ref[...], b_ref[...])
    o_ref[...] = out
    p_ref[...] = _pack_bf16_pairs(out)


def _outproj_ln(a, w, x, g, b):
    n, d = x.shape
    tm = TOKEN_TILE
    row = lambda i: (i, 0)
    fixed = lambda i: (0, 0)
    return pl.pallas_call(
        _outproj_ln_kernel,
        grid=(n // tm,),
        in_specs=[pl.BlockSpec((tm, a.shape[1]), row), pl.BlockSpec(w.shape, fixed),
                  pl.BlockSpec((tm, d), row), pl.BlockSpec((1, d), fixed), pl.BlockSpec((1, d), fixed)],
        out_specs=[pl.BlockSpec((tm, d), row), pl.BlockSpec((tm, d // 2), row)],
        out_shape=[jax.ShapeDtypeStruct((n, d), F32), jax.ShapeDtypeStruct((n, d // 2), jnp.uint32)],
        compiler_params=_params("parallel"),
        name="out_proj_ln",
    )(a, w, x, g.reshape(1, d), b.reshape(1, d))


def _na_kernel(q_ref, k_ref, v_ref, cc_ref, o_ref, *, rows):
    i = pl.program_id(2)
    low = _lane_is_low()

    for g0 in range(0, NA_ROW_BLOCK, NA_ROW_GROUP):
        wins, scores = [], []
        for rr in range(g0, g0 + NA_ROW_GROUP):
            r = i * NA_ROW_BLOCK + rr
            rs = jnp.clip(r - NA_ROWS // 2, 0, rows - NA_ROWS)
            var = r - rs
            q = q_ref[rr * GRID_W:(rr + 1) * GRID_W, :]
            kstart = pl.multiple_of(rs * GRID_W, GRID_W)
            kwin = k_ref[pl.ds(kstart, NA_ROWS * GRID_W), :]
            wins.append(kstart)
            for half in range(2):
                qm = jnp.where(low if half == 0 else jnp.logical_not(low), q, jnp.zeros_like(q))
                s = lax.dot_general(qm, kwin, (((1,), (1,)), ((), ())), preferred_element_type=F32)
                scores.append(s + cc_ref[half, var])
        probs = []
        for s in scores:
            p = jnp.exp2(s - jnp.max(s, axis=-1, keepdims=True))
            probs.append((p.astype(BF16), jnp.sum(p, axis=-1, keepdims=True)))
        for u, rr in enumerate(range(g0, g0 + NA_ROW_GROUP)):
            vwin = v_ref[pl.ds(wins[u], NA_ROWS * GRID_W), :]
            outs = [jnp.dot(p, vwin, preferred_element_type=F32) / l for p, l in probs[2 * u:2 * u + 2]]
            o_ref[rr * GRID_W:(rr + 1) * GRID_W, :] = jnp.where(low, outs[0], outs[1]).astype(o_ref.dtype)


def _na_attention(proj, cc, row0, b, s):
    rows = s // GRID_W
    assert rows >= NA_ROWS and rows % NA_ROW_BLOCK == 0 and row0 % s == 0
    tq = NA_ROW_BLOCK * GRID_W
    n_slab = NA_HEADS * HEAD_DIM // LANES
    qb0, sb0 = row0 // tq, row0 // s
    return pl.pallas_call(
        functools.partial(_na_kernel, rows=rows),
        grid=(b, n_slab, s // tq),
        in_specs=[pl.BlockSpec((tq, LANES), lambda bi, j, i: (qb0 + bi * (s // tq) + i, j)),
                  pl.BlockSpec((s, LANES), lambda bi, j, i: (sb0 + bi, n_slab + j)),
                  pl.BlockSpec((s, LANES), lambda bi, j, i: (sb0 + bi, 2 * n_slab + j)),
                  pl.BlockSpec((2, NA_ROWS, GRID_W, NA_ROWS * GRID_W), lambda bi, j, i: (j, 0, 0, 0))],
        out_specs=pl.BlockSpec((tq, LANES), lambda bi, j, i: (bi * (s // tq) + i, j)),
        out_shape=jax.ShapeDtypeStruct((b * s, n_slab * LANES), BF16),
        compiler_params=_params("parallel", "parallel", "parallel"),
        name="na_attn",
    )(proj, proj, proj, cc)


def _na_bias_table(rpb):
    var = np.arange(NA_ROWS)[:, None]
    j = np.arange(NA_ROWS)[None, :]
    qc = np.arange(GRID_W)[:, None]
    kc = np.arange(GRID_W)[None, :]
    row_sel = ((j - var + NA_ROWS - 1)[..., None] == np.arange(2 * NA_ROWS - 1)).astype(np.float32)
    col_sel = (np.clip(kc - qc + NA_COLS - 1, 0, 2 * NA_COLS - 2)[..., None]
               == np.arange(2 * NA_COLS - 1)).astype(np.float32)
    win = np.clip(qc - NA_COLS // 2, 0, GRID_W - NA_COLS)
    valid = (kc >= win) & (kc < win + NA_COLS)
    t = jnp.einsum("hab,vja,qkb->hvqjk", rpb.astype(F32), row_sel, col_sel, precision=lax.Precision.HIGHEST)
    t = jnp.where(valid[None, None, :, None, :], t * LOG2E, NEG)
    return t.reshape(rpb.shape[0], NA_ROWS, GRID_W, NA_ROWS * GRID_W)


def _gqa_kernel(q_ref, k_ref, vt_ref, o_ref, acc_ref, st_ref, *, n_kt):
    assert n_kt % 2 == 0
    low = _lane_is_low()
    q = q_ref[...]
    tq = q.shape[0]
    zero = jnp.zeros_like(q)
    qs = jnp.concatenate([jnp.where(low, q, zero), jnp.where(low, zero, q)], axis=0)
    acc_ref[...] = jnp.zeros_like(acc_ref)

    def scores(kt, slot):
        start = pl.multiple_of(kt * GQA_K_TILE, GQA_K_TILE)
        kb = k_ref[pl.ds(start, GQA_K_TILE), :]
        st_ref[slot] = lax.dot_general(kb, qs, (((1,), (1,)), ((), ())), preferred_element_type=F32)

    def softmax_pv(kt, slot, m_old):
        st = st_ref[slot]
        m_new = jnp.maximum(m_old, jnp.max(st, axis=0, keepdims=True))
        alpha = jnp.exp2(m_old - m_new)
        pt = jnp.exp2(st - m_new).astype(BF16)
        acc_ref[...] = alpha * acc_ref[...] + jnp.dot(vt_ref[kt], pt, preferred_element_type=F32)
        return m_new

    def step(i, m):
        kt = 2 * i
        scores(kt + 1, 1)
        m = softmax_pv(kt, 0, m)
        scores(jnp.minimum(kt + 2, n_kt - 1), 0)
        return softmax_pv(kt + 1, 1, m)

    scores(0, 0)
    lax.fori_loop(0, n_kt // 2, step, jnp.full((1, 2 * tq), -jnp.inf, F32))
    out_t = acc_ref[:LANES, :] / acc_ref[LANES:LANES + 1, :]
    o_t = jnp.concatenate([out_t[:HEAD_DIM, :tq], out_t[HEAD_DIM:, tq:]], axis=0)
    o_ref[...] = o_t.T.astype(o_ref.dtype)


def _gqa_attention(proj, row0, b, s):
    tq, tk = GQA_Q_TILE, GQA_K_TILE
    assert s % tk == 0 and row0 % s == 0
    n_slab = GQA_Q_HEADS * HEAD_DIM // LANES
    q_col = 3 * NA_HEADS * HEAD_DIM // LANES
    k_col = q_col + n_slab
    v0 = (k_col + 1) * LANES
    qb0, sb0 = row0 // tq, row0 // s
    v_t = proj[row0:row0 + b * s, v0:v0 + LANES].reshape(b, s // tk, tk, LANES).transpose(0, 1, 3, 2)
    ones_rows = jnp.zeros((b, s // tk, GQA_DEN_ROWS, tk), BF16).at[:, :, 0, :].set(1.0)
    v_t = jnp.concatenate([v_t, ones_rows], axis=2)
    vt_rows = LANES + GQA_DEN_ROWS
    return pl.pallas_call(
        functools.partial(_gqa_kernel, n_kt=s // tk),
        grid=(b, n_slab, s // tq),
        in_specs=[pl.BlockSpec((tq, LANES), lambda bi, j, i: (qb0 + bi * (s // tq) + i, q_col + j)),
                  pl.BlockSpec((s, LANES), lambda bi, j, i: (sb0 + bi, k_col)),
                  pl.BlockSpec((None, s // tk, vt_rows, tk), lambda bi, j, i: (bi, 0, 0, 0))],
        out_specs=pl.BlockSpec((tq, LANES), lambda bi, j, i: (bi * (s // tq) + i, j)),
        out_shape=jax.ShapeDtypeStruct((b * s, n_slab * LANES), BF16),
        scratch_shapes=[pltpu.VMEM((vt_rows, 2 * tq), F32), pltpu.VMEM((2, tk, 2 * tq), F32)],
        compiler_params=_params("parallel", "parallel", "parallel"),
        name="gqa_attn",
    )(proj, proj, v_t)


def _dil_kernel(q_ref, k_ref, v_ref, o_ref, m_ref, l_ref, acc_ref, *, seq):
    tile = q_ref.shape[0]
    t0 = pl.program_id(2) * tile
    low = _lane_is_low()
    sub = WIN_Q_TILE
    n_sub = tile // sub

    for bi, (_, r) in enumerate(DIL_BRANCHES):
        length = seq // r
        win = min(sub + 2 * WIN_RADIUS, length)

        def sub_blocks(g, carry, r=r, length=length, win=win, first=(bi == 0)):
            rows, scores, stats = [], [], []
            for u in range(DIL_GROUP):
                n = g * DIL_GROUP + u
                c = n % r
                blk = n // r
                u0 = t0 // r + blk * sub
                ks = jnp.clip(u0 - WIN_RADIUS, 0, length - win)
                if r == 1:
                    q_rows = pl.ds(pl.multiple_of(n * sub, sub), sub)
                    k_rows = pl.ds(pl.multiple_of(ks, WIN_RADIUS), win)
                else:
                    q_rows = pl.ds(c + r * blk * sub, sub, stride=r)
                    k_rows = pl.ds(c + r * ks, win, stride=r)
                rows.append((q_rows, k_rows))
                q = q_ref[q_rows, :].astype(BF16)
                kwin = k_ref[k_rows, :].astype(BF16)
                qpos = u0 + lax.broadcasted_iota(I32, (sub, win), 0)
                kpos = ks + lax.broadcasted_iota(I32, (sub, win), 1)
                valid = jnp.abs(kpos - qpos) <= WIN_RADIUS
                for half in range(2):
                    qm = jnp.where(low if half == 0 else jnp.logical_not(low), q, jnp.zeros_like(q))
                    s = lax.dot_general(qm, kwin, (((1,), (1,)), ((), ())), preferred_element_type=F32)
                    scores.append(jnp.where(valid, s, NEG))
            for s in scores:
                m = jnp.max(s, axis=-1, keepdims=True)
                p = jnp.exp2(s - m)
                stats.append((m, jnp.sum(p, axis=-1, keepdims=True), p.astype(BF16)))
            for u, (q_rows, k_rows) in enumerate(rows):
                vwin = v_ref[k_rows, :].astype(BF16)
                (m0, l0, p0), (m1, l1, p1) = stats[2 * u:2 * u + 2]
                m_b = jnp.where(low, m0, m1)
                l_b = jnp.where(low, l0, l1)
                pv_b = jnp.where(low, jnp.dot(p0, vwin, preferred_element_type=F32),
                                 jnp.dot(p1, vwin, preferred_element_type=F32))
                if first:
                    m_ref[q_rows, :] = m_b
                    l_ref[q_rows, :] = l_b
                    acc_ref[q_rows, :] = pv_b
                else:
                    m_old = m_ref[q_rows, :]
                    m_new = jnp.maximum(m_old, m_b)
                    a_old = jnp.exp2(m_old - m_new)
                    a_b = jnp.exp2(m_b - m_new)
                    l_ref[q_rows, :] = a_old * l_ref[q_rows, :] + a_b * l_b
                    acc_ref[q_rows, :] = a_old * acc_ref[q_rows, :] + a_b * pv_b
                    m_ref[q_rows, :] = m_new
            return carry

        lax.fori_loop(0, n_sub // DIL_GROUP, sub_blocks, 0)

    o_ref[...] = (acc_ref[...] / l_ref[...]).astype(o_ref.dtype)


def _dilated_attention(qkv, row0, b, s):
    tile = 16 * WIN_Q_TILE
    assert s % tile == 0 and row0 % s == 0
    n_slab = DIL_HEADS * HEAD_DIM // LANES
    qb0, sb0 = row0 // tile, row0 // s
    state = pltpu.VMEM((tile, LANES), F32)
    return pl.pallas_call(
        functools.partial(_dil_kernel, seq=s),
        grid=(b, n_slab, s // tile),
        in_specs=[pl.BlockSpec((tile, LANES), lambda bi, j, i: (qb0 + bi * (s // tile) + i, j)),
                  pl.BlockSpec((s, LANES), lambda bi, j, i: (sb0 + bi, n_slab + j)),
                  pl.BlockSpec((s, LANES), lambda bi, j, i: (sb0 + bi, 2 * n_slab + j))],
        out_specs=pl.BlockSpec((tile, LANES), lambda bi, j, i: (bi * (s // tile) + i, j)),
        out_shape=jax.ShapeDtypeStruct((b * s, n_slab * LANES), BF16),
        scratch_shapes=[state, state, state],
        compiler_params=_params("parallel", "parallel", "parallel"),
        name="dil_attn",
    )(qkv, qkv, qkv)


def _router_kernel(x_ref, wh_ref, wl_ref, b_ref, idx_ref, gate_ref, rank_ref, cnt_ref, base_ref):
    step = pl.program_id(0)

    @pl.when(step == 0)
    def _():
        base_ref[...] = jnp.zeros_like(base_ref)

    x = x_ref[...]
    xh = x.astype(BF16)
    xl = (x - xh.astype(F32)).astype(BF16)
    nt = (((1,), (1,)), ((), ()))
    logits = (lax.dot_general(wh_ref[...], xh, nt, preferred_element_type=F32)
              + lax.dot_general(wl_ref[...], xh, nt, preferred_element_type=F32)
              + lax.dot_general(wh_ref[...], xl, nt, preferred_element_type=F32)) + b_ref[...]
    tm = x.shape[0]
    eid = lax.broadcasted_iota(I32, (N_EXPERTS, tm), 0)
    vals = logits
    top_v, top_i, hots = [], [], []
    for _ in range(TOP_K):
        m = jnp.max(vals, axis=0, keepdims=True)
        idx = jnp.min(jnp.where(vals == m, eid, N_EXPERTS), axis=0, keepdims=True)
        hot = eid == idx
        top_v.append(m)
        top_i.append(idx)
        hots.append(hot)
        vals = jnp.where(hot, -jnp.inf, vals)
    es = [jnp.exp(v - top_v[0]) for v in top_v]
    den = functools.reduce(jnp.add, es)
    chosen = functools.reduce(jnp.logical_or, hots)
    before = (lax.broadcasted_iota(I32, (tm, tm), 0) < lax.broadcasted_iota(I32, (tm, tm), 1))
    prefix = jnp.dot(chosen.astype(BF16), before.astype(BF16), preferred_element_type=F32) + base_ref[...]
    for k in range(TOP_K):
        idx_ref[k:k + 1, :] = top_i[k]
        gate_ref[k:k + 1, :] = es[k] / den
        rank_ref[k:k + 1, :] = jnp.sum(jnp.where(hots[k], prefix, 0.0), axis=0, keepdims=True).astype(I32)
    base_ref[...] = base_ref[...] + jnp.sum(chosen.astype(F32), axis=1, keepdims=True)
    cnt_ref[...] = base_ref[...].astype(I32)


def _router(x, w_hi_t, w_lo_t, bias):
    n, d = x.shape
    tm = TOKEN_TILE
    tok = pl.BlockSpec((TOP_K, tm), lambda i: (0, i))
    fixed = lambda i: (0, 0)
    return pl.pallas_call(
        _router_kernel,
        grid=(n // tm,),
        in_specs=[pl.BlockSpec((tm, d), lambda i: (i, 0)), pl.BlockSpec((N_EXPERTS, d), fixed),
                  pl.BlockSpec((N_EXPERTS, d), fixed), pl.BlockSpec((N_EXPERTS, 1), fixed)],
        out_specs=[tok, tok, tok, pl.BlockSpec((N_EXPERTS, 1), fixed)],
        out_shape=[jax.ShapeDtypeStruct((TOP_K, n), I32), jax.ShapeDtypeStruct((TOP_K, n), F32),
                   jax.ShapeDtypeStruct((TOP_K, n), I32), jax.ShapeDtypeStruct((N_EXPERTS, 1), I32)],
        scratch_shapes=[pltpu.VMEM((N_EXPERTS, 1), F32)],
        compiler_params=_params("arbitrary"),
        name="router",
    )(x, w_hi_t, w_lo_t, bias.reshape(N_EXPERTS, 1).astype(F32))


def _expert_kernel(blk_exp_ref, n_used_ref, x_ref, w1_ref, b1_ref, w2_ref, b2_ref, o_ref):
    blk = pl.program_id(0)

    @pl.when(blk < n_used_ref[0])
    def _():
        half = D_MODEL // 2
        x_lo, x_hi = _unpack_bf16_pairs(x_ref[...])
        h = (jnp.dot(x_lo, w1_ref[:half, :], preferred_element_type=F32)
             + jnp.dot(x_hi, w1_ref[half:, :], preferred_element_type=F32)) + b1_ref[...]
        g = jnp.minimum(h[:, :D_FF], SWIGLU_LIMIT)
        u = jnp.clip(h[:, D_FF:], -SWIGLU_LIMIT, SWIGLU_LIMIT)
        act = g * jax.nn.sigmoid(SWIGLU_ALPHA * g) * (u + 1.0)
        o_ref[...] = jnp.dot(act.astype(BF16), w2_ref[...], preferred_element_type=F32) + b2_ref[...]

    @pl.when(blk >= n_used_ref[0])
    def _():
        o_ref[...] = jnp.zeros_like(o_ref)


def _experts(xs, blk_exp, n_used, layer, w1, b1, w2, b2):
    n_rows = xs.shape[0]
    d = D_MODEL
    bm = EXPERT_ROWS
    f2 = w1.shape[3]
    wmap = lambda i, be, nu: (layer, be[i], 0, 0)
    return pl.pallas_call(
        _expert_kernel,
        grid_spec=pltpu.PrefetchScalarGridSpec(
            num_scalar_prefetch=2,
            grid=(n_rows // bm,),
            in_specs=[pl.BlockSpec((bm, d // 2), lambda i, be, nu: (i, 0)),
                      pl.BlockSpec((None, None, d, f2), wmap),
                      pl.BlockSpec((None, None, 1, f2), wmap),
                      pl.BlockSpec((None, None, f2 // 2, d), wmap),
                      pl.BlockSpec((None, None, 1, d), wmap)],
            out_specs=pl.BlockSpec((bm, d), lambda i, be, nu: (i, 0)),
        ),
        out_shape=jax.ShapeDtypeStruct((n_rows, d), F32),
        compiler_params=_params("arbitrary"),
        name="experts",
    )(blk_exp, n_used, xs, w1, b1, w2, b2)


def _combine_ln_kernel(y_ref, gate_ref, x_ref, g_ref, b_ref, o_ref):
    gate = gate_ref[...]
    y = functools.reduce(jnp.add, [gate[:, k:k + 1] * y_ref[k] for k in range(TOP_K)])
    o_ref[...] = _layer_norm_rows(DN_ALPHA * x_ref[...] + y, g_ref[...], b_ref[...])


def _combine_ln(yg, gate, x, g, b, row0=0, n_rows=None):
    n, d = x.shape
    n_rows = n if n_rows is None else n_rows
    tm = TOKEN_TILE
    assert row0 % tm == 0 and n_rows % tm == 0
    blk0 = row0 // tm
    row = lambda i: (blk0 + i, 0)
    fixed = lambda i: (0, 0)
    return pl.pallas_call(
        _combine_ln_kernel,
        grid=(n_rows // tm,),
        in_specs=[pl.BlockSpec((TOP_K, tm, d), lambda i: (0, blk0 + i, 0)), pl.BlockSpec((tm, TOP_K), row),
                  pl.BlockSpec((tm, d), row), pl.BlockSpec((1, d), fixed), pl.BlockSpec((1, d), fixed)],
        out_specs=pl.BlockSpec((tm, d), lambda i: (i, 0)),
        out_shape=jax.ShapeDtypeStruct((n_rows, d), F32),
        compiler_params=_params("parallel"),
        name="combine_ln",
    )(yg, gate, x, g.reshape(1, d), b.reshape(1, d))


def _sc_mesh():
    return plsc.VectorSubcoreMesh(core_axis_name="core", subcore_axis_name="subcore")


def _sc_scatter_rows(x, dest, n_out):
    n, d = x.shape
    mesh = _sc_mesh()
    per_worker = n // (mesh.num_cores * mesh.num_subcores)
    n_sub = SC_INDEX_WINDOW // SC_ROW_WINDOW
    assert per_worker % SC_INDEX_WINDOW == 0 and n_sub >= 2

    @functools.partial(
        pl.kernel, out_type=jax.ShapeDtypeStruct((n_out, d), x.dtype), mesh=mesh,
        scratch_types=[pltpu.VMEM((TOP_K, SC_INDEX_WINDOW), I32), pltpu.VMEM((2, SC_ROW_WINDOW, d), x.dtype),
                       pltpu.SemaphoreType.DMA((2,)), pltpu.SemaphoreType.DMA((2,))])
    def scatter(x_hbm, i_hbm, o_hbm, idx_v, buf, sem_r, sem_s):
        wid = lax.axis_index("core") * mesh.num_subcores + lax.axis_index("subcore")

        @pl.loop(0, per_worker // SC_INDEX_WINDOW)
        def _(it):
            base = wid * per_worker + it * SC_INDEX_WINDOW
            for k in range(TOP_K):
                pltpu.sync_copy(i_hbm.at[k, pl.ds(base, SC_INDEX_WINDOW)], idx_v.at[k])

            def read(j):
                rows = pl.ds(base + j * SC_ROW_WINDOW, SC_ROW_WINDOW)
                return pltpu.make_async_copy(x_hbm.at[rows], buf.at[j % 2], sem_r.at[j % 2])

            def send(j, k):
                rows = idx_v.at[k, pl.ds(j * SC_ROW_WINDOW, SC_ROW_WINDOW)]
                return pltpu.make_async_copy(buf.at[j % 2], o_hbm.at[rows], sem_s.at[j % 2])

            read(0).start()
            for j in range(n_sub):
                read(j).wait()
                for k in range(TOP_K):
                    send(j, k).start()
                if j + 1 < n_sub:
                    if j >= 1:
                        for k in range(TOP_K):
                            send(j - 1, k).wait()
                    read(j + 1).start()
            for j in (n_sub - 2, n_sub - 1):
                for k in range(TOP_K):
                    send(j, k).wait()

    return scatter(x, dest)


def _sc_gather_rows(table, indices):
    num = indices.shape[0]
    d = table.shape[1]
    mesh = _sc_mesh()
    per_worker = num // (mesh.num_cores * mesh.num_subcores)
    n_sub = SC_INDEX_WINDOW // SC_ROW_WINDOW
    assert per_worker % SC_INDEX_WINDOW == 0 and n_sub >= 2

    @functools.partial(
        pl.kernel, out_type=jax.ShapeDtypeStruct((num, d), table.dtype), mesh=mesh,
        scratch_types=[pltpu.VMEM((SC_INDEX_WINDOW,), I32), pltpu.VMEM((2, SC_ROW_WINDOW, d), table.dtype),
                       pltpu.SemaphoreType.DMA((2,)), pltpu.SemaphoreType.DMA((2,))])
    def gather(x_hbm, i_hbm, o_hbm, idx_v, buf, sem_g, sem_w):
        wid = lax.axis_index("core") * mesh.num_subcores + lax.axis_index("subcore")

        @pl.loop(0, per_worker // SC_INDEX_WINDOW)
        def _(it):
            base = wid * per_worker + it * SC_INDEX_WINDOW
            pltpu.sync_copy(i_hbm.at[pl.ds(base, SC_INDEX_WINDOW)], idx_v)

            def fetch(j):
                rows = idx_v.at[pl.ds(j * SC_ROW_WINDOW, SC_ROW_WINDOW)]
                return pltpu.make_async_copy(x_hbm.at[rows], buf.at[j % 2], sem_g.at[j % 2])

            def write(j):
                rows = pl.ds(base + j * SC_ROW_WINDOW, SC_ROW_WINDOW)
                return pltpu.make_async_copy(buf.at[j % 2], o_hbm.at[rows], sem_w.at[j % 2])

            fetch(0).start()
            for j in range(n_sub):
                fetch(j).wait()
                write(j).start()
                if j + 1 < n_sub:
                    if j >= 1:
                        write(j - 1).wait()
                    fetch(j + 1).start()
            write(n_sub - 2).wait()
            write(n_sub - 1).wait()

    return gather(table, indices)


def _moe(x, x_packed, w_r, b_r, layer, w1, b1, w2, b2):
    n, d = x.shape
    bm = EXPERT_ROWS
    wr_t = w_r.T
    wr_hi = wr_t.astype(BF16)
    wr_lo = (wr_t - wr_hi.astype(F32)).astype(BF16)
    idx, gate, rank, counts = _router(x, wr_hi, wr_lo, b_r)
    counts = counts[:, 0]
    padded = (counts + bm - 1) // bm * bm
    pad_ends = jnp.cumsum(padded)
    pad_starts = pad_ends - padded
    hot = idx[:, None, :] == jnp.arange(N_EXPERTS, dtype=I32)[None, :, None]
    dest = jnp.sum(jnp.where(hot, pad_starts[None, :, None], 0), axis=1) + rank
    n_blocks = n * TOP_K // bm + N_EXPERTS
    blk_start = jnp.arange(n_blocks, dtype=I32) * bm
    blk_exp = jnp.minimum(jnp.sum(blk_start[:, None] >= pad_ends[None, :], axis=1), N_EXPERTS - 1).astype(I32)
    n_used = (pad_ends[-1:] // bm).astype(I32)
    xs = _sc_scatter_rows(x_packed, dest, n_blocks * bm)
    ys = _experts(xs, blk_exp, n_used, layer, w1, b1, w2, b2)
    yg = _sc_gather_rows(ys, dest.reshape(-1)).reshape(TOP_K, n, d)
    return yg, gate.T


def _positions(groups):
    return jnp.concatenate([jnp.tile(jnp.arange(s), b) for b, s in groups])


def _axial_tables(groups):
    t = _positions(groups)
    n = HEAD_DIM // 4
    inv = AXIAL_THETA ** (-jnp.arange(n, dtype=F32) / n)
    ar = (t // GRID_W).astype(F32)[:, None] * inv
    ac = (t % GRID_W).astype(F32)[:, None] * inv
    z = jnp.zeros_like(ar)
    cr, sr, cc, sc = jnp.cos(ar), jnp.sin(ar), jnp.cos(ac), jnp.sin(ac)
    c = jnp.concatenate([cr, cr, cc, cc], axis=-1)
    s1 = jnp.concatenate([z, sr, z, sc], axis=-1)
    s2 = jnp.concatenate([-sr, z, -sc, z], axis=-1)
    return tuple(jnp.tile(a, (1, 2)) for a in (c, s1, s2))


def _rope_tables(groups):
    t = _positions(groups)
    n = ROPE_DIMS // 2
    inv = ROPE_THETA ** (-jnp.arange(n, dtype=F32) / n)
    ang = t.astype(F32)[:, None] * inv
    c, s = jnp.cos(ang), jnp.sin(ang)
    z = jnp.zeros_like(c)
    rest = HEAD_DIM - ROPE_DIMS
    pad1 = jnp.ones((t.shape[0], rest), F32)
    pad0 = jnp.zeros((t.shape[0], rest), F32)
    cc = jnp.concatenate([c, c, pad1], axis=-1)
    s1 = jnp.concatenate([z, s, pad0], axis=-1)
    s2 = jnp.concatenate([-s, z, pad0], axis=-1)
    return tuple(jnp.tile(a, (1, 2)) for a in (cc, s1, s2))


def _gqa_head_order():
    g = GQA_Q_HEADS // GQA_KV_HEADS
    return [h for j in range(g) for h in (j, g + j)]


def _mixer_even(x, groups, tabs, w_in, rpb, q_gain, k_gain, w_out):
    hd = HEAD_DIM
    na_w = NA_HEADS * hd
    order = _gqa_head_order()
    q0 = 3 * na_w
    q_cols = np.concatenate([q0 + h * hd + np.arange(hd) for h in order])
    w_in_p = jnp.concatenate([w_in[:, :q0], w_in[:, q_cols], w_in[:, q0 + GQA_Q_HEADS * hd:]], axis=1).astype(BF16)
    out_rows = np.concatenate([na_w + h * hd + np.arange(hd) for h in order])
    w_out_p = jnp.concatenate([w_out[:na_w], w_out[out_rows]], axis=0).astype(BF16)
    n_na = na_w // LANES
    n_q = GQA_Q_HEADS * hd // LANES
    modes = ([("plain", Q_SCALE, 0)] * n_na + [("plain", 1.0, 0)] * (2 * n_na)
             + [("norm_rope", Q_SCALE, 0)] * n_q + [("norm_rope", 1.0, 1)] + [("plain", 1.0, 0)])
    gains = jnp.stack([jnp.tile(q_gain.astype(F32), 2), jnp.tile(k_gain.astype(F32), 2)])
    proj = _project(x, w_in_p, tabs, gains, modes, HEAD_DIM // 4)
    cc = _na_bias_table(rpb)
    ya, yb = [], []
    row0 = 0
    for b, s in groups:
        ya.append(_na_attention(proj, cc, row0, b, s))
        yb.append(_gqa_attention(proj, row0, b, s))
        row0 += b * s
    attn = jnp.concatenate([jnp.concatenate(ya, axis=0), jnp.concatenate(yb, axis=0)], axis=1)
    return attn, w_out_p


def _mixer_odd(x, groups, tabs, w_in, w_out):
    n_slab = DIL_HEADS * HEAD_DIM // LANES
    modes = [("rope", Q_SCALE, 0)] * n_slab + [("rope", 1.0, 0)] * n_slab + [("plain", 1.0, 0)] * n_slab
    gains = jnp.ones((1, LANES), F32)
    qkv = _project(x, w_in.astype(BF16), tabs, gains, modes, ROPE_DIMS // 2, out_dtype=F32)
    parts, row0 = [], 0
    for b, s in groups:
        parts.append(_dilated_attention(qkv, row0, b, s))
        row0 += b * s
    return jnp.concatenate(parts, axis=0), w_out.astype(BF16)


def _trunk(xs, w_in_even, rpb_a, q_gain_b, k_gain_b, w_out_even, w_in_odd, w_out_odd,
           ln1_g, ln1_b, ln2_g, ln2_b, router_w, router_b, moe_w1, moe_b1, moe_w2, moe_b2):
    groups = [(x.shape[0], x.shape[1]) for x in xs]
    x = jnp.concatenate([t.reshape(-1, D_MODEL) for t in xs], axis=0).astype(F32)
    tabs_even = _axial_tables(groups)
    tabs_odd = _rope_tables(groups)
    b1 = moe_b1.astype(F32)[:, :, None, :]
    b2 = moe_b2.astype(F32)[:, :, None, :]
    for l in range(DEPTH):
        i = l // 2
        if l % 2 == 0:
            attn, w_out = _mixer_even(x, groups, tabs_even, w_in_even[i], rpb_a[i], q_gain_b[i], k_gain_b[i],
                                      w_out_even[i])
        else:
            attn, w_out = _mixer_odd(x, groups, tabs_odd, w_in_odd[i], w_out_odd[i])
        x, x_packed = _outproj_ln(attn, w_out, x, ln1_g[l], ln1_b[l])
        yg, gate = _moe(x, x_packed, router_w[l], router_b[l], l, moe_w1, b1, moe_w2, b2)
        if l + 1 < DEPTH:
            x = _combine_ln(yg, gate, x, ln2_g[l], ln2_b[l])
    outs, row0 = [], 0
    for t in xs:
        n = t.shape[0] * t.shape[1]
        outs.append(_combine_ln(yg, gate, x, ln2_g[-1], ln2_b[-1], row0, n).reshape(t.shape))
        row0 += n
    return tuple(outs)


def kernel(x_prompt, x_sample, w_in_even, rpb_a, q_gain_b, k_gain_b, w_out_even, w_in_odd, w_out_odd, ln1_g, ln1_b,
           ln2_g, ln2_b, router_w, router_b, moe_w1, moe_b1, moe_w2, moe_b2):
    params = (w_in_even, rpb_a, q_gain_b, k_gain_b, w_out_even, w_in_odd, w_out_odd,
              ln1_g, ln1_b, ln2_g, ln2_b, router_w, router_b, moe_w1.astype(BF16), moe_b1, moe_w2.astype(BF16), moe_b2)
    (y_prompt,) = _trunk((x_prompt,), *params)
    (y_sample,) = _trunk((x_sample,), *params)
    return y_prompt, y_sample
```

```python
import functools

import jax
import jax.numpy as jnp
import numpy as np
from jax import lax
from jax.experimental import pallas as pl
from jax.experimental.pallas import tpu as pltpu
from jax.experimental.pallas import tpu_sc as plsc

F32 = jnp.float32
BF16 = jnp.bfloat16
I32 = jnp.int32

D_MODEL = 1024
DEPTH = 4
HEAD_DIM = 64
GRID_W = 64
NA_HEADS = 8
NA_ROWS = 8
NA_COLS = 16
GQA_Q_HEADS = 8
GQA_KV_HEADS = 2
AXIAL_THETA = 10000.0
QK_NORM_EPS = 1e-6
DIL_HEADS = 16
DIL_BRANCHES = ((128, 1), (512, 4), (2048, 16))
ROPE_THETA = 500000.0
ROPE_DIMS = HEAD_DIM // 4
N_EXPERTS = 32
TOP_K = 4
D_FF = D_MODEL
SWIGLU_LIMIT = 7.0
SWIGLU_ALPHA = 1.702
DN_ALPHA = (2 * DEPTH) ** 0.25
LN_EPS = 1e-5
LOG2E = 1.4426950408889634
Q_SCALE = HEAD_DIM ** -0.5 * LOG2E

LANES = 128
NEG = -1e30
VMEM_LIMIT = 56 * 1024 * 1024
TOKEN_TILE = 512
EXPERT_ROWS = 256
NA_ROW_BLOCK = 8
NA_ROW_GROUP = 4
GQA_Q_TILE = 512
GQA_K_TILE = 512
GQA_DEN_ROWS = 16
WIN_Q_TILE = 128
WIN_RADIUS = 64
DIL_GROUP = 4
SC_INDEX_WINDOW = 128
SC_ROW_WINDOW = 64

EVEN_IN = 3 * NA_HEADS * HEAD_DIM + GQA_Q_HEADS * HEAD_DIM + 2 * GQA_KV_HEADS * HEAD_DIM


def _params(*sem):
    return pltpu.CompilerParams(dimension_semantics=sem, vmem_limit_bytes=VMEM_LIMIT)


def _lane_is_low():
    return lax.broadcasted_iota(I32, (1, LANES), 1) < HEAD_DIM


def _proj_kernel(x_ref, w_ref, c_ref, s1_ref, s2_ref, gain_ref, o_ref, *, slab_modes, shift):
    x = x_ref[...].astype(BF16)
    n_out = o_ref.shape[1]
    chunk = 2 * LANES
    if any(m[0] == "norm_rope" for m in slab_modes):
        r = lax.broadcasted_iota(I32, (LANES, LANES), 0) // HEAD_DIM
        c = lax.broadcasted_iota(I32, (LANES, LANES), 1) // HEAD_DIM
        head_mean = jnp.where(r == c, 1.0 / HEAD_DIM, 0.0).astype(BF16)
    for c0 in range(0, n_out, chunk):
        acc = jnp.dot(x, w_ref[:, c0:c0 + chunk], preferred_element_type=F32)
        for s in range(chunk // LANES):
            slab = c0 // LANES + s
            mode, scale, gidx = slab_modes[slab]
            y = acc[:, s * LANES:(s + 1) * LANES]
            if mode == "norm_rope":
                sq = y * y
                hi = sq.astype(BF16)
                lo = (sq - hi.astype(F32)).astype(BF16)
                ms = (jnp.dot(hi, head_mean, preferred_element_type=F32)
                      + jnp.dot(lo, head_mean, preferred_element_type=F32))
                y = y * lax.rsqrt(ms + QK_NORM_EPS) * gain_ref[gidx:gidx + 1, :]
            if mode in ("rope", "norm_rope"):
                y = (y * c_ref[...] + pltpu.roll(y, LANES - shift, 1) * s2_ref[...]
                     + pltpu.roll(y, shift, 1) * s1_ref[...])
            if scale != 1.0:
                y = y * scale
            o_ref[:, slab * LANES:(slab + 1) * LANES] = y.astype(o_ref.dtype)


def _project(x, w, tabs, gains, slab_modes, shift, out_dtype=BF16):
    n, d = x.shape
    m = w.shape[1]
    tm = TOKEN_TILE
    tab_spec = pl.BlockSpec((tm, LANES), lambda i: (i, 0))
    return pl.pallas_call(
        functools.partial(_proj_kernel, slab_modes=tuple(slab_modes), shift=shift),
        grid=(n // tm,),
        in_specs=[pl.BlockSpec((tm, d), lambda i: (i, 0)),
                  pl.BlockSpec((d, m), lambda i: (0, 0)),
                  tab_spec, tab_spec, tab_spec,
                  pl.BlockSpec(gains.shape, lambda i: (0, 0))],
        out_specs=pl.BlockSpec((tm, m), lambda i: (i, 0)),
        out_shape=jax.ShapeDtypeStruct((n, m), out_dtype),
        compiler_params=_params("parallel"),
        name="in_proj",
    )(x, w, tabs[0], tabs[1], tabs[2], gains)


def _layer_norm_rows(z, g, b):
    mu = jnp.mean(z, axis=-1, keepdims=True)
    zc = z - mu
    var = jnp.mean(zc * zc, axis=-1, keepdims=True)
    return zc * lax.rsqrt(var + LN_EPS) * g + b


def _pack_bf16_pairs(v):
    half = v.shape[1] // 2
    bits = pltpu.bitcast(v.astype(BF16).astype(F32), jnp.uint32)
    return (bits[:, :half] >> 16) | bits[:, half:]


def _unpack_bf16_pairs(w):
    lo = pltpu.bitcast(w << 16, F32)
    hi = pltpu.bitcast(w & jnp.uint32(0xFFFF0000), F32)
    return lo, hi


def _outproj_ln_kernel(a_ref, w_ref, x_ref, g_ref, b_ref, o_ref, p_ref):
    y = jnp.dot(a_ref[...], w_ref[...], preferred_element_type=F32)
    out = _layer_norm_rows(DN_ALPHA * x_ref[...] + y, g_ref[...], b_ref[...])
    o_ref[...] = out
    p_ref[...] = _pack_bf16_pairs(out)


def _outproj_ln(a, w, x, g, b):
    n, d = x.shape
    tm = TOKEN_TILE
    row = lambda i: (i, 0)
    fixed = lambda i: (0, 0)
    return pl.pallas_call(
        _outproj_ln_kernel,
        grid=(n // tm,),
        in_specs=[pl.BlockSpec((tm, a.shape[1]), row), pl.BlockSpec(w.shape, fixed),
                  pl.BlockSpec((tm, d), row), pl.BlockSpec((1, d), fixed), pl.BlockSpec((1, d), fixed)],
        out_specs=[pl.BlockSpec((tm, d), row), pl.BlockSpec((tm, d // 2), row)],
        out_shape=[jax.ShapeDtypeStruct((n, d), F32), jax.ShapeDtypeStruct((n, d // 2), jnp.uint32)],
        compiler_params=_params("parallel"),
        name="out_proj_ln",
    )(a, w, x, g.reshape(1, d), b.reshape(1, d))


def _na_kernel(q_ref, k_ref, v_ref, cc_ref, o_ref, *, rows):
    i = pl.program_id(2)
    low = _lane_is_low()

    for g0 in range(0, NA_ROW_BLOCK, NA_ROW_GROUP):
        wins, scores = [], []
        for rr in range(g0, g0 + NA_ROW_GROUP):
            r = i * NA_ROW_BLOCK + rr
            rs = jnp.clip(r - NA_ROWS // 2, 0, rows - NA_ROWS)
            var = r - rs
            q = q_ref[rr * GRID_W:(rr + 1) * GRID_W, :]
            kstart = pl.multiple_of(rs * GRID_W, GRID_W)
            kwin = k_ref[pl.ds(kstart, NA_ROWS * GRID_W), :]
            wins.append(kstart)
            for half in range(2):
                qm = jnp.where(low if half == 0 else jnp.logical_not(low), q, jnp.zeros_like(q))
                s = lax.dot_general(qm, kwin, (((1,), (1,)), ((), ())), preferred_element_type=F32)
                scores.append(s + cc_ref[half, var])
        probs = []
        for s in scores:
            p = jnp.exp2(s - jnp.max(s, axis=-1, keepdims=True))
            probs.append((p.astype(BF16), jnp.sum(p, axis=-1, keepdims=True)))
        for u, rr in enumerate(range(g0, g0 + NA_ROW_GROUP)):
            vwin = v_ref[pl.ds(wins[u], NA_ROWS * GRID_W), :]
            outs = [jnp.dot(p, vwin, preferred_element_type=F32) / l for p, l in probs[2 * u:2 * u + 2]]
            o_ref[rr * GRID_W:(rr + 1) * GRID_W, :] = jnp.where(low, outs[0], outs[1]).astype(o_ref.dtype)


def _na_attention(proj, cc, row0, b, s):
    rows = s // GRID_W
    assert rows >= NA_ROWS and rows % NA_ROW_BLOCK == 0 and row0 % s == 0
    tq = NA_ROW_BLOCK * GRID_W
    n_slab = NA_HEADS * HEAD_DIM // LANES
    qb0, sb0 = row0 // tq, row0 // s
    return pl.pallas_call(
        functools.partial(_na_kernel, rows=rows),
        grid=(b, n_slab, s // tq),
        in_specs=[pl.BlockSpec((tq, LANES), lambda bi, j, i: (qb0 + bi * (s // tq) + i, j)),
                  pl.BlockSpec((s, LANES), lambda bi, j, i: (sb0 + bi, n_slab + j)),
                  pl.BlockSpec((s, LANES), lambda bi, j, i: (sb0 + bi, 2 * n_slab + j)),
                  pl.BlockSpec((2, NA_ROWS, GRID_W, NA_ROWS * GRID_W), lambda bi, j, i: (j, 0, 0, 0))],
        out_specs=pl.BlockSpec((tq, LANES), lambda bi, j, i: (bi * (s // tq) + i, j)),
        out_shape=jax.ShapeDtypeStruct((b * s, n_slab * LANES), BF16),
        compiler_params=_params("parallel", "parallel", "parallel"),
        name="na_attn",
    )(proj, proj, proj, cc)


def _na_bias_table(rpb):
    var = np.arange(NA_ROWS)[:, None]
    j = np.arange(NA_ROWS)[None, :]
    qc = np.arange(GRID_W)[:, None]
    kc = np.arange(GRID_W)[None, :]
    row_sel = ((j - var + NA_ROWS - 1)[..., None] == np.arange(2 * NA_ROWS - 1)).astype(np.float32)
    col_sel = (np.clip(kc - qc + NA_COLS - 1, 0, 2 * NA_COLS - 2)[..., None]
               == np.arange(2 * NA_COLS - 1)).astype(np.float32)
    win = np.clip(qc - NA_COLS // 2, 0, GRID_W - NA_COLS)
    valid = (kc >= win) & (kc < win + NA_COLS)
    t = jnp.einsum("hab,vja,qkb->hvqjk", rpb.astype(F32), row_sel, col_sel, precision=lax.Precision.HIGHEST)
    t = jnp.where(valid[None, None, :, None, :], t * LOG2E, NEG)
    return t.reshape(rpb.shape[0], NA_ROWS, GRID_W, NA_ROWS * GRID_W)


def _gqa_kernel(q_ref, k_ref, vt_ref, o_ref, acc_ref, st_ref, *, n_kt):
    assert n_kt % 2 == 0
    low = _lane_is_low()
    q = q_ref[...]
    tq = q.shape[0]
    zero = jnp.zeros_like(q)
    qs = jnp.concatenate([jnp.where(low, q, zero), jnp.where(low, zero, q)], axis=0)
    acc_ref[...] = jnp.zeros_like(acc_ref)

    def scores(kt, slot):
        start = pl.multiple_of(kt * GQA_K_TILE, GQA_K_TILE)
        kb = k_ref[pl.ds(start, GQA_K_TILE), :]
        st_ref[slot] = lax.dot_general(kb, qs, (((1,), (1,)), ((), ())), preferred_element_type=F32)

    def softmax_pv(kt, slot, m_old):
        st = st_ref[slot]
        m_new = jnp.maximum(m_old, jnp.max(st, axis=0, keepdims=True))
        alpha = jnp.exp2(m_old - m_new)
        pt = jnp.exp2(st - m_new).astype(BF16)
        acc_ref[...] = alpha * acc_ref[...] + jnp.dot(vt_ref[kt], pt, preferred_element_type=F32)
        return m_new

    def step(i, m):
        kt = 2 * i
        scores(kt + 1, 1)
        m = softmax_pv(kt, 0, m)
        scores(jnp.minimum(kt + 2, n_kt - 1), 0)
        return softmax_pv(kt + 1, 1, m)

    scores(0, 0)
    lax.fori_loop(0, n_kt // 2, step, jnp.full((1, 2 * tq), -jnp.inf, F32))
    out_t = acc_ref[:LANES, :] / acc_ref[LANES:LANES + 1, :]
    o_t = jnp.concatenate([out_t[:HEAD_DIM, :tq], out_t[HEAD_DIM:, tq:]], axis=0)
    o_ref[...] = o_t.T.astype(o_ref.dtype)


def _gqa_attention(proj, row0, b, s):
    tq, tk = GQA_Q_TILE, GQA_K_TILE
    assert s % tk == 0 and row0 % s == 0
    n_slab = GQA_Q_HEADS * HEAD_DIM // LANES
    q_col = 3 * NA_HEADS * HEAD_DIM // LANES
    k_col = q_col + n_slab
    v0 = (k_col + 1) * LANES
    qb0, sb0 = row0 // tq, row0 // s
    v_t = proj[row0:row0 + b * s, v0:v0 + LANES].reshape(b, s // tk, tk, LANES).transpose(0, 1, 3, 2)
    ones_rows = jnp.zeros((b, s // tk, GQA_DEN_ROWS, tk), BF16).at[:, :, 0, :].set(1.0)
    v_t = jnp.concatenate([v_t, ones_rows], axis=2)
    vt_rows = LANES + GQA_DEN_ROWS
    return pl.pallas_call(
        functools.partial(_gqa_kernel, n_kt=s // tk),
        grid=(b, n_slab, s // tq),
        in_specs=[pl.BlockSpec((tq, LANES), lambda bi, j, i: (qb0 + bi * (s // tq) + i, q_col + j)),
                  pl.BlockSpec((s, LANES), lambda bi, j, i: (sb0 + bi, k_col)),
                  pl.BlockSpec((None, s // tk, vt_rows, tk), lambda bi, j, i: (bi, 0, 0, 0))],
        out_specs=pl.BlockSpec((tq, LANES), lambda bi, j, i: (bi * (s // tq) + i, j)),
        out_shape=jax.ShapeDtypeStruct((b * s, n_slab * LANES), BF16),
        scratch_shapes=[pltpu.VMEM((vt_rows, 2 * tq), F32), pltpu.VMEM((2, tk, 2 * tq), F32)],
        compiler_params=_params("parallel", "parallel", "parallel"),
        name="gqa_attn",
    )(proj, proj, v_t)


def _dil_kernel(q_ref, k_ref, v_ref, o_ref, m_ref, l_ref, acc_ref, *, seq):
    tile = q_ref.shape[0]
    t0 = pl.program_id(2) * tile
    low = _lane_is_low()
    sub = WIN_Q_TILE
    n_sub = tile // sub

    for bi, (_, r) in enumerate(DIL_BRANCHES):
        length = seq // r
        win = min(sub + 2 * WIN_RADIUS, length)

        def sub_blocks(g, carry, r=r, length=length, win=win, first=(bi == 0)):
            rows, scores, stats = [], [], []
            for u in range(DIL_GROUP):
                n = g * DIL_GROUP + u
                c = n % r
                blk = n // r
                u0 = t0 // r + blk * sub
                ks = jnp.clip(u0 - WIN_RADIUS, 0, length - win)
                if r == 1:
                    q_rows = pl.ds(pl.multiple_of(n * sub, sub), sub)
                    k_rows = pl.ds(pl.multiple_of(ks, WIN_RADIUS), win)
                else:
                    q_rows = pl.ds(c + r * blk * sub, sub, stride=r)
                    k_rows = pl.ds(c + r * ks, win, stride=r)
                rows.append((q_rows, k_rows))
                q = q_ref[q_rows, :].astype(BF16)
                kwin = k_ref[k_rows, :].astype(BF16)
                qpos = u0 + lax.broadcasted_iota(I32, (sub, win), 0)
                kpos = ks + lax.broadcasted_iota(I32, (sub, win), 1)
                valid = jnp.abs(kpos - qpos) <= WIN_RADIUS
                for half in range(2):
                    qm = jnp.where(low if half == 0 else jnp.logical_not(low), q, jnp.zeros_like(q))
                    s = lax.dot_general(qm, kwin, (((1,), (1,)), ((), ())), preferred_element_type=F32)
                    scores.append(jnp.where(valid, s, NEG))
            for s in scores:
                m = jnp.max(s, axis=-1, keepdims=True)
                p = jnp.exp2(s - m)
                stats.append((m, jnp.sum(p, axis=-1, keepdims=True), p.astype(BF16)))
            for u, (q_rows, k_rows) in enumerate(rows):
                vwin = v_ref[k_rows, :].astype(BF16)
                (m0, l0, p0), (m1, l1, p1) = stats[2 * u:2 * u + 2]
                m_b = jnp.where(low, m0, m1)
                l_b = jnp.where(low, l0, l1)
                pv_b = jnp.where(low, jnp.dot(p0, vwin, preferred_element_type=F32),
                                 jnp.dot(p1, vwin, preferred_element_type=F32))
                if first:
                    m_ref[q_rows, :] = m_b
                    l_ref[q_rows, :] = l_b
                    acc_ref[q_rows, :] = pv_b
                else:
                    m_old = m_ref[q_rows, :]
                    m_new = jnp.maximum(m_old, m_b)
                    a_old = jnp.exp2(m_old - m_new)
                    a_b = jnp.exp2(m_b - m_new)
                    l_ref[q_rows, :] = a_old * l_ref[q_rows, :] + a_b * l_b
                    acc_ref[q_rows, :] = a_old * acc_ref[q_rows, :] + a_b * pv_b
                    m_ref[q_rows, :] = m_new
            return carry

        lax.fori_loop(0, n_sub // DIL_GROUP, sub_blocks, 0)

    o_ref[...] = (acc_ref[...] / l_ref[...]).astype(o_ref.dtype)


def _dilated_attention(qkv, row0, b, s):
    tile = 16 * WIN_Q_TILE
    assert s % tile == 0 and row0 % s == 0
    n_slab = DIL_HEADS * HEAD_DIM // LANES
    qb0, sb0 = row0 // tile, row0 // s
    state = pltpu.VMEM((tile, LANES), F32)
    return pl.pallas_call(
        functools.partial(_dil_kernel, seq=s),
        grid=(b, n_slab, s // tile),
        in_specs=[pl.BlockSpec((tile, LANES), lambda bi, j, i: (qb0 + bi * (s // tile) + i, j)),
                  pl.BlockSpec((s, LANES), lambda bi, j, i: (sb0 + bi, n_slab + j)),
                  pl.BlockSpec((s, LANES), lambda bi, j, i: (sb0 + bi, 2 * n_slab + j))],
        out_specs=pl.BlockSpec((tile, LANES), lambda bi, j, i: (bi * (s // tile) + i, j)),
        out_shape=jax.ShapeDtypeStruct((b * s, n_slab * LANES), BF16),
        scratch_shapes=[state, state, state],
        compiler_params=_params("parallel", "parallel", "parallel"),
        name="dil_attn",
    )(qkv, qkv, qkv)


def _router_kernel(x_ref, wh_ref, wl_ref, b_ref, idx_ref, gate_ref, rank_ref, cnt_ref, base_ref):
    step = pl.program_id(0)

    @pl.when(step == 0)
    def _():
        base_ref[...] = jnp.zeros_like(base_ref)

    x = x_ref[...]
    xh = x.astype(BF16)
    xl = (x - xh.astype(F32)).astype(BF16)
    nt = (((1,), (1,)), ((), ()))
    logits = (lax.dot_general(wh_ref[...], xh, nt, preferred_element_type=F32)
              + lax.dot_general(wl_ref[...], xh, nt, preferred_element_type=F32)
              + lax.dot_general(wh_ref[...], xl, nt, preferred_element_type=F32)) + b_ref[...]
    tm = x.shape[0]
    eid = lax.broadcasted_iota(I32, (N_EXPERTS, tm), 0)
    vals = logits
    top_v, top_i, hots = [], [], []
    for _ in range(TOP_K):
        m = jnp.max(vals, axis=0, keepdims=True)
        idx = jnp.min(jnp.where(vals == m, eid, N_EXPERTS), axis=0, keepdims=True)
        hot = eid == idx
        top_v.append(m)
        top_i.append(idx)
        hots.append(hot)
        vals = jnp.where(hot, -jnp.inf, vals)
    es = [jnp.exp(v - top_v[0]) for v in top_v]
    den = functools.reduce(jnp.add, es)
    chosen = functools.reduce(jnp.logical_or, hots)
    before = (lax.broadcasted_iota(I32, (tm, tm), 0) < lax.broadcasted_iota(I32, (tm, tm), 1))
    prefix = jnp.dot(chosen.astype(BF16), before.astype(BF16), preferred_element_type=F32) + base_ref[...]
    for k in range(TOP_K):
        idx_ref[k:k + 1, :] = top_i[k]
        gate_ref[k:k + 1, :] = es[k] / den
        rank_ref[k:k + 1, :] = jnp.sum(jnp.where(hots[k], prefix, 0.0), axis=0, keepdims=True).astype(I32)
    base_ref[...] = base_ref[...] + jnp.sum(chosen.astype(F32), axis=1, keepdims=True)
    cnt_ref[...] = base_ref[...].astype(I32)


def _router(x, w_hi_t, w_lo_t, bias):
    n, d = x.shape
    tm = TOKEN_TILE
    tok = pl.BlockSpec((TOP_K, tm), lambda i: (0, i))
    fixed = lambda i: (0, 0)
    return pl.pallas_call(
        _router_kernel,
        grid=(n // tm,),
        in_specs=[pl.BlockSpec((tm, d), lambda i: (i, 0)), pl.BlockSpec((N_EXPERTS, d), fixed),
                  pl.BlockSpec((N_EXPERTS, d), fixed), pl.BlockSpec((N_EXPERTS, 1), fixed)],
        out_specs=[tok, tok, tok, pl.BlockSpec((N_EXPERTS, 1), fixed)],
        out_shape=[jax.ShapeDtypeStruct((TOP_K, n), I32), jax.ShapeDtypeStruct((TOP_K, n), F32),
                   jax.ShapeDtypeStruct((TOP_K, n), I32), jax.ShapeDtypeStruct((N_EXPERTS, 1), I32)],
        scratch_shapes=[pltpu.VMEM((N_EXPERTS, 1), F32)],
        compiler_params=_params("arbitrary"),
        name="router",
    )(x, w_hi_t, w_lo_t, bias.reshape(N_EXPERTS, 1).astype(F32))


def _expert_kernel(blk_exp_ref, n_used_ref, x_ref, w1_ref, b1_ref, w2_ref, b2_ref, o_ref):
    blk = pl.program_id(0)

    @pl.when(blk < n_used_ref[0])
    def _():
        x_lo, x_hi = _unpack_bf16_pairs(x_ref[...])
        x = jnp.concatenate([x_lo.astype(BF16), x_hi.astype(BF16)], axis=1)
        h = jnp.dot(x, w1_ref[...], preferred_element_type=F32) + b1_ref[...]
        g = jnp.minimum(h[:, :D_FF], SWIGLU_LIMIT)
        u = jnp.clip(h[:, D_FF:], -SWIGLU_LIMIT, SWIGLU_LIMIT)
        act = g * jax.nn.sigmoid(SWIGLU_ALPHA * g) * (u + 1.0)
        y = jnp.dot(act.astype(BF16), w2_ref[...], preferred_element_type=F32) + b2_ref[...]
        o_ref[...] = _pack_bf16_pairs(y)

    @pl.when(blk >= n_used_ref[0])
    def _():
        o_ref[...] = jnp.zeros_like(o_ref)


def _experts(xs, blk_exp, n_used, layer, w1, b1, w2, b2):
    n_rows = xs.shape[0]
    d = D_MODEL
    bm = EXPERT_ROWS
    f2 = w1.shape[3]
    wmap = lambda i, be, nu: (layer, be[i], 0, 0)
    return pl.pallas_call(
        _expert_kernel,
        grid_spec=pltpu.PrefetchScalarGridSpec(
            num_scalar_prefetch=2,
            grid=(n_rows // bm,),
            in_specs=[pl.BlockSpec((bm, d // 2), lambda i, be, nu: (i, 0)),
                      pl.BlockSpec((None, None, d, f2), wmap),
                      pl.BlockSpec((None, None, 1, f2), wmap),
                      pl.BlockSpec((None, None, f2 // 2, d), wmap),
                      pl.BlockSpec((None, None, 1, d), wmap)],
            out_specs=pl.BlockSpec((bm, d // 2), lambda i, be, nu: (i, 0)),
        ),
        out_shape=jax.ShapeDtypeStruct((n_rows, d // 2), jnp.uint32),
        compiler_params=_params("arbitrary"),
        name="experts",
    )(blk_exp, n_used, xs, w1, b1, w2, b2)


def _combine_ln_kernel(y_ref, gate_ref, x_ref, g_ref, b_ref, o_ref):
    gate = gate_ref[...]
    halves = [_unpack_bf16_pairs(y_ref[k]) for k in range(TOP_K)]
    y = jnp.concatenate([functools.reduce(jnp.add, [gate[:, k:k + 1] * halves[k][h] for k in range(TOP_K)])
                         for h in range(2)], axis=1)
    o_ref[...] = _layer_norm_rows(DN_ALPHA * x_ref[...] + y, g_ref[...], b_ref[...])


def _combine_ln(yg, gate, x, g, b, row0=0, n_rows=None):
    n, d = x.shape
    n_rows = n if n_rows is None else n_rows
    tm = TOKEN_TILE
    assert row0 % tm == 0 and n_rows % tm == 0
    blk0 = row0 // tm
    row = lambda i: (blk0 + i, 0)
    fixed = lambda i: (0, 0)
    return pl.pallas_call(
        _combine_ln_kernel,
        grid=(n_rows // tm,),
        in_specs=[pl.BlockSpec((TOP_K, tm, d // 2), lambda i: (0, blk0 + i, 0)), pl.BlockSpec((tm, TOP_K), row),
                  pl.BlockSpec((tm, d), row), pl.BlockSpec((1, d), fixed), pl.BlockSpec((1, d), fixed)],
        out_specs=pl.BlockSpec((tm, d), lambda i: (i, 0)),
        out_shape=jax.ShapeDtypeStruct((n_rows, d), F32),
        compiler_params=_params("parallel"),
        name="combine_ln",
    )(yg, gate, x, g.reshape(1, d), b.reshape(1, d))


def _sc_mesh():
    return plsc.VectorSubcoreMesh(core_axis_name="core", subcore_axis_name="subcore")


def _sc_scatter_rows(x, dest, n_out):
    n, d = x.shape
    mesh = _sc_mesh()
    per_worker = n // (mesh.num_cores * mesh.num_subcores)
    n_sub = SC_INDEX_WINDOW // SC_ROW_WINDOW
    assert per_worker % SC_INDEX_WINDOW == 0 and n_sub >= 2

    @functools.partial(
        pl.kernel, out_type=jax.ShapeDtypeStruct((n_out, d), x.dtype), mesh=mesh,
        scratch_types=[pltpu.VMEM((TOP_K, SC_INDEX_WINDOW), I32), pltpu.VMEM((2, SC_ROW_WINDOW, d), x.dtype),
                       pltpu.SemaphoreType.DMA((2,)), pltpu.SemaphoreType.DMA((2,))])
    def scatter(x_hbm, i_hbm, o_hbm, idx_v, buf, sem_r, sem_s):
        wid = lax.axis_index("core") * mesh.num_subcores + lax.axis_index("subcore")

        @pl.loop(0, per_worker // SC_INDEX_WINDOW)
        def _(it):
            base = wid * per_worker + it * SC_INDEX_WINDOW
            for k in range(TOP_K):
                pltpu.sync_copy(i_hbm.at[k, pl.ds(base, SC_INDEX_WINDOW)], idx_v.at[k])

            def read(j):
                rows = pl.ds(base + j * SC_ROW_WINDOW, SC_ROW_WINDOW)
                return pltpu.make_async_copy(x_hbm.at[rows], buf.at[j % 2], sem_r.at[j % 2])

            def send(j, k):
                rows = idx_v.at[k, pl.ds(j * SC_ROW_WINDOW, SC_ROW_WINDOW)]
                return pltpu.make_async_copy(buf.at[j % 2], o_hbm.at[rows], sem_s.at[j % 2])

            read(0).start()
            for j in range(n_sub):
                read(j).wait()
                for k in range(TOP_K):
                    send(j, k).start()
                if j + 1 < n_sub:
                    if j >= 1:
                        for k in range(TOP_K):
                            send(j - 1, k).wait()
                    read(j + 1).start()
            for j in (n_sub - 2, n_sub - 1):
                for k in range(TOP_K):
                    send(j, k).wait()

    return scatter(x, dest)


def _sc_gather_rows(table, indices):
    num = indices.shape[0]
    d = table.shape[1]
    mesh = _sc_mesh()
    per_worker = num // (mesh.num_cores * mesh.num_subcores)
    n_sub = SC_INDEX_WINDOW // SC_ROW_WINDOW
    assert per_worker % SC_INDEX_WINDOW == 0 and n_sub >= 2

    @functools.partial(
        pl.kernel, out_type=jax.ShapeDtypeStruct((num, d), table.dtype), mesh=mesh,
        scratch_types=[pltpu.VMEM((SC_INDEX_WINDOW,), I32), pltpu.VMEM((2, SC_ROW_WINDOW, d), table.dtype),
                       pltpu.SemaphoreType.DMA((2,)), pltpu.SemaphoreType.DMA((2,))])
    def gather(x_hbm, i_hbm, o_hbm, idx_v, buf, sem_g, sem_w):
        wid = lax.axis_index("core") * mesh.num_subcores + lax.axis_index("subcore")

        @pl.loop(0, per_worker // SC_INDEX_WINDOW)
        def _(it):
            base = wid * per_worker + it * SC_INDEX_WINDOW
            pltpu.sync_copy(i_hbm.at[pl.ds(base, SC_INDEX_WINDOW)], idx_v)

            def fetch(j):
                rows = idx_v.at[pl.ds(j * SC_ROW_WINDOW, SC_ROW_WINDOW)]
                return pltpu.make_async_copy(x_hbm.at[rows], buf.at[j % 2], sem_g.at[j % 2])

            def write(j):
                rows = pl.ds(base + j * SC_ROW_WINDOW, SC_ROW_WINDOW)
                return pltpu.make_async_copy(buf.at[j % 2], o_hbm.at[rows], sem_w.at[j % 2])

            fetch(0).start()
            for j in range(n_sub):
                fetch(j).wait()
                write(j).start()
                if j + 1 < n_sub:
                    if j >= 1:
                        write(j - 1).wait()
                    fetch(j + 1).start()
            write(n_sub - 2).wait()
            write(n_sub - 1).wait()

    return gather(table, indices)


def _moe(x, x_packed, w_r, b_r, layer, w1, b1, w2, b2):
    n, d = x.shape
    bm = EXPERT_ROWS
    wr_t = w_r.T
    wr_hi = wr_t.astype(BF16)
    wr_lo = (wr_t - wr_hi.astype(F32)).astype(BF16)
    idx, gate, rank, counts = _router(x, wr_hi, wr_lo, b_r)
    counts = counts[:, 0]
    padded = (counts + bm - 1) // bm * bm
    pad_ends = jnp.cumsum(padded)
    pad_starts = pad_ends - padded
    hot = idx[:, None, :] == jnp.arange(N_EXPERTS, dtype=I32)[None, :, None]
    dest = jnp.sum(jnp.where(hot, pad_starts[None, :, None], 0), axis=1) + rank
    n_blocks = n * TOP_K // bm + N_EXPERTS
    blk_start = jnp.arange(n_blocks, dtype=I32) * bm
    blk_exp = jnp.minimum(jnp.sum(blk_start[:, None] >= pad_ends[None, :], axis=1), N_EXPERTS - 1).astype(I32)
    n_used = (pad_ends[-1:] // bm).astype(I32)
    xs = _sc_scatter_rows(x_packed, dest, n_blocks * bm)
    ys = _experts(xs, blk_exp, n_used, layer, w1, b1, w2, b2)
    yg = _sc_gather_rows(ys, dest.reshape(-1)).reshape(TOP_K, n, d // 2)
    return yg, gate.T


def _positions(groups):
    return jnp.concatenate([jnp.tile(jnp.arange(s), b) for b, s in groups])


def _axial_tables(groups):
    t = _positions(groups)
    n = HEAD_DIM // 4
    inv = AXIAL_THETA ** (-jnp.arange(n, dtype=F32) / n)
    ar = (t // GRID_W).astype(F32)[:, None] * inv
    ac = (t % GRID_W).astype(F32)[:, None] * inv
    z = jnp.zeros_like(ar)
    cr, sr, cc, sc = jnp.cos(ar), jnp.sin(ar), jnp.cos(ac), jnp.sin(ac)
    c = jnp.concatenate([cr, cr, cc, cc], axis=-1)
    s1 = jnp.concatenate([z, sr, z, sc], axis=-1)
    s2 = jnp.concatenate([-sr, z, -sc, z], axis=-1)
    return tuple(jnp.tile(a, (1, 2)) for a in (c, s1, s2))


def _rope_tables(groups):
    t = _positions(groups)
    n = ROPE_DIMS // 2
    inv = ROPE_THETA ** (-jnp.arange(n, dtype=F32) / n)
    ang = t.astype(F32)[:, None] * inv
    c, s = jnp.cos(ang), jnp.sin(ang)
    z = jnp.zeros_like(c)
    rest = HEAD_DIM - ROPE_DIMS
    pad1 = jnp.ones((t.shape[0], rest), F32)
    pad0 = jnp.zeros((t.shape[0], rest), F32)
    cc = jnp.concatenate([c, c, pad1], axis=-1)
    s1 = jnp.concatenate([z, s, pad0], axis=-1)
    s2 = jnp.concatenate([-s, z, pad0], axis=-1)
    return tuple(jnp.tile(a, (1, 2)) for a in (cc, s1, s2))


def _gqa_head_order():
    g = GQA_Q_HEADS // GQA_KV_HEADS
    return [h for j in range(g) for h in (j, g + j)]


def _mixer_even(x, groups, tabs, w_in, rpb, q_gain, k_gain, w_out):
    hd = HEAD_DIM
    na_w = NA_HEADS * hd
    order = _gqa_head_order()
    q0 = 3 * na_w
    q_cols = np.concatenate([q0 + h * hd + np.arange(hd) for h in order])
    w_in_p = jnp.concatenate([w_in[:, :q0], w_in[:, q_cols], w_in[:, q0 + GQA_Q_HEADS * hd:]], axis=1).astype(BF16)
    out_rows = np.concatenate([na_w + h * hd + np.arange(hd) for h in order])
    w_out_p = jnp.concatenate([w_out[:na_w], w_out[out_rows]], axis=0).astype(BF16)
    n_na = na_w // LANES
    n_q = GQA_Q_HEADS * hd // LANES
    modes = ([("plain", Q_SCALE, 0)] * n_na + [("plain", 1.0, 0)] * (2 * n_na)
             + [("norm_rope", Q_SCALE, 0)] * n_q + [("norm_rope", 1.0, 1)] + [("plain", 1.0, 0)])
    gains = jnp.stack([jnp.tile(q_gain.astype(F32), 2), jnp.tile(k_gain.astype(F32), 2)])
    proj = _project(x, w_in_p, tabs, gains, modes, HEAD_DIM // 4)
    cc = _na_bias_table(rpb)
    ya, yb = [], []
    row0 = 0
    for b, s in groups:
        ya.append(_na_attention(proj, cc, row0, b, s))
        yb.append(_gqa_attention(proj, row0, b, s))
        row0 += b * s
    attn = jnp.concatenate([jnp.concatenate(ya, axis=0), jnp.concatenate(yb, axis=0)], axis=1)
    return attn, w_out_p


def _mixer_odd(x, groups, tabs, w_in, w_out):
    n_slab = DIL_HEADS * HEAD_DIM // LANES
    modes = [("rope", Q_SCALE, 0)] * n_slab + [("rope", 1.0, 0)] * n_slab + [("plain", 1.0, 0)] * n_slab
    gains = jnp.ones((1, LANES), F32)
    qkv = _project(x, w_in.astype(BF16), tabs, gains, modes, ROPE_DIMS // 2, out_dtype=F32)
    parts, row0 = [], 0
    for b, s in groups:
        parts.append(_dilated_attention(qkv, row0, b, s))
        row0 += b * s
    return jnp.concatenate(parts, axis=0), w_out.astype(BF16)


def _trunk(xs, w_in_even, rpb_a, q_gain_b, k_gain_b, w_out_even, w_in_odd, w_out_odd,
           ln1_g, ln1_b, ln2_g, ln2_b, router_w, router_b, moe_w1, moe_b1, moe_w2, moe_b2):
    groups = [(x.shape[0], x.shape[1]) for x in xs]
    x = jnp.concatenate([t.reshape(-1, D_MODEL) for t in xs], axis=0).astype(F32)
    tabs_even = _axial_tables(groups)
    tabs_odd = _rope_tables(groups)
    b1 = moe_b1.astype(F32)[:, :, None, :]
    b2 = moe_b2.astype(F32)[:, :, None, :]
    for l in range(DEPTH):
        i = l // 2
        if l % 2 == 0:
            attn, w_out = _mixer_even(x, groups, tabs_even, w_in_even[i], rpb_a[i], q_gain_b[i], k_gain_b[i],
                                      w_out_even[i])
        else:
            attn, w_out = _mixer_odd(x, groups, tabs_odd, w_in_odd[i], w_out_odd[i])
        x, x_packed = _outproj_ln(attn, w_out, x, ln1_g[l], ln1_b[l])
        yg, gate = _moe(x, x_packed, router_w[l], router_b[l], l, moe_w1, b1, moe_w2, b2)
        if l + 1 < DEPTH:
            x = _combine_ln(yg, gate, x, ln2_g[l], ln2_b[l])
    outs, row0 = [], 0
    for t in xs:
        n = t.shape[0] * t.shape[1]
        outs.append(_combine_ln(yg, gate, x, ln2_g[-1], ln2_b[-1], row0, n).reshape(t.shape))
        row0 += n
    return tuple(outs)


def kernel(x_prompt, x_sample, w_in_even, rpb_a, q_gain_b, k_gain_b, w_out_even, w_in_odd, w_out_odd, ln1_g, ln1_b,
           ln2_g, ln2_b, router_w, router_b, moe_w1, moe_b1, moe_w2, moe_b2):
    params = (w_in_even, rpb_a, q_gain_b, k_gain_b, w_out_even, w_in_odd, w_out_odd,
              ln1_g, ln1_b, ln2_g, ln2_b, router_w, router_b, moe_w1.astype(BF16), moe_b1, moe_w2.astype(BF16), moe_b2)
    (y_prompt,) = _trunk((x_prompt,), *params)
    (y_sample,) = _trunk((x_sample,), *params)
    return y_prompt, y_sample
```

```python
import functools

import jax
import jax.numpy as jnp
import numpy as np
from jax import lax
from jax.experimental import pallas as pl
from jax.experimental.pallas import tpu as pltpu
from jax.experimental.pallas import tpu_sc as plsc

F32 = jnp.float32
BF16 = jnp.bfloat16
I32 = jnp.int32

D_MODEL = 1024
DEPTH = 4
HEAD_DIM = 64
GRID_W = 64
NA_HEADS = 8
NA_ROWS = 8
NA_COLS = 16
GQA_Q_HEADS = 8
GQA_KV_HEADS = 2
AXIAL_THETA = 10000.0
QK_NORM_EPS = 1e-6
DIL_HEADS = 16
DIL_BRANCHES = ((128, 1), (512, 4), (2048, 16))
ROPE_THETA = 500000.0
ROPE_DIMS = HEAD_DIM // 4
N_EXPERTS = 32
TOP_K = 4
D_FF = D_MODEL
SWIGLU_LIMIT = 7.0
SWIGLU_ALPHA = 1.702
DN_ALPHA = (2 * DEPTH) ** 0.25
LN_EPS = 1e-5
LOG2E = 1.4426950408889634
Q_SCALE = HEAD_DIM ** -0.5 * LOG2E

LANES = 128
NEG = -1e30
VMEM_LIMIT = 56 * 1024 * 1024
TOKEN_TILE = 512
COMBINE_TILE = 1024
EXPERT_ROWS = 256
NA_ROW_BLOCK = 8
NA_ROW_GROUP = 4
GQA_Q_TILE = 512
GQA_K_TILE = 512
GQA_DEN_ROWS = 16
WIN_Q_TILE = 128
WIN_RADIUS = 64
DIL_GROUP = 4
SC_INDEX_WINDOW = 128
SC_ROW_WINDOW = 64

EVEN_IN = 3 * NA_HEADS * HEAD_DIM + GQA_Q_HEADS * HEAD_DIM + 2 * GQA_KV_HEADS * HEAD_DIM


def _params(*sem):
    return pltpu.CompilerParams(dimension_semantics=sem, vmem_limit_bytes=VMEM_LIMIT)


def _lane_is_low():
    return lax.broadcasted_iota(I32, (1, LANES), 1) < HEAD_DIM


def _proj_kernel(x_ref, w_ref, c_ref, s1_ref, s2_ref, gain_ref, o_ref, *, slab_modes, shift):
    x = x_ref[...].astype(BF16)
    n_out = o_ref.shape[1]
    chunk = 2 * LANES
    if any(m[0] == "norm_rope" for m in slab_modes):
        r = lax.broadcasted_iota(I32, (LANES, LANES), 0) // HEAD_DIM
        c = lax.broadcasted_iota(I32, (LANES, LANES), 1) // HEAD_DIM
        head_mean = jnp.where(r == c, 1.0 / HEAD_DIM, 0.0).astype(BF16)
    for c0 in range(0, n_out, chunk):
        acc = jnp.dot(x, w_ref[:, c0:c0 + chunk], preferred_element_type=F32)
        for s in range(chunk // LANES):
            slab = c0 // LANES + s
            mode, scale, gidx = slab_modes[slab]
            y = acc[:, s * LANES:(s + 1) * LANES]
            if mode == "norm_rope":
                sq = y * y
                hi = sq.astype(BF16)
                lo = (sq - hi.astype(F32)).astype(BF16)
                ms = (jnp.dot(hi, head_mean, preferred_element_type=F32)
                      + jnp.dot(lo, head_mean, preferred_element_type=F32))
                y = y * lax.rsqrt(ms + QK_NORM_EPS) * gain_ref[gidx:gidx + 1, :]
            if mode in ("rope", "norm_rope"):
                y = (y * c_ref[...] + pltpu.roll(y, LANES - shift, 1) * s2_ref[...]
                     + pltpu.roll(y, shift, 1) * s1_ref[...])
            if scale != 1.0:
                y = y * scale
            o_ref[:, slab * LANES:(slab + 1) * LANES] = y.astype(o_ref.dtype)


def _project(x, w, tabs, gains, slab_modes, shift, out_dtype=BF16):
    n, d = x.shape
    m = w.shape[1]
    tm = TOKEN_TILE
    tab_spec = pl.BlockSpec((tm, LANES), lambda i: (i, 0))
    return pl.pallas_call(
        functools.partial(_proj_kernel, slab_modes=tuple(slab_modes), shift=shift),
        grid=(n // tm,),
        in_specs=[pl.BlockSpec((tm, d), lambda i: (i, 0)),
                  pl.BlockSpec((d, m), lambda i: (0, 0)),
                  tab_spec, tab_spec, tab_spec,
                  pl.BlockSpec(gains.shape, lambda i: (0, 0))],
        out_specs=pl.BlockSpec((tm, m), lambda i: (i, 0)),
        out_shape=jax.ShapeDtypeStruct((n, m), out_dtype),
        compiler_params=_params("parallel"),
        name="in_proj",
    )(x, w, tabs[0], tabs[1], tabs[2], gains)


def _layer_norm_rows(z, g, b):
    mu = jnp.mean(z, axis=-1, keepdims=True)
    zc = z - mu
    var = jnp.mean(zc * zc, axis=-1, keepdims=True)
    return zc * lax.rsqrt(var + LN_EPS) * g + b


def _pack_bf16_pairs(v):
    half = v.shape[1] // 2
    bits = pltpu.bitcast(v.astype(BF16).astype(F32), jnp.uint32)
    return (bits[:, :half] >> 16) | bits[:, half:]


def _unpack_bf16_pairs(w):
    lo = pltpu.bitcast(w << 16, F32)
    hi = pltpu.bitcast(w & jnp.uint32(0xFFFF0000), F32)
    return lo, hi


def _route_tile(x, wh_ref, wl_ref, rb_ref, base_ref, idx_ref, gate_ref, rank_ref, cnt_ref):
    xh = x.astype(BF16)
    xl = (x - xh.astype(F32)).astype(BF16)
    nt = (((1,), (1,)), ((), ()))
    logits = (lax.dot_general(wh_ref[...], xh, nt, preferred_element_type=F32)
              + lax.dot_general(wl_ref[...], xh, nt, preferred_element_type=F32)
              + lax.dot_general(wh_ref[...], xl, nt, preferred_element_type=F32)) + rb_ref[...]
    tm = x.shape[0]
    eid = lax.broadcasted_iota(I32, (N_EXPERTS, tm), 0)
    vals = logits
    top_v, top_i, hots = [], [], []
    for _ in range(TOP_K):
        m = jnp.max(vals, axis=0, keepdims=True)
        idx = jnp.min(jnp.where(vals == m, eid, N_EXPERTS), axis=0, keepdims=True)
        hot = eid == idx
        top_v.append(m)
        top_i.append(idx)
        hots.append(hot)
        vals = jnp.where(hot, -jnp.inf, vals)
    es = [jnp.exp(v - top_v[0]) for v in top_v]
    den = functools.reduce(jnp.add, es)
    chosen = functools.reduce(jnp.logical_or, hots)
    before = (lax.broadcasted_iota(I32, (tm, tm), 0) < lax.broadcasted_iota(I32, (tm, tm), 1))
    prefix = jnp.dot(chosen.astype(BF16), before.astype(BF16), preferred_element_type=F32) + base_ref[...]
    for k in range(TOP_K):
        idx_ref[k:k + 1, :] = top_i[k]
        gate_ref[k:k + 1, :] = es[k] / den
        rank_ref[k:k + 1, :] = jnp.sum(jnp.where(hots[k], prefix, 0.0), axis=0, keepdims=True).astype(I32)
    base_ref[...] = base_ref[...] + jnp.sum(chosen.astype(F32), axis=1, keepdims=True)
    cnt_ref[...] = base_ref[...].astype(I32)


def _outproj_ln_kernel(*refs, n_parts):
    a_refs, w_refs = refs[:n_parts], refs[n_parts:2 * n_parts]
    x_ref, g_ref, b_ref, wh_ref, wl_ref, rb_ref = refs[2 * n_parts:2 * n_parts + 6]
    o_ref, p_ref, idx_ref, gate_ref, rank_ref, cnt_ref, base_ref = refs[2 * n_parts + 6:]

    @pl.when(pl.program_id(0) == 0)
    def _():
        base_ref[...] = jnp.zeros_like(base_ref)

    y = functools.reduce(jnp.add, [jnp.dot(a[...], w[...], preferred_element_type=F32)
                                   for a, w in zip(a_refs, w_refs)])
    out = _layer_norm_rows(DN_ALPHA * x_ref[...] + y, g_ref[...], b_ref[...])
    o_ref[...] = out
    p_ref[...] = _pack_bf16_pairs(out)
    _route_tile(out, wh_ref, wl_ref, rb_ref, base_ref, idx_ref, gate_ref, rank_ref, cnt_ref)


def _outproj_ln_route(parts, x, g, b, w_r, b_r):
    n, d = x.shape
    tm = TOKEN_TILE
    row = lambda i: (i, 0)
    fixed = lambda i: (0, 0)
    tok = pl.BlockSpec((TOP_K, tm), lambda i: (0, i))
    wr_t = w_r.T
    wr_hi = wr_t.astype(BF16)
    wr_lo = (wr_t - wr_hi.astype(F32)).astype(BF16)
    expert_vec = pl.BlockSpec((N_EXPERTS, 1), fixed)
    return pl.pallas_call(
        functools.partial(_outproj_ln_kernel, n_parts=len(parts)),
        grid=(n // tm,),
        in_specs=([pl.BlockSpec((tm, a.shape[1]), row) for a, _ in parts]
                  + [pl.BlockSpec(w.shape, fixed) for _, w in parts]
                  + [pl.BlockSpec((tm, d), row), pl.BlockSpec((1, d), fixed), pl.BlockSpec((1, d), fixed),
                     pl.BlockSpec((N_EXPERTS, d), fixed), pl.BlockSpec((N_EXPERTS, d), fixed), expert_vec]),
        out_specs=[pl.BlockSpec((tm, d), row), pl.BlockSpec((tm, d // 2), row), tok, tok, tok, expert_vec],
        out_shape=[jax.ShapeDtypeStruct((n, d), F32), jax.ShapeDtypeStruct((n, d // 2), jnp.uint32),
                   jax.ShapeDtypeStruct((TOP_K, n), I32), jax.ShapeDtypeStruct((TOP_K, n), F32),
                   jax.ShapeDtypeStruct((TOP_K, n), I32), jax.ShapeDtypeStruct((N_EXPERTS, 1), I32)],
        scratch_shapes=[pltpu.VMEM((N_EXPERTS, 1), F32)],
        compiler_params=_params("arbitrary"),
        name="out_proj_ln_route",
    )(*[a for a, _ in parts], *[w for _, w in parts], x, g.reshape(1, d), b.reshape(1, d), wr_hi, wr_lo,
      b_r.reshape(N_EXPERTS, 1).astype(F32))


def _na_kernel(q_ref, k_ref, v_ref, cc_ref, o_ref, *, rows):
    i = pl.program_id(2)
    low = _lane_is_low()

    for g0 in range(0, NA_ROW_BLOCK, NA_ROW_GROUP):
        wins, scores = [], []
        for rr in range(g0, g0 + NA_ROW_GROUP):
            r = i * NA_ROW_BLOCK + rr
            rs = jnp.clip(r - NA_ROWS // 2, 0, rows - NA_ROWS)
            var = r - rs
            q = q_ref[rr * GRID_W:(rr + 1) * GRID_W, :]
            kstart = pl.multiple_of(rs * GRID_W, GRID_W)
            kwin = k_ref[pl.ds(kstart, NA_ROWS * GRID_W), :]
            wins.append(kstart)
            for half in range(2):
                qm = jnp.where(low if half == 0 else jnp.logical_not(low), q, jnp.zeros_like(q))
                s = lax.dot_general(qm, kwin, (((1,), (1,)), ((), ())), preferred_element_type=F32)
                scores.append(s + cc_ref[half, var])
        probs = []
        for s in scores:
            p = jnp.exp2(s - jnp.max(s, axis=-1, keepdims=True))
            probs.append((p.astype(BF16), jnp.sum(p, axis=-1, keepdims=True)))
        for u, rr in enumerate(range(g0, g0 + NA_ROW_GROUP)):
            vwin = v_ref[pl.ds(wins[u], NA_ROWS * GRID_W), :]
            outs = [jnp.dot(p, vwin, preferred_element_type=F32) / l for p, l in probs[2 * u:2 * u + 2]]
            o_ref[rr * GRID_W:(rr + 1) * GRID_W, :] = jnp.where(low, outs[0], outs[1]).astype(o_ref.dtype)


def _na_attention(proj, cc, row0, b, s):
    rows = s // GRID_W
    assert rows >= NA_ROWS and rows % NA_ROW_BLOCK == 0 and row0 % s == 0
    tq = NA_ROW_BLOCK * GRID_W
    n_slab = NA_HEADS * HEAD_DIM // LANES
    qb0, sb0 = row0 // tq, row0 // s
    return pl.pallas_call(
        functools.partial(_na_kernel, rows=rows),
        grid=(b, n_slab, s // tq),
        in_specs=[pl.BlockSpec((tq, LANES), lambda bi, j, i: (qb0 + bi * (s // tq) + i, j)),
                  pl.BlockSpec((s, LANES), lambda bi, j, i: (sb0 + bi, n_slab + j)),
                  pl.BlockSpec((s, LANES), lambda bi, j, i: (sb0 + bi, 2 * n_slab + j)),
                  pl.BlockSpec((2, NA_ROWS, GRID_W, NA_ROWS * GRID_W), lambda bi, j, i: (j, 0, 0, 0))],
        out_specs=pl.BlockSpec((tq, LANES), lambda bi, j, i: (bi * (s // tq) + i, j)),
        out_shape=jax.ShapeDtypeStruct((b * s, n_slab * LANES), BF16),
        compiler_params=_params("parallel", "parallel", "parallel"),
        name="na_attn",
    )(proj, proj, proj, cc)


def _na_bias_table(rpb):
    var = np.arange(NA_ROWS)[:, None]
    j = np.arange(NA_ROWS)[None, :]
    qc = np.arange(GRID_W)[:, None]
    kc = np.arange(GRID_W)[None, :]
    row_sel = ((j - var + NA_ROWS - 1)[..., None] == np.arange(2 * NA_ROWS - 1)).astype(np.float32)
    col_sel = (np.clip(kc - qc + NA_COLS - 1, 0, 2 * NA_COLS - 2)[..., None]
               == np.arange(2 * NA_COLS - 1)).astype(np.float32)
    win = np.clip(qc - NA_COLS // 2, 0, GRID_W - NA_COLS)
    valid = (kc >= win) & (kc < win + NA_COLS)
    t = jnp.einsum("hab,vja,qkb->hvqjk", rpb.astype(F32), row_sel, col_sel, precision=lax.Precision.HIGHEST)
    t = jnp.where(valid[None, None, :, None, :], t * LOG2E, NEG)
    return t.reshape(rpb.shape[0], NA_ROWS, GRID_W, NA_ROWS * GRID_W)


def _gqa_kernel(q_ref, k_ref, vt_ref, o_ref, acc_ref, st_ref, *, n_kt):
    assert n_kt % 2 == 0
    low = _lane_is_low()
    q = q_ref[...]
    tq = q.shape[0]
    zero = jnp.zeros_like(q)
    qs = jnp.concatenate([jnp.where(low, q, zero), jnp.where(low, zero, q)], axis=0)
    acc_ref[...] = jnp.zeros_like(acc_ref)

    def scores(kt, slot):
        start = pl.multiple_of(kt * GQA_K_TILE, GQA_K_TILE)
        kb = k_ref[pl.ds(start, GQA_K_TILE), :]
        st_ref[slot] = lax.dot_general(kb, qs, (((1,), (1,)), ((), ())), preferred_element_type=F32)

    def softmax_pv(kt, slot, m_old):
        st = st_ref[slot]
        m_new = jnp.maximum(m_old, jnp.max(st, axis=0, keepdims=True))
        alpha = jnp.exp2(m_old - m_new)
        pt = jnp.exp2(st - m_new).astype(BF16)
        acc_ref[...] = alpha * acc_ref[...] + jnp.dot(vt_ref[kt], pt, preferred_element_type=F32)
        return m_new

    def step(i, m):
        kt = 2 * i
        scores(kt + 1, 1)
        m = softmax_pv(kt, 0, m)
        scores(jnp.minimum(kt + 2, n_kt - 1), 0)
        return softmax_pv(kt + 1, 1, m)

    scores(0, 0)
    lax.fori_loop(0, n_kt // 2, step, jnp.full((1, 2 * tq), -jnp.inf, F32))
    out_t = acc_ref[:LANES, :] / acc_ref[LANES:LANES + 1, :]
    o_t = jnp.concatenate([out_t[:HEAD_DIM, :tq], out_t[HEAD_DIM:, tq:]], axis=0)
    o_ref[...] = o_t.T.astype(o_ref.dtype)


def _gqa_attention(proj, row0, b, s):
    tq, tk = GQA_Q_TILE, GQA_K_TILE
    assert s % tk == 0 and row0 % s == 0
    n_slab = GQA_Q_HEADS * HEAD_DIM // LANES
    q_col = 3 * NA_HEADS * HEAD_DIM // LANES
    k_col = q_col + n_slab
    v0 = (k_col + 1) * LANES
    qb0, sb0 = row0 // tq, row0 // s
    v_t = proj[row0:row0 + b * s, v0:v0 + LANES].reshape(b, s // tk, tk, LANES).transpose(0, 1, 3, 2)
    ones_rows = jnp.zeros((b, s // tk, GQA_DEN_ROWS, tk), BF16).at[:, :, 0, :].set(1.0)
    v_t = jnp.concatenate([v_t, ones_rows], axis=2)
    vt_rows = LANES + GQA_DEN_ROWS
    return pl.pallas_call(
        functools.partial(_gqa_kernel, n_kt=s // tk),
        grid=(b, n_slab, s // tq),
        in_specs=[pl.BlockSpec((tq, LANES), lambda bi, j, i: (qb0 + bi * (s // tq) + i, q_col + j)),
                  pl.BlockSpec((s, LANES), lambda bi, j, i: (sb0 + bi, k_col)),
                  pl.BlockSpec((None, s // tk, vt_rows, tk), lambda bi, j, i: (bi, 0, 0, 0))],
        out_specs=pl.BlockSpec((tq, LANES), lambda bi, j, i: (bi * (s // tq) + i, j)),
        out_shape=jax.ShapeDtypeStruct((b * s, n_slab * LANES), BF16),
        scratch_shapes=[pltpu.VMEM((vt_rows, 2 * tq), F32), pltpu.VMEM((2, tk, 2 * tq), F32)],
        compiler_params=_params("parallel", "parallel", "parallel"),
        name="gqa_attn",
    )(proj, proj, v_t)


def _dil_kernel(q_ref, k_ref, v_ref, o_ref, m_ref, l_ref, acc_ref, *, seq):
    tile = q_ref.shape[0]
    t0 = pl.program_id(2) * tile
    low = _lane_is_low()
    sub = WIN_Q_TILE
    n_sub = tile // sub

    for bi, (_, r) in enumerate(DIL_BRANCHES):
        length = seq // r
        win = min(sub + 2 * WIN_RADIUS, length)

        def sub_blocks(g, carry, r=r, length=length, win=win, first=(bi == 0)):
            rows, scores, stats = [], [], []
            for u in range(DIL_GROUP):
                n = g * DIL_GROUP + u
                c = n % r
                blk = n // r
                u0 = t0 // r + blk * sub
                ks = jnp.clip(u0 - WIN_RADIUS, 0, length - win)
                if r == 1:
                    q_rows = pl.ds(pl.multiple_of(n * sub, sub), sub)
                    k_rows = pl.ds(pl.multiple_of(ks, WIN_RADIUS), win)
                else:
                    q_rows = pl.ds(c + r * blk * sub, sub, stride=r)
                    k_rows = pl.ds(c + r * ks, win, stride=r)
                rows.append((q_rows, k_rows))
                q = q_ref[q_rows, :].astype(BF16)
                kwin = k_ref[k_rows, :].astype(BF16)
                qpos = u0 + lax.broadcasted_iota(I32, (sub, win), 0)
                kpos = ks + lax.broadcasted_iota(I32, (sub, win), 1)
                valid = jnp.abs(kpos - qpos) <= WIN_RADIUS
                for half in range(2):
                    qm = jnp.where(low if half == 0 else jnp.logical_not(low), q, jnp.zeros_like(q))
                    s = lax.dot_general(qm, kwin, (((1,), (1,)), ((), ())), preferred_element_type=F32)
                    scores.append(jnp.where(valid, s, NEG))
            for s in scores:
                m = jnp.max(s, axis=-1, keepdims=True)
                p = jnp.exp2(s - m)
                stats.append((m, jnp.sum(p, axis=-1, keepdims=True), p.astype(BF16)))
            for u, (q_rows, k_rows) in enumerate(rows):
                vwin = v_ref[k_rows, :].astype(BF16)
                (m0, l0, p0), (m1, l1, p1) = stats[2 * u:2 * u + 2]
                m_b = jnp.where(low, m0, m1)
                l_b = jnp.where(low, l0, l1)
                pv_b = jnp.where(low, jnp.dot(p0, vwin, preferred_element_type=F32),
                                 jnp.dot(p1, vwin, preferred_element_type=F32))
                if first:
                    m_ref[q_rows, :] = m_b
                    l_ref[q_rows, :] = l_b
                    acc_ref[q_rows, :] = pv_b
                else:
                    m_old = m_ref[q_rows, :]
                    m_new = jnp.maximum(m_old, m_b)
                    a_old = jnp.exp2(m_old - m_new)
                    a_b = jnp.exp2(m_b - m_new)
                    l_ref[q_rows, :] = a_old * l_ref[q_rows, :] + a_b * l_b
                    acc_ref[q_rows, :] = a_old * acc_ref[q_rows, :] + a_b * pv_b
                    m_ref[q_rows, :] = m_new
            return carry

        lax.fori_loop(0, n_sub // DIL_GROUP, sub_blocks, 0)

    o_ref[...] = (acc_ref[...] / l_ref[...]).astype(o_ref.dtype)


def _dilated_attention(qkv, row0, b, s):
    tile = 16 * WIN_Q_TILE
    assert s % tile == 0 and row0 % s == 0
    n_slab = DIL_HEADS * HEAD_DIM // LANES
    qb0, sb0 = row0 // tile, row0 // s
    state = pltpu.VMEM((tile, LANES), F32)
    return pl.pallas_call(
        functools.partial(_dil_kernel, seq=s),
        grid=(b, n_slab, s // tile),
        in_specs=[pl.BlockSpec((tile, LANES), lambda bi, j, i: (qb0 + bi * (s // tile) + i, j)),
                  pl.BlockSpec((s, LANES), lambda bi, j, i: (sb0 + bi, n_slab + j)),
                  pl.BlockSpec((s, LANES), lambda bi, j, i: (sb0 + bi, 2 * n_slab + j))],
        out_specs=pl.BlockSpec((tile, LANES), lambda bi, j, i: (bi * (s // tile) + i, j)),
        out_shape=jax.ShapeDtypeStruct((b * s, n_slab * LANES), BF16),
        scratch_shapes=[state, state, state],
        compiler_params=_params("parallel", "parallel", "parallel"),
        name="dil_attn",
    )(qkv, qkv, qkv)


def _expert_kernel(blk_exp_ref, n_used_ref, x_ref, w1_ref, b1_ref, w2_ref, b2_ref, o_ref):
    blk = pl.program_id(0)

    @pl.when(blk < n_used_ref[0])
    def _():
        x_lo, x_hi = _unpack_bf16_pairs(x_ref[...])
        x = jnp.concatenate([x_lo.astype(BF16), x_hi.astype(BF16)], axis=1)
        h = jnp.dot(x, w1_ref[...], preferred_element_type=F32) + b1_ref[...]
        g = jnp.minimum(h[:, :D_FF], SWIGLU_LIMIT)
        u = jnp.clip(h[:, D_FF:], -SWIGLU_LIMIT, SWIGLU_LIMIT)
        act = g * jax.nn.sigmoid(SWIGLU_ALPHA * g) * (u + 1.0)
        y = jnp.dot(act.astype(BF16), w2_ref[...], preferred_element_type=F32) + b2_ref[...]
        o_ref[...] = _pack_bf16_pairs(y)

    @pl.when(blk >= n_used_ref[0])
    def _():
        o_ref[...] = jnp.zeros_like(o_ref)


def _experts(xs, blk_exp, n_used, layer, w1, b1, w2, b2):
    n_rows = xs.shape[0]
    d = D_MODEL
    bm = EXPERT_ROWS
    f2 = w1.shape[3]
    wmap = lambda i, be, nu: (layer, be[i], 0, 0)
    return pl.pallas_call(
        _expert_kernel,
        grid_spec=pltpu.PrefetchScalarGridSpec(
            num_scalar_prefetch=2,
            grid=(n_rows // bm,),
            in_specs=[pl.BlockSpec((bm, d // 2), lambda i, be, nu: (i, 0)),
                      pl.BlockSpec((None, None, d, f2), wmap),
                      pl.BlockSpec((None, None, 1, f2), wmap),
                      pl.BlockSpec((None, None, f2 // 2, d), wmap),
                      pl.BlockSpec((None, None, 1, d), wmap)],
            out_specs=pl.BlockSpec((bm, d // 2), lambda i, be, nu: (i, 0)),
        ),
        out_shape=jax.ShapeDtypeStruct((n_rows, d // 2), jnp.uint32),
        compiler_params=_params("arbitrary"),
        name="experts",
    )(blk_exp, n_used, xs, w1, b1, w2, b2)


def _combine_ln_kernel(y_ref, gate_ref, x_ref, g_ref, b_ref, o_ref):
    gate = gate_ref[...]
    halves = [_unpack_bf16_pairs(y_ref[k]) for k in range(TOP_K)]
    y = jnp.concatenate([functools.reduce(jnp.add, [gate[:, k:k + 1] * halves[k][h] for k in range(TOP_K)])
                         for h in range(2)], axis=1)
    o_ref[...] = _layer_norm_rows(DN_ALPHA * x_ref[...] + y, g_ref[...], b_ref[...])


def _combine_ln(yg, gate, x, g, b, row0=0, n_rows=None):
    n, d = x.shape
    n_rows = n if n_rows is None else n_rows
    tm = COMBINE_TILE
    assert row0 % tm == 0 and n_rows % tm == 0
    blk0 = row0 // tm
    row = lambda i: (blk0 + i, 0)
    fixed = lambda i: (0, 0)
    return pl.pallas_call(
        _combine_ln_kernel,
        grid=(n_rows // tm,),
        in_specs=[pl.BlockSpec((TOP_K, tm, d // 2), lambda i: (0, blk0 + i, 0)), pl.BlockSpec((tm, TOP_K), row),
                  pl.BlockSpec((tm, d), row), pl.BlockSpec((1, d), fixed), pl.BlockSpec((1, d), fixed)],
        out_specs=pl.BlockSpec((tm, d), lambda i: (i, 0)),
        out_shape=jax.ShapeDtypeStruct((n_rows, d), F32),
        compiler_params=_params("parallel"),
        name="combine_ln",
    )(yg, gate, x, g.reshape(1, d), b.reshape(1, d))


def _sc_mesh():
    return plsc.VectorSubcoreMesh(core_axis_name="core", subcore_axis_name="subcore")


def _sc_scatter_rows(x, dest, n_out):
    n, d = x.shape
    mesh = _sc_mesh()
    per_worker = n // (mesh.num_cores * mesh.num_subcores)
    n_sub = SC_INDEX_WINDOW // SC_ROW_WINDOW
    assert per_worker % SC_INDEX_WINDOW == 0 and n_sub >= 2

    @functools.partial(
        pl.kernel, out_type=jax.ShapeDtypeStruct((n_out, d), x.dtype), mesh=mesh,
        scratch_types=[pltpu.VMEM((TOP_K, SC_INDEX_WINDOW), I32), pltpu.VMEM((2, SC_ROW_WINDOW, d), x.dtype),
                       pltpu.SemaphoreType.DMA((2,)), pltpu.SemaphoreType.DMA((2,))])
    def scatter(x_hbm, i_hbm, o_hbm, idx_v, buf, sem_r, sem_s):
        wid = lax.axis_index("core") * mesh.num_subcores + lax.axis_index("subcore")

        @pl.loop(0, per_worker // SC_INDEX_WINDOW)
        def _(it):
            base = wid * per_worker + it * SC_INDEX_WINDOW
            for k in range(TOP_K):
                pltpu.sync_copy(i_hbm.at[k, pl.ds(base, SC_INDEX_WINDOW)], idx_v.at[k])

            def read(j):
                rows = pl.ds(base + j * SC_ROW_WINDOW, SC_ROW_WINDOW)
                return pltpu.make_async_copy(x_hbm.at[rows], buf.at[j % 2], sem_r.at[j % 2])

            def send(j, k):
                rows = idx_v.at[k, pl.ds(j * SC_ROW_WINDOW, SC_ROW_WINDOW)]
                return pltpu.make_async_copy(buf.at[j % 2], o_hbm.at[rows], sem_s.at[j % 2])

            read(0).start()
            for j in range(n_sub):
                read(j).wait()
                for k in range(TOP_K):
                    send(j, k).start()
                if j + 1 < n_sub:
                    if j >= 1:
                        for k in range(TOP_K):
                            send(j - 1, k).wait()
                    read(j + 1).start()
            for j in (n_sub - 2, n_sub - 1):
                for k in range(TOP_K):
                    send(j, k).wait()

    return scatter(x, dest)


def _sc_gather_rows(table, indices):
    num = indices.shape[0]
    d = table.shape[1]
    mesh = _sc_mesh()
    per_worker = num // (mesh.num_cores * mesh.num_subcores)
    n_sub = SC_INDEX_WINDOW // SC_ROW_WINDOW
    assert per_worker % SC_INDEX_WINDOW == 0 and n_sub >= 2

    @functools.partial(
        pl.kernel, out_type=jax.ShapeDtypeStruct((num, d), table.dtype), mesh=mesh,
        scratch_types=[pltpu.VMEM((SC_INDEX_WINDOW,), I32), pltpu.VMEM((2, SC_ROW_WINDOW, d), table.dtype),
                       pltpu.SemaphoreType.DMA((2,)), pltpu.SemaphoreType.DMA((2,))])
    def gather(x_hbm, i_hbm, o_hbm, idx_v, buf, sem_g, sem_w):
        wid = lax.axis_index("core") * mesh.num_subcores + lax.axis_index("subcore")

        @pl.loop(0, per_worker // SC_INDEX_WINDOW)
        def _(it):
            base = wid * per_worker + it * SC_INDEX_WINDOW
            pltpu.sync_copy(i_hbm.at[pl.ds(base, SC_INDEX_WINDOW)], idx_v)

            def fetch(j):
                rows = idx_v.at[pl.ds(j * SC_ROW_WINDOW, SC_ROW_WINDOW)]
                return pltpu.make_async_copy(x_hbm.at[rows], buf.at[j % 2], sem_g.at[j % 2])

            def write(j):
                rows = pl.ds(base + j * SC_ROW_WINDOW, SC_ROW_WINDOW)
                return pltpu.make_async_copy(buf.at[j % 2], o_hbm.at[rows], sem_w.at[j % 2])

            fetch(0).start()
            for j in range(n_sub):
                fetch(j).wait()
                write(j).start()
                if j + 1 < n_sub:
                    if j >= 1:
                        write(j - 1).wait()
                    fetch(j + 1).start()
            write(n_sub - 2).wait()
            write(n_sub - 1).wait()

    return gather(table, indices)


def _moe(x_packed, idx, gate, rank, counts, layer, w1, b1, w2, b2):
    n = x_packed.shape[0]
    d = D_MODEL
    bm = EXPERT_ROWS
    counts = counts[:, 0]
    padded = (counts + bm - 1) // bm * bm
    pad_ends = jnp.cumsum(padded)
    pad_starts = pad_ends - padded
    hot = idx[:, None, :] == jnp.arange(N_EXPERTS, dtype=I32)[None, :, None]
    dest = jnp.sum(jnp.where(hot, pad_starts[None, :, None], 0), axis=1) + rank
    n_blocks = n * TOP_K // bm + N_EXPERTS
    blk_start = jnp.arange(n_blocks, dtype=I32) * bm
    blk_exp = jnp.minimum(jnp.sum(blk_start[:, None] >= pad_ends[None, :], axis=1), N_EXPERTS - 1).astype(I32)
    n_used = (pad_ends[-1:] // bm).astype(I32)
    xs = _sc_scatter_rows(x_packed, dest, n_blocks * bm)
    ys = _experts(xs, blk_exp, n_used, layer, w1, b1, w2, b2)
    yg = _sc_gather_rows(ys, dest.reshape(-1)).reshape(TOP_K, n, d // 2)
    return yg, gate.T


def _positions(groups):
    return jnp.concatenate([jnp.tile(jnp.arange(s), b) for b, s in groups])


def _axial_tables(groups):
    t = _positions(groups)
    n = HEAD_DIM // 4
    inv = AXIAL_THETA ** (-jnp.arange(n, dtype=F32) / n)
    ar = (t // GRID_W).astype(F32)[:, None] * inv
    ac = (t % GRID_W).astype(F32)[:, None] * inv
    z = jnp.zeros_like(ar)
    cr, sr, cc, sc = jnp.cos(ar), jnp.sin(ar), jnp.cos(ac), jnp.sin(ac)
    c = jnp.concatenate([cr, cr, cc, cc], axis=-1)
    s1 = jnp.concatenate([z, sr, z, sc], axis=-1)
    s2 = jnp.concatenate([-sr, z, -sc, z], axis=-1)
    return tuple(jnp.tile(a, (1, 2)) for a in (c, s1, s2))


def _rope_tables(groups):
    t = _positions(groups)
    n = ROPE_DIMS // 2
    inv = ROPE_THETA ** (-jnp.arange(n, dtype=F32) / n)
    ang = t.astype(F32)[:, None] * inv
    c, s = jnp.cos(ang), jnp.sin(ang)
    z = jnp.zeros_like(c)
    rest = HEAD_DIM - ROPE_DIMS
    pad1 = jnp.ones((t.shape[0], rest), F32)
    pad0 = jnp.zeros((t.shape[0], rest), F32)
    cc = jnp.concatenate([c, c, pad1], axis=-1)
    s1 = jnp.concatenate([z, s, pad0], axis=-1)
    s2 = jnp.concatenate([-s, z, pad0], axis=-1)
    return tuple(jnp.tile(a, (1, 2)) for a in (cc, s1, s2))


def _gqa_head_order():
    g = GQA_Q_HEADS // GQA_KV_HEADS
    return [h for j in range(g) for h in (j, g + j)]


def _mixer_even(x, groups, tabs, w_in, rpb, q_gain, k_gain, w_out):
    hd = HEAD_DIM
    na_w = NA_HEADS * hd
    order = _gqa_head_order()
    q0 = 3 * na_w
    q_cols = np.concatenate([q0 + h * hd + np.arange(hd) for h in order])
    w_in_p = jnp.concatenate([w_in[:, :q0], w_in[:, q_cols], w_in[:, q0 + GQA_Q_HEADS * hd:]], axis=1).astype(BF16)
    out_rows = np.concatenate([na_w + h * hd + np.arange(hd) for h in order])
    w_out_p = jnp.concatenate([w_out[:na_w], w_out[out_rows]], axis=0).astype(BF16)
    n_na = na_w // LANES
    n_q = GQA_Q_HEADS * hd // LANES
    modes = ([("plain", Q_SCALE, 0)] * n_na + [("plain", 1.0, 0)] * (2 * n_na)
             + [("norm_rope", Q_SCALE, 0)] * n_q + [("norm_rope", 1.0, 1)] + [("plain", 1.0, 0)])
    gains = jnp.stack([jnp.tile(q_gain.astype(F32), 2), jnp.tile(k_gain.astype(F32), 2)])
    proj = _project(x, w_in_p, tabs, gains, modes, HEAD_DIM // 4)
    cc = _na_bias_table(rpb)
    ya, yb = [], []
    row0 = 0
    for b, s in groups:
        ya.append(_na_attention(proj, cc, row0, b, s))
        yb.append(_gqa_attention(proj, row0, b, s))
        row0 += b * s
    return [(jnp.concatenate(ya, axis=0), w_out_p[:na_w]), (jnp.concatenate(yb, axis=0), w_out_p[na_w:])]


def _mixer_odd(x, groups, tabs, w_in, w_out):
    n_slab = DIL_HEADS * HEAD_DIM // LANES
    modes = [("rope", Q_SCALE, 0)] * n_slab + [("rope", 1.0, 0)] * n_slab + [("plain", 1.0, 0)] * n_slab
    gains = jnp.ones((1, LANES), F32)
    qkv = _project(x, w_in.astype(BF16), tabs, gains, modes, ROPE_DIMS // 2, out_dtype=F32)
    parts, row0 = [], 0
    for b, s in groups:
        parts.append(_dilated_attention(qkv, row0, b, s))
        row0 += b * s
    return [(jnp.concatenate(parts, axis=0), w_out.astype(BF16))]


def _trunk(xs, w_in_even, rpb_a, q_gain_b, k_gain_b, w_out_even, w_in_odd, w_out_odd,
           ln1_g, ln1_b, ln2_g, ln2_b, router_w, router_b, moe_w1, moe_b1, moe_w2, moe_b2):
    groups = [(x.shape[0], x.shape[1]) for x in xs]
    x = jnp.concatenate([t.reshape(-1, D_MODEL) for t in xs], axis=0).astype(F32)
    tabs_even = _axial_tables(groups)
    tabs_odd = _rope_tables(groups)
    b1 = moe_b1.astype(F32)[:, :, None, :]
    b2 = moe_b2.astype(F32)[:, :, None, :]
    for l in range(DEPTH):
        i = l // 2
        if l % 2 == 0:
            parts = _mixer_even(x, groups, tabs_even, w_in_even[i], rpb_a[i], q_gain_b[i], k_gain_b[i],
                                w_out_even[i])
        else:
            parts = _mixer_odd(x, groups, tabs_odd, w_in_odd[i], w_out_odd[i])
        x, x_packed, idx, gate, rank, counts = _outproj_ln_route(parts, x, ln1_g[l], ln1_b[l], router_w[l],
                                                                 router_b[l])
        yg, gate = _moe(x_packed, idx, gate, rank, counts, l, moe_w1, b1, moe_w2, b2)
        if l + 1 < DEPTH:
            x = _combine_ln(yg, gate, x, ln2_g[l], ln2_b[l])
    outs, row0 = [], 0
    for t in xs:
        n = t.shape[0] * t.shape[1]
        outs.append(_combine_ln(yg, gate, x, ln2_g[-1], ln2_b[-1], row0, n).reshape(t.shape))
        row0 += n
    return tuple(outs)


def kernel(x_prompt, x_sample, w_in_even, rpb_a, q_gain_b, k_gain_b, w_out_even, w_in_odd, w_out_odd, ln1_g, ln1_b,
           ln2_g, ln2_b, router_w, router_b, moe_w1, moe_b1, moe_w2, moe_b2):
    params = (w_in_even, rpb_a, q_gain_b, k_gain_b, w_out_even, w_in_odd, w_out_odd,
              ln1_g, ln1_b, ln2_g, ln2_b, router_w, router_b, moe_w1.astype(BF16), moe_b1, moe_w2.astype(BF16), moe_b2)
    (y_prompt,) = _trunk((x_prompt,), *params)
    (y_sample,) = _trunk((x_sample,), *params)
    return y_prompt, y_sample
```

```python
import functools

import jax
import jax.numpy as jnp
import numpy as np
from jax import lax
from jax.experimental import pallas as pl
from jax.experimental.pallas import tpu as pltpu
from jax.experimental.pallas import tpu_sc as plsc

F32 = jnp.float32
BF16 = jnp.bfloat16
I32 = jnp.int32

D_MODEL = 1024
DEPTH = 4
HEAD_DIM = 64
GRID_W = 64
NA_HEADS = 8
NA_ROWS = 8
NA_COLS = 16
GQA_Q_HEADS = 8
GQA_KV_HEADS = 2
AXIAL_THETA = 10000.0
QK_NORM_EPS = 1e-6
DIL_HEADS = 16
DIL_BRANCHES = ((128, 1), (512, 4), (2048, 16))
ROPE_THETA = 500000.0
ROPE_DIMS = HEAD_DIM // 4
N_EXPERTS = 32
TOP_K = 4
D_FF = D_MODEL
SWIGLU_LIMIT = 7.0
SWIGLU_ALPHA = 1.702
DN_ALPHA = (2 * DEPTH) ** 0.25
LN_EPS = 1e-5
LOG2E = 1.4426950408889634
Q_SCALE = HEAD_DIM ** -0.5 * LOG2E

LANES = 128
NEG = -1e30
VMEM_LIMIT = 56 * 1024 * 1024
TOKEN_TILE = 512
COMBINE_TILE = 1024
EXPERT_ROWS = 512
NA_ROW_BLOCK = 8
NA_ROW_GROUP = 8
GQA_Q_TILE = 512
GQA_K_TILE = 512
GQA_DEN_ROWS = 16
WIN_Q_TILE = 128
WIN_RADIUS = 64
DIL_GROUP = 4
SC_INDEX_WINDOW = 128
SC_ROW_WINDOW = 64

EVEN_IN = 3 * NA_HEADS * HEAD_DIM + GQA_Q_HEADS * HEAD_DIM + 2 * GQA_KV_HEADS * HEAD_DIM


def _params(*sem):
    return pltpu.CompilerParams(dimension_semantics=sem, vmem_limit_bytes=VMEM_LIMIT)


def _lane_is_low():
    return lax.broadcasted_iota(I32, (1, LANES), 1) < HEAD_DIM


def _proj_kernel(x_ref, w_ref, c_ref, s1_ref, s2_ref, gain_ref, o_ref, *, slab_modes, shift):
    x = x_ref[...].astype(BF16)
    n_out = o_ref.shape[1]
    chunk = 2 * LANES
    if any(m[0] == "norm_rope" for m in slab_modes):
        r = lax.broadcasted_iota(I32, (LANES, LANES), 0) // HEAD_DIM
        c = lax.broadcasted_iota(I32, (LANES, LANES), 1) // HEAD_DIM
        head_mean = jnp.where(r == c, 1.0 / HEAD_DIM, 0.0).astype(BF16)
    for c0 in range(0, n_out, chunk):
        acc = jnp.dot(x, w_ref[:, c0:c0 + chunk], preferred_element_type=F32)
        for s in range(chunk // LANES):
            slab = c0 // LANES + s
            mode, scale, gidx = slab_modes[slab]
            y = acc[:, s * LANES:(s + 1) * LANES]
            if mode == "norm_rope":
                sq = y * y
                hi = sq.astype(BF16)
                lo = (sq - hi.astype(F32)).astype(BF16)
                ms = (jnp.dot(hi, head_mean, preferred_element_type=F32)
                      + jnp.dot(lo, head_mean, preferred_element_type=F32))
                y = y * lax.rsqrt(ms + QK_NORM_EPS) * gain_ref[gidx:gidx + 1, :]
            if mode in ("rope", "norm_rope"):
                y = (y * c_ref[...] + pltpu.roll(y, LANES - shift, 1) * s2_ref[...]
                     + pltpu.roll(y, shift, 1) * s1_ref[...])
            if scale != 1.0:
                y = y * scale
            o_ref[:, slab * LANES:(slab + 1) * LANES] = y.astype(o_ref.dtype)


def _project(x, w, tabs, gains, slab_modes, shift, out_dtype=BF16):
    n, d = x.shape
    m = w.shape[1]
    tm = TOKEN_TILE
    tab_spec = pl.BlockSpec((tm, LANES), lambda i: (i, 0))
    return pl.pallas_call(
        functools.partial(_proj_kernel, slab_modes=tuple(slab_modes), shift=shift),
        grid=(n // tm,),
        in_specs=[pl.BlockSpec((tm, d), lambda i: (i, 0)),
                  pl.BlockSpec((d, m), lambda i: (0, 0)),
                  tab_spec, tab_spec, tab_spec,
                  pl.BlockSpec(gains.shape, lambda i: (0, 0))],
        out_specs=pl.BlockSpec((tm, m), lambda i: (i, 0)),
        out_shape=jax.ShapeDtypeStruct((n, m), out_dtype),
        compiler_params=_params("parallel"),
        name="in_proj",
    )(x, w, tabs[0], tabs[1], tabs[2], gains)


def _layer_norm_rows(z, g, b):
    mu = jnp.mean(z, axis=-1, keepdims=True)
    zc = z - mu
    var = jnp.mean(zc * zc, axis=-1, keepdims=True)
    return zc * lax.rsqrt(var + LN_EPS) * g + b


def _pack_bf16_pairs(v):
    half = v.shape[1] // 2
    bits = pltpu.bitcast(v.astype(BF16).astype(F32), jnp.uint32)
    return (bits[:, :half] >> 16) | bits[:, half:]


def _unpack_bf16_pairs(w):
    lo = pltpu.bitcast(w << 16, F32)
    hi = pltpu.bitcast(w & jnp.uint32(0xFFFF0000), F32)
    return lo, hi


def _route_tile(x, wh_ref, wl_ref, rb_ref, base_ref, idx_ref, gate_ref, rank_ref, cnt_ref):
    xh = x.astype(BF16)
    xl = (x - xh.astype(F32)).astype(BF16)
    nt = (((1,), (1,)), ((), ()))
    logits = (lax.dot_general(wh_ref[...], xh, nt, preferred_element_type=F32)
              + lax.dot_general(wl_ref[...], xh, nt, preferred_element_type=F32)
              + lax.dot_general(wh_ref[...], xl, nt, preferred_element_type=F32)) + rb_ref[...]
    tm = x.shape[0]
    eid = lax.broadcasted_iota(I32, (N_EXPERTS, tm), 0)
    vals = logits
    top_v, top_i, hots = [], [], []
    for _ in range(TOP_K):
        m = jnp.max(vals, axis=0, keepdims=True)
        idx = jnp.min(jnp.where(vals == m, eid, N_EXPERTS), axis=0, keepdims=True)
        hot = eid == idx
        top_v.append(m)
        top_i.append(idx)
        hots.append(hot)
        vals = jnp.where(hot, -jnp.inf, vals)
    es = [jnp.exp(v - top_v[0]) for v in top_v]
    den = functools.reduce(jnp.add, es)
    chosen = functools.reduce(jnp.logical_or, hots)
    before = (lax.broadcasted_iota(I32, (tm, tm), 0) < lax.broadcasted_iota(I32, (tm, tm), 1))
    prefix = jnp.dot(chosen.astype(BF16), before.astype(BF16), preferred_element_type=F32) + base_ref[...]
    for k in range(TOP_K):
        idx_ref[k:k + 1, :] = top_i[k]
        gate_ref[k:k + 1, :] = es[k] / den
        rank_ref[k:k + 1, :] = jnp.sum(jnp.where(hots[k], prefix, 0.0), axis=0, keepdims=True).astype(I32)
    base_ref[...] = base_ref[...] + jnp.sum(chosen.astype(F32), axis=1, keepdims=True)
    cnt_ref[...] = base_ref[...].astype(I32)


def _outproj_ln_kernel(*refs, n_parts):
    a_refs, w_refs = refs[:n_parts], refs[n_parts:2 * n_parts]
    x_ref, g_ref, b_ref, wh_ref, wl_ref, rb_ref = refs[2 * n_parts:2 * n_parts + 6]
    o_ref, p_ref, idx_ref, gate_ref, rank_ref, cnt_ref, base_ref = refs[2 * n_parts + 6:]

    @pl.when(pl.program_id(0) == 0)
    def _():
        base_ref[...] = jnp.zeros_like(base_ref)

    y = functools.reduce(jnp.add, [jnp.dot(a[...], w[...], preferred_element_type=F32)
                                   for a, w in zip(a_refs, w_refs)])
    out = _layer_norm_rows(DN_ALPHA * x_ref[...] + y, g_ref[...], b_ref[...])
    o_ref[...] = out
    p_ref[...] = _pack_bf16_pairs(out)
    _route_tile(out, wh_ref, wl_ref, rb_ref, base_ref, idx_ref, gate_ref, rank_ref, cnt_ref)


def _outproj_ln_route(parts, x, g, b, w_r, b_r):
    n, d = x.shape
    tm = TOKEN_TILE
    row = lambda i: (i, 0)
    fixed = lambda i: (0, 0)
    tok = pl.BlockSpec((TOP_K, tm), lambda i: (0, i))
    wr_t = w_r.T
    wr_hi = wr_t.astype(BF16)
    wr_lo = (wr_t - wr_hi.astype(F32)).astype(BF16)
    expert_vec = pl.BlockSpec((N_EXPERTS, 1), fixed)
    return pl.pallas_call(
        functools.partial(_outproj_ln_kernel, n_parts=len(parts)),
        grid=(n // tm,),
        in_specs=([pl.BlockSpec((tm, a.shape[1]), row) for a, _ in parts]
                  + [pl.BlockSpec(w.shape, fixed) for _, w in parts]
                  + [pl.BlockSpec((tm, d), row), pl.BlockSpec((1, d), fixed), pl.BlockSpec((1, d), fixed),
                     pl.BlockSpec((N_EXPERTS, d), fixed), pl.BlockSpec((N_EXPERTS, d), fixed), expert_vec]),
        out_specs=[pl.BlockSpec((tm, d), row), pl.BlockSpec((tm, d // 2), row), tok, tok, tok, expert_vec],
        out_shape=[jax.ShapeDtypeStruct((n, d), F32), jax.ShapeDtypeStruct((n, d // 2), jnp.uint32),
                   jax.ShapeDtypeStruct((TOP_K, n), I32), jax.ShapeDtypeStruct((TOP_K, n), F32),
                   jax.ShapeDtypeStruct((TOP_K, n), I32), jax.ShapeDtypeStruct((N_EXPERTS, 1), I32)],
        scratch_shapes=[pltpu.VMEM((N_EXPERTS, 1), F32)],
        compiler_params=_params("arbitrary"),
        name="out_proj_ln_route",
    )(*[a for a, _ in parts], *[w for _, w in parts], x, g.reshape(1, d), b.reshape(1, d), wr_hi, wr_lo,
      b_r.reshape(N_EXPERTS, 1).astype(F32))


def _na_kernel(q_ref, k_ref, v_ref, cc_ref, o_ref, *, rows):
    i = pl.program_id(2)
    low = _lane_is_low()

    for g0 in range(0, NA_ROW_BLOCK, NA_ROW_GROUP):
        wins, scores = [], []
        for rr in range(g0, g0 + NA_ROW_GROUP):
            r = i * NA_ROW_BLOCK + rr
            rs = jnp.clip(r - NA_ROWS // 2, 0, rows - NA_ROWS)
            var = r - rs
            q = q_ref[rr * GRID_W:(rr + 1) * GRID_W, :]
            kstart = pl.multiple_of(rs * GRID_W, GRID_W)
            kwin = k_ref[pl.ds(kstart, NA_ROWS * GRID_W), :]
            wins.append(kstart)
            for half in range(2):
                qm = jnp.where(low if half == 0 else jnp.logical_not(low), q, jnp.zeros_like(q))
                s = lax.dot_general(qm, kwin, (((1,), (1,)), ((), ())), preferred_element_type=F32)
                scores.append(s + cc_ref[half, var])
        probs = []
        for s in scores:
            p = jnp.exp2(s - jnp.max(s, axis=-1, keepdims=True))
            probs.append((p.astype(BF16), jnp.sum(p, axis=-1, keepdims=True)))
        for u, rr in enumerate(range(g0, g0 + NA_ROW_GROUP)):
            vwin = v_ref[pl.ds(wins[u], NA_ROWS * GRID_W), :]
            outs = [jnp.dot(p, vwin, preferred_element_type=F32) / l for p, l in probs[2 * u:2 * u + 2]]
            o_ref[rr * GRID_W:(rr + 1) * GRID_W, :] = jnp.where(low, outs[0], outs[1]).astype(o_ref.dtype)


def _na_attention(proj, cc, row0, b, s):
    rows = s // GRID_W
    assert rows >= NA_ROWS and rows % NA_ROW_BLOCK == 0 and row0 % s == 0
    tq = NA_ROW_BLOCK * GRID_W
    n_slab = NA_HEADS * HEAD_DIM // LANES
    qb0, sb0 = row0 // tq, row0 // s
    return pl.pallas_call(
        functools.partial(_na_kernel, rows=rows),
        grid=(b, n_slab, s // tq),
        in_specs=[pl.BlockSpec((tq, LANES), lambda bi, j, i: (qb0 + bi * (s // tq) + i, j)),
                  pl.BlockSpec((s, LANES), lambda bi, j, i: (sb0 + bi, n_slab + j)),
                  pl.BlockSpec((s, LANES), lambda bi, j, i: (sb0 + bi, 2 * n_slab + j)),
                  pl.BlockSpec((2, NA_ROWS, GRID_W, NA_ROWS * GRID_W), lambda bi, j, i: (j, 0, 0, 0))],
        out_specs=pl.BlockSpec((tq, LANES), lambda bi, j, i: (bi * (s // tq) + i, j)),
        out_shape=jax.ShapeDtypeStruct((b * s, n_slab * LANES), BF16),
        compiler_params=_params("parallel", "parallel", "parallel"),
        name="na_attn",
    )(proj, proj, proj, cc)


def _na_bias_table(rpb):
    var = np.arange(NA_ROWS)[:, None]
    j = np.arange(NA_ROWS)[None, :]
    qc = np.arange(GRID_W)[:, None]
    kc = np.arange(GRID_W)[None, :]
    row_sel = ((j - var + NA_ROWS - 1)[..., None] == np.arange(2 * NA_ROWS - 1)).astype(np.float32)
    col_sel = (np.clip(kc - qc + NA_COLS - 1, 0, 2 * NA_COLS - 2)[..., None]
               == np.arange(2 * NA_COLS - 1)).astype(np.float32)
    win = np.clip(qc - NA_COLS // 2, 0, GRID_W - NA_COLS)
    valid = (kc >= win) & (kc < win + NA_COLS)
    t = jnp.einsum("hab,vja,qkb->hvqjk", rpb.astype(F32), row_sel, col_sel, precision=lax.Precision.HIGHEST)
    t = jnp.where(valid[None, None, :, None, :], t * LOG2E, NEG)
    return t.reshape(rpb.shape[0], NA_ROWS, GRID_W, NA_ROWS * GRID_W)


def _gqa_kernel(q_ref, k_ref, vt_ref, o_ref, acc_ref, st_ref, *, n_kt):
    assert n_kt % 2 == 0
    low = _lane_is_low()
    q = q_ref[...]
    tq = q.shape[0]
    zero = jnp.zeros_like(q)
    qs = jnp.concatenate([jnp.where(low, q, zero), jnp.where(low, zero, q)], axis=0)
    acc_ref[...] = jnp.zeros_like(acc_ref)

    def scores(kt, slot):
        start = pl.multiple_of(kt * GQA_K_TILE, GQA_K_TILE)
        kb = k_ref[pl.ds(start, GQA_K_TILE), :]
        st_ref[slot] = lax.dot_general(kb, qs, (((1,), (1,)), ((), ())), preferred_element_type=F32)

    def softmax_pv(kt, slot, m_old):
        st = st_ref[slot]
        m_new = jnp.maximum(m_old, jnp.max(st, axis=0, keepdims=True))
        alpha = jnp.exp2(m_old - m_new)
        pt = jnp.exp2(st - m_new).astype(BF16)
        acc_ref[...] = alpha * acc_ref[...] + jnp.dot(vt_ref[kt], pt, preferred_element_type=F32)
        return m_new

    def step(i, m):
        kt = 2 * i
        scores(kt + 1, 1)
        m = softmax_pv(kt, 0, m)
        scores(jnp.minimum(kt + 2, n_kt - 1), 0)
        return softmax_pv(kt + 1, 1, m)

    scores(0, 0)
    lax.fori_loop(0, n_kt // 2, step, jnp.full((1, 2 * tq), -jnp.inf, F32))
    out_t = acc_ref[:LANES, :] / acc_ref[LANES:LANES + 1, :]
    o_t = jnp.concatenate([out_t[:HEAD_DIM, :tq], out_t[HEAD_DIM:, tq:]], axis=0)
    o_ref[...] = o_t.T.astype(o_ref.dtype)


def _gqa_attention(proj, row0, b, s):
    tq, tk = GQA_Q_TILE, GQA_K_TILE
    assert s % tk == 0 and row0 % s == 0
    n_slab = GQA_Q_HEADS * HEAD_DIM // LANES
    q_col = 3 * NA_HEADS * HEAD_DIM // LANES
    k_col = q_col + n_slab
    v0 = (k_col + 1) * LANES
    qb0, sb0 = row0 // tq, row0 // s
    v_t = proj[row0:row0 + b * s, v0:v0 + LANES].reshape(b, s // tk, tk, LANES).transpose(0, 1, 3, 2)
    ones_rows = jnp.zeros((b, s // tk, GQA_DEN_ROWS, tk), BF16).at[:, :, 0, :].set(1.0)
    v_t = jnp.concatenate([v_t, ones_rows], axis=2)
    vt_rows = LANES + GQA_DEN_ROWS
    return pl.pallas_call(
        functools.partial(_gqa_kernel, n_kt=s // tk),
        grid=(b, n_slab, s // tq),
        in_specs=[pl.BlockSpec((tq, LANES), lambda bi, j, i: (qb0 + bi * (s // tq) + i, q_col + j)),
                  pl.BlockSpec((s, LANES), lambda bi, j, i: (sb0 + bi, k_col)),
                  pl.BlockSpec((None, s // tk, vt_rows, tk), lambda bi, j, i: (bi, 0, 0, 0))],
        out_specs=pl.BlockSpec((tq, LANES), lambda bi, j, i: (bi * (s // tq) + i, j)),
        out_shape=jax.ShapeDtypeStruct((b * s, n_slab * LANES), BF16),
        scratch_shapes=[pltpu.VMEM((vt_rows, 2 * tq), F32), pltpu.VMEM((2, tk, 2 * tq), F32)],
        compiler_params=_params("parallel", "parallel", "parallel"),
        name="gqa_attn",
    )(proj, proj, v_t)


def _dil_kernel(q_ref, k_ref, v_ref, o_ref, m_ref, l_ref, acc_ref, *, seq):
    tile = q_ref.shape[0]
    t0 = pl.program_id(2) * tile
    low = _lane_is_low()
    sub = WIN_Q_TILE
    n_sub = tile // sub

    for bi, (_, r) in enumerate(DIL_BRANCHES):
        length = seq // r
        win = min(sub + 2 * WIN_RADIUS, length)

        def sub_blocks(g, carry, r=r, length=length, win=win, first=(bi == 0)):
            rows, scores, stats = [], [], []
            for u in range(DIL_GROUP):
                n = g * DIL_GROUP + u
                c = n % r
                blk = n // r
                u0 = t0 // r + blk * sub
                ks = jnp.clip(u0 - WIN_RADIUS, 0, length - win)
                if r == 1:
                    q_rows = pl.ds(pl.multiple_of(n * sub, sub), sub)
                    k_rows = pl.ds(pl.multiple_of(ks, WIN_RADIUS), win)
                else:
                    q_rows = pl.ds(c + r * blk * sub, sub, stride=r)
                    k_rows = pl.ds(c + r * ks, win, stride=r)
                rows.append((q_rows, k_rows))
                q = q_ref[q_rows, :].astype(BF16)
                kwin = k_ref[k_rows, :].astype(BF16)
                qpos = u0 + lax.broadcasted_iota(I32, (sub, win), 0)
                kpos = ks + lax.broadcasted_iota(I32, (sub, win), 1)
                valid = jnp.abs(kpos - qpos) <= WIN_RADIUS
                for half in range(2):
                    qm = jnp.where(low if half == 0 else jnp.logical_not(low), q, jnp.zeros_like(q))
                    s = lax.dot_general(qm, kwin, (((1,), (1,)), ((), ())), preferred_element_type=F32)
                    scores.append(jnp.where(valid, s, NEG))
            for s in scores:
                m = jnp.max(s, axis=-1, keepdims=True)
                p = jnp.exp2(s - m)
                stats.append((m, jnp.sum(p, axis=-1, keepdims=True), p.astype(BF16)))
            for u, (q_rows, k_rows) in enumerate(rows):
                vwin = v_ref[k_rows, :].astype(BF16)
                (m0, l0, p0), (m1, l1, p1) = stats[2 * u:2 * u + 2]
                m_b = jnp.where(low, m0, m1)
                l_b = jnp.where(low, l0, l1)
                pv_b = jnp.where(low, jnp.dot(p0, vwin, preferred_element_type=F32),
                                 jnp.dot(p1, vwin, preferred_element_type=F32))
                if first:
                    m_ref[q_rows, :] = m_b
                    l_ref[q_rows, :] = l_b
                    acc_ref[q_rows, :] = pv_b
                else:
                    m_old = m_ref[q_rows, :]
                    m_new = jnp.maximum(m_old, m_b)
                    a_old = jnp.exp2(m_old - m_new)
                    a_b = jnp.exp2(m_b - m_new)
                    l_ref[q_rows, :] = a_old * l_ref[q_rows, :] + a_b * l_b
                    acc_ref[q_rows, :] = a_old * acc_ref[q_rows, :] + a_b * pv_b
                    m_ref[q_rows, :] = m_new
            return carry

        lax.fori_loop(0, n_sub // DIL_GROUP, sub_blocks, 0)

    o_ref[...] = (acc_ref[...] / l_ref[...]).astype(o_ref.dtype)


def _dilated_attention(qkv, row0, b, s):
    tile = 16 * WIN_Q_TILE
    assert s % tile == 0 and row0 % s == 0
    n_slab = DIL_HEADS * HEAD_DIM // LANES
    qb0, sb0 = row0 // tile, row0 // s
    state = pltpu.VMEM((tile, LANES), F32)
    return pl.pallas_call(
        functools.partial(_dil_kernel, seq=s),
        grid=(b, n_slab, s // tile),
        in_specs=[pl.BlockSpec((tile, LANES), lambda bi, j, i: (qb0 + bi * (s // tile) + i, j)),
                  pl.BlockSpec((s, LANES), lambda bi, j, i: (sb0 + bi, n_slab + j)),
                  pl.BlockSpec((s, LANES), lambda bi, j, i: (sb0 + bi, 2 * n_slab + j))],
        out_specs=pl.BlockSpec((tile, LANES), lambda bi, j, i: (bi * (s // tile) + i, j)),
        out_shape=jax.ShapeDtypeStruct((b * s, n_slab * LANES), BF16),
        scratch_shapes=[state, state, state],
        compiler_params=_params("parallel", "parallel", "parallel"),
        name="dil_attn",
    )(qkv, qkv, qkv)


def _expert_kernel(blk_exp_ref, n_used_ref, x_ref, w1_ref, b1_ref, w2_ref, b2_ref, o_ref):
    blk = pl.program_id(0)

    @pl.when(blk < n_used_ref[0])
    def _():
        x_lo, x_hi = _unpack_bf16_pairs(x_ref[...])
        x = jnp.concatenate([x_lo.astype(BF16), x_hi.astype(BF16)], axis=1)
        h = jnp.dot(x, w1_ref[...], preferred_element_type=F32) + b1_ref[...]
        g = jnp.minimum(h[:, :D_FF], SWIGLU_LIMIT)
        u = jnp.clip(h[:, D_FF:], -SWIGLU_LIMIT, SWIGLU_LIMIT)
        act = g * jax.nn.sigmoid(SWIGLU_ALPHA * g) * (u + 1.0)
        y = jnp.dot(act.astype(BF16), w2_ref[...], preferred_element_type=F32) + b2_ref[...]
        o_ref[...] = _pack_bf16_pairs(y)

    @pl.when(blk >= n_used_ref[0])
    def _():
        o_ref[...] = jnp.zeros_like(o_ref)


def _experts(xs, blk_exp, n_used, layer, w1, b1, w2, b2):
    n_rows = xs.shape[0]
    d = D_MODEL
    bm = EXPERT_ROWS
    f2 = w1.shape[3]
    wmap = lambda i, be, nu: (layer, be[i], 0, 0)
    return pl.pallas_call(
        _expert_kernel,
        grid_spec=pltpu.PrefetchScalarGridSpec(
            num_scalar_prefetch=2,
            grid=(n_rows // bm,),
            in_specs=[pl.BlockSpec((bm, d // 2), lambda i, be, nu: (i, 0)),
                      pl.BlockSpec((None, None, d, f2), wmap),
                      pl.BlockSpec((None, None, 1, f2), wmap),
                      pl.BlockSpec((None, None, f2 // 2, d), wmap),
                      pl.BlockSpec((None, None, 1, d), wmap)],
            out_specs=pl.BlockSpec((bm, d // 2), lambda i, be, nu: (i, 0)),
        ),
        out_shape=jax.ShapeDtypeStruct((n_rows, d // 2), jnp.uint32),
        compiler_params=_params("arbitrary"),
        name="experts",
    )(blk_exp, n_used, xs, w1, b1, w2, b2)


def _combine_ln_kernel(y_ref, gate_ref, x_ref, g_ref, b_ref, o_ref):
    gate = gate_ref[...]
    halves = [_unpack_bf16_pairs(y_ref[k]) for k in range(TOP_K)]
    y = jnp.concatenate([functools.reduce(jnp.add, [gate[:, k:k + 1] * halves[k][h] for k in range(TOP_K)])
                         for h in range(2)], axis=1)
    o_ref[...] = _layer_norm_rows(DN_ALPHA * x_ref[...] + y, g_ref[...], b_ref[...])


def _combine_ln(yg, gate, x, g, b, row0=0, n_rows=None):
    n, d = x.shape
    n_rows = n if n_rows is None else n_rows
    tm = COMBINE_TILE
    assert row0 % tm == 0 and n_rows % tm == 0
    blk0 = row0 // tm
    row = lambda i: (blk0 + i, 0)
    fixed = lambda i: (0, 0)
    return pl.pallas_call(
        _combine_ln_kernel,
        grid=(n_rows // tm,),
        in_specs=[pl.BlockSpec((TOP_K, tm, d // 2), lambda i: (0, blk0 + i, 0)), pl.BlockSpec((tm, TOP_K), row),
                  pl.BlockSpec((tm, d), row), pl.BlockSpec((1, d), fixed), pl.BlockSpec((1, d), fixed)],
        out_specs=pl.BlockSpec((tm, d), lambda i: (i, 0)),
        out_shape=jax.ShapeDtypeStruct((n_rows, d), F32),
        compiler_params=_params("parallel"),
        name="combine_ln",
    )(yg, gate, x, g.reshape(1, d), b.reshape(1, d))


def _sc_mesh():
    return plsc.VectorSubcoreMesh(core_axis_name="core", subcore_axis_name="subcore")


def _sc_scatter_rows(x, dest, n_out):
    n, d = x.shape
    mesh = _sc_mesh()
    per_worker = n // (mesh.num_cores * mesh.num_subcores)
    n_sub = SC_INDEX_WINDOW // SC_ROW_WINDOW
    assert per_worker % SC_INDEX_WINDOW == 0 and n_sub >= 2

    @functools.partial(
        pl.kernel, out_type=jax.ShapeDtypeStruct((n_out, d), x.dtype), mesh=mesh,
        scratch_types=[pltpu.VMEM((TOP_K, SC_INDEX_WINDOW), I32), pltpu.VMEM((2, SC_ROW_WINDOW, d), x.dtype),
                       pltpu.SemaphoreType.DMA((2,)), pltpu.SemaphoreType.DMA((2,))])
    def scatter(x_hbm, i_hbm, o_hbm, idx_v, buf, sem_r, sem_s):
        wid = lax.axis_index("core") * mesh.num_subcores + lax.axis_index("subcore")

        @pl.loop(0, per_worker // SC_INDEX_WINDOW)
        def _(it):
            base = wid * per_worker + it * SC_INDEX_WINDOW
            for k in range(TOP_K):
                pltpu.sync_copy(i_hbm.at[k, pl.ds(base, SC_INDEX_WINDOW)], idx_v.at[k])

            def read(j):
                rows = pl.ds(base + j * SC_ROW_WINDOW, SC_ROW_WINDOW)
                return pltpu.make_async_copy(x_hbm.at[rows], buf.at[j % 2], sem_r.at[j % 2])

            def send(j, k):
                rows = idx_v.at[k, pl.ds(j * SC_ROW_WINDOW, SC_ROW_WINDOW)]
                return pltpu.make_async_copy(buf.at[j % 2], o_hbm.at[rows], sem_s.at[j % 2])

            read(0).start()
            for j in range(n_sub):
                read(j).wait()
                for k in range(TOP_K):
                    send(j, k).start()
                if j + 1 < n_sub:
                    if j >= 1:
                        for k in range(TOP_K):
                            send(j - 1, k).wait()
                    read(j + 1).start()
            for j in (n_sub - 2, n_sub - 1):
                for k in range(TOP_K):
                    send(j, k).wait()

    return scatter(x, dest)


def _sc_gather_rows(table, indices):
    num = indices.shape[0]
    d = table.shape[1]
    mesh = _sc_mesh()
    per_worker = num // (mesh.num_cores * mesh.num_subcores)
    n_sub = SC_INDEX_WINDOW // SC_ROW_WINDOW
    assert per_worker % SC_INDEX_WINDOW == 0 and n_sub >= 2

    @functools.partial(
        pl.kernel, out_type=jax.ShapeDtypeStruct((num, d), table.dtype), mesh=mesh,
        scratch_types=[pltpu.VMEM((SC_INDEX_WINDOW,), I32), pltpu.VMEM((2, SC_ROW_WINDOW, d), table.dtype),
                       pltpu.SemaphoreType.DMA((2,)), pltpu.SemaphoreType.DMA((2,))])
    def gather(x_hbm, i_hbm, o_hbm, idx_v, buf, sem_g, sem_w):
        wid = lax.axis_index("core") * mesh.num_subcores + lax.axis_index("subcore")

        @pl.loop(0, per_worker // SC_INDEX_WINDOW)
        def _(it):
            base = wid * per_worker + it * SC_INDEX_WINDOW
            pltpu.sync_copy(i_hbm.at[pl.ds(base, SC_INDEX_WINDOW)], idx_v)

            def fetch(j):
                rows = idx_v.at[pl.ds(j * SC_ROW_WINDOW, SC_ROW_WINDOW)]
                return pltpu.make_async_copy(x_hbm.at[rows], buf.at[j % 2], sem_g.at[j % 2])

            def write(j):
                rows = pl.ds(base + j * SC_ROW_WINDOW, SC_ROW_WINDOW)
                return pltpu.make_async_copy(buf.at[j % 2], o_hbm.at[rows], sem_w.at[j % 2])

            fetch(0).start()
            for j in range(n_sub):
                fetch(j).wait()
                write(j).start()
                if j + 1 < n_sub:
                    if j >= 1:
                        write(j - 1).wait()
                    fetch(j + 1).start()
            write(n_sub - 2).wait()
            write(n_sub - 1).wait()

    return gather(table, indices)


def _moe(x_packed, idx, gate, rank, counts, layer, w1, b1, w2, b2):
    n = x_packed.shape[0]
    d = D_MODEL
    bm = EXPERT_ROWS
    counts = counts[:, 0]
    padded = (counts + bm - 1) // bm * bm
    pad_ends = jnp.cumsum(padded)
    pad_starts = pad_ends - padded
    hot = idx[:, None, :] == jnp.arange(N_EXPERTS, dtype=I32)[None, :, None]
    dest = jnp.sum(jnp.where(hot, pad_starts[None, :, None], 0), axis=1) + rank
    n_blocks = n * TOP_K // bm + N_EXPERTS
    blk_start = jnp.arange(n_blocks, dtype=I32) * bm
    blk_exp = jnp.minimum(jnp.sum(blk_start[:, None] >= pad_ends[None, :], axis=1), N_EXPERTS - 1).astype(I32)
    n_used = (pad_ends[-1:] // bm).astype(I32)
    xs = _sc_scatter_rows(x_packed, dest, n_blocks * bm)
    ys = _experts(xs, blk_exp, n_used, layer, w1, b1, w2, b2)
    yg = _sc_gather_rows(ys, dest.reshape(-1)).reshape(TOP_K, n, d // 2)
    return yg, gate.T


def _positions(groups):
    return jnp.concatenate([jnp.tile(jnp.arange(s), b) for b, s in groups])


def _axial_tables(groups):
    t = _positions(groups)
    n = HEAD_DIM // 4
    inv = AXIAL_THETA ** (-jnp.arange(n, dtype=F32) / n)
    ar = (t // GRID_W).astype(F32)[:, None] * inv
    ac = (t % GRID_W).astype(F32)[:, None] * inv
    z = jnp.zeros_like(ar)
    cr, sr, cc, sc = jnp.cos(ar), jnp.sin(ar), jnp.cos(ac), jnp.sin(ac)
    c = jnp.concatenate([cr, cr, cc, cc], axis=-1)
    s1 = jnp.concatenate([z, sr, z, sc], axis=-1)
    s2 = jnp.concatenate([-sr, z, -sc, z], axis=-1)
    return tuple(jnp.tile(a, (1, 2)) for a in (c, s1, s2))


def _rope_tables(groups):
    t = _positions(groups)
    n = ROPE_DIMS // 2
    inv = ROPE_THETA ** (-jnp.arange(n, dtype=F32) / n)
    ang = t.astype(F32)[:, None] * inv
    c, s = jnp.cos(ang), jnp.sin(ang)
    z = jnp.zeros_like(c)
    rest = HEAD_DIM - ROPE_DIMS
    pad1 = jnp.ones((t.shape[0], rest), F32)
    pad0 = jnp.zeros((t.shape[0], rest), F32)
    cc = jnp.concatenate([c, c, pad1], axis=-1)
    s1 = jnp.concatenate([z, s, pad0], axis=-1)
    s2 = jnp.concatenate([-s, z, pad0], axis=-1)
    return tuple(jnp.tile(a, (1, 2)) for a in (cc, s1, s2))


def _gqa_head_order():
    g = GQA_Q_HEADS // GQA_KV_HEADS
    return [h for j in range(g) for h in (j, g + j)]


def _mixer_even(x, groups, tabs, w_in, rpb, q_gain, k_gain, w_out):
    hd = HEAD_DIM
    na_w = NA_HEADS * hd
    order = _gqa_head_order()
    q0 = 3 * na_w
    q_cols = np.concatenate([q0 + h * hd + np.arange(hd) for h in order])
    w_in_p = jnp.concatenate([w_in[:, :q0], w_in[:, q_cols], w_in[:, q0 + GQA_Q_HEADS * hd:]], axis=1).astype(BF16)
    out_rows = np.concatenate([na_w + h * hd + np.arange(hd) for h in order])
    w_out_p = jnp.concatenate([w_out[:na_w], w_out[out_rows]], axis=0).astype(BF16)
    n_na = na_w // LANES
    n_q = GQA_Q_HEADS * hd // LANES
    modes = ([("plain", Q_SCALE, 0)] * n_na + [("plain", 1.0, 0)] * (2 * n_na)
             + [("norm_rope", Q_SCALE, 0)] * n_q + [("norm_rope", 1.0, 1)] + [("plain", 1.0, 0)])
    gains = jnp.stack([jnp.tile(q_gain.astype(F32), 2), jnp.tile(k_gain.astype(F32), 2)])
    proj = _project(x, w_in_p, tabs, gains, modes, HEAD_DIM // 4)
    cc = _na_bias_table(rpb)
    ya, yb = [], []
    row0 = 0
    for b, s in groups:
        ya.append(_na_attention(proj, cc, row0, b, s))
        yb.append(_gqa_attention(proj, row0, b, s))
        row0 += b * s
    return [(jnp.concatenate(ya, axis=0), w_out_p[:na_w]), (jnp.concatenate(yb, axis=0), w_out_p[na_w:])]


def _mixer_odd(x, groups, tabs, w_in, w_out):
    n_slab = DIL_HEADS * HEAD_DIM // LANES
    modes = [("rope", Q_SCALE, 0)] * n_slab + [("rope", 1.0, 0)] * n_slab + [("plain", 1.0, 0)] * n_slab
    gains = jnp.ones((1, LANES), F32)
    qkv = _project(x, w_in.astype(BF16), tabs, gains, modes, ROPE_DIMS // 2, out_dtype=F32)
    parts, row0 = [], 0
    for b, s in groups:
        parts.append(_dilated_attention(qkv, row0, b, s))
        row0 += b * s
    return [(jnp.concatenate(parts, axis=0), w_out.astype(BF16))]


def _trunk(xs, w_in_even, rpb_a, q_gain_b, k_gain_b, w_out_even, w_in_odd, w_out_odd,
           ln1_g, ln1_b, ln2_g, ln2_b, router_w, router_b, moe_w1, moe_b1, moe_w2, moe_b2):
    groups = [(x.shape[0], x.shape[1]) for x in xs]
    x = jnp.concatenate([t.reshape(-1, D_MODEL) for t in xs], axis=0).astype(F32)
    tabs_even = _axial_tables(groups)
    tabs_odd = _rope_tables(groups)
    b1 = moe_b1.astype(F32)[:, :, None, :]
    b2 = moe_b2.astype(F32)[:, :, None, :]
    for l in range(DEPTH):
        i = l // 2
        if l % 2 == 0:
            parts = _mixer_even(x, groups, tabs_even, w_in_even[i], rpb_a[i], q_gain_b[i], k_gain_b[i],
                                w_out_even[i])
        else:
            parts = _mixer_odd(x, groups, tabs_odd, w_in_odd[i], w_out_odd[i])
        x, x_packed, idx, gate, rank, counts = _outproj_ln_route(parts, x, ln1_g[l], ln1_b[l], router_w[l],
                                                                 router_b[l])
        yg, gate = _moe(x_packed, idx, gate, rank, counts, l, moe_w1, b1, moe_w2, b2)
        if l + 1 < DEPTH:
            x = _combine_ln(yg, gate, x, ln2_g[l], ln2_b[l])
    outs, row0 = [], 0
    for t in xs:
        n = t.shape[0] * t.shape[1]
        outs.append(_combine_ln(yg, gate, x, ln2_g[-1], ln2_b[-1], row0, n).reshape(t.shape))
        row0 += n
    return tuple(outs)


def kernel(x_prompt, x_sample, w_in_even, rpb_a, q_gain_b, k_gain_b, w_out_even, w_in_odd, w_out_odd, ln1_g, ln1_b,
           ln2_g, ln2_b, router_w, router_b, moe_w1, moe_b1, moe_w2, moe_b2):
    params = (w_in_even, rpb_a, q_gain_b, k_gain_b, w_out_even, w_in_odd, w_out_odd,
              ln1_g, ln1_b, ln2_g, ln2_b, router_w, router_b, moe_w1.astype(BF16), moe_b1, moe_w2.astype(BF16), moe_b2)
    (y_prompt,) = _trunk((x_prompt,), *params)
    (y_sample,) = _trunk((x_sample,), *params)
    return y_prompt, y_sample
```

```python
import functools

import jax
import jax.numpy as jnp
import numpy as np
from jax import lax
from jax.experimental import pallas as pl
from jax.experimental.pallas import tpu as pltpu
from jax.experimental.pallas import tpu_sc as plsc

F32 = jnp.float32
BF16 = jnp.bfloat16
I32 = jnp.int32

D_MODEL = 1024
DEPTH = 4
HEAD_DIM = 64
GRID_W = 64
NA_HEADS = 8
NA_ROWS = 8
NA_COLS = 16
GQA_Q_HEADS = 8
GQA_KV_HEADS = 2
AXIAL_THETA = 10000.0
QK_NORM_EPS = 1e-6
DIL_HEADS = 16
DIL_BRANCHES = ((128, 1), (512, 4), (2048, 16))
ROPE_THETA = 500000.0
ROPE_DIMS = HEAD_DIM // 4
N_EXPERTS = 32
TOP_K = 4
D_FF = D_MODEL
SWIGLU_LIMIT = 7.0
SWIGLU_ALPHA = 1.702
DN_ALPHA = (2 * DEPTH) ** 0.25
LN_EPS = 1e-5
LOG2E = 1.4426950408889634
Q_SCALE = HEAD_DIM ** -0.5 * LOG2E

LANES = 128
NEG = -1e30
VMEM_LIMIT = 56 * 1024 * 1024
TOKEN_TILE = 512
COMBINE_TILE = 1024
EXPERT_ROWS = 512
NA_ROW_BLOCK = 8
NA_ROW_GROUP = 8
GQA_Q_TILE = 512
GQA_K_TILE = 512
GQA_DEN_ROWS = 16
WIN_Q_TILE = 128
WIN_RADIUS = 64
DIL_GROUP = 8
SC_INDEX_WINDOW = 128
SC_ROW_WINDOW = 64

EVEN_IN = 3 * NA_HEADS * HEAD_DIM + GQA_Q_HEADS * HEAD_DIM + 2 * GQA_KV_HEADS * HEAD_DIM


def _params(*sem):
    return pltpu.CompilerParams(dimension_semantics=sem, vmem_limit_bytes=VMEM_LIMIT)


def _lane_is_low():
    return lax.broadcasted_iota(I32, (1, LANES), 1) < HEAD_DIM


def _proj_kernel(x_ref, w_ref, c_ref, s1_ref, s2_ref, gain_ref, o_ref, *, slab_modes, shift):
    x = x_ref[...].astype(BF16)
    n_out = o_ref.shape[1]
    chunk = 2 * LANES
    if any(m[0] == "norm_rope" for m in slab_modes):
        r = lax.broadcasted_iota(I32, (LANES, LANES), 0) // HEAD_DIM
        c = lax.broadcasted_iota(I32, (LANES, LANES), 1) // HEAD_DIM
        head_mean = jnp.where(r == c, 1.0 / HEAD_DIM, 0.0).astype(BF16)
    for c0 in range(0, n_out, chunk):
        acc = jnp.dot(x, w_ref[:, c0:c0 + chunk], preferred_element_type=F32)
        for s in range(chunk // LANES):
            slab = c0 // LANES + s
            mode, scale, gidx = slab_modes[slab]
            y = acc[:, s * LANES:(s + 1) * LANES]
            if mode == "norm_rope":
                sq = y * y
                hi = sq.astype(BF16)
                lo = (sq - hi.astype(F32)).astype(BF16)
                ms = (jnp.dot(hi, head_mean, preferred_element_type=F32)
                      + jnp.dot(lo, head_mean, preferred_element_type=F32))
                y = y * lax.rsqrt(ms + QK_NORM_EPS) * gain_ref[gidx:gidx + 1, :]
            if mode in ("rope", "norm_rope"):
                y = (y * c_ref[...] + pltpu.roll(y, LANES - shift, 1) * s2_ref[...]
                     + pltpu.roll(y, shift, 1) * s1_ref[...])
            if scale != 1.0:
                y = y * scale
            o_ref[:, slab * LANES:(slab + 1) * LANES] = y.astype(o_ref.dtype)


def _project(x, w, tabs, gains, slab_modes, shift, out_dtype=BF16):
    n, d = x.shape
    m = w.shape[1]
    tm = TOKEN_TILE
    tab_spec = pl.BlockSpec((tm, LANES), lambda i: (i, 0))
    return pl.pallas_call(
        functools.partial(_proj_kernel, slab_modes=tuple(slab_modes), shift=shift),
        grid=(n // tm,),
        in_specs=[pl.BlockSpec((tm, d), lambda i: (i, 0)),
                  pl.BlockSpec((d, m), lambda i: (0, 0)),
                  tab_spec, tab_spec, tab_spec,
                  pl.BlockSpec(gains.shape, lambda i: (0, 0))],
        out_specs=pl.BlockSpec((tm, m), lambda i: (i, 0)),
        out_shape=jax.ShapeDtypeStruct((n, m), out_dtype),
        compiler_params=_params("parallel"),
        name="in_proj",
    )(x, w, tabs[0], tabs[1], tabs[2], gains)


def _layer_norm_rows(z, g, b):
    mu = jnp.mean(z, axis=-1, keepdims=True)
    zc = z - mu
    var = jnp.mean(zc * zc, axis=-1, keepdims=True)
    return zc * lax.rsqrt(var + LN_EPS) * g + b


def _pack_bf16_pairs(v):
    half = v.shape[1] // 2
    bits = pltpu.bitcast(v.astype(BF16).astype(F32), jnp.uint32)
    return (bits[:, :half] >> 16) | bits[:, half:]


def _unpack_bf16_pairs(w):
    lo = pltpu.bitcast(w << 16, F32)
    hi = pltpu.bitcast(w & jnp.uint32(0xFFFF0000), F32)
    return lo, hi


def _route_tile(x, wh_ref, wl_ref, rb_ref, base_ref, idx_ref, gate_ref, rank_ref, cnt_ref):
    xh = x.astype(BF16)
    xl = (x - xh.astype(F32)).astype(BF16)
    nt = (((1,), (1,)), ((), ()))
    logits = (lax.dot_general(wh_ref[...], xh, nt, preferred_element_type=F32)
              + lax.dot_general(wl_ref[...], xh, nt, preferred_element_type=F32)
              + lax.dot_general(wh_ref[...], xl, nt, preferred_element_type=F32)) + rb_ref[...]
    tm = x.shape[0]
    eid = lax.broadcasted_iota(I32, (N_EXPERTS, tm), 0)
    vals = logits
    top_v, top_i, hots = [], [], []
    for _ in range(TOP_K):
        m = jnp.max(vals, axis=0, keepdims=True)
        idx = jnp.min(jnp.where(vals == m, eid, N_EXPERTS), axis=0, keepdims=True)
        hot = eid == idx
        top_v.append(m)
        top_i.append(idx)
        hots.append(hot)
        vals = jnp.where(hot, -jnp.inf, vals)
    es = [jnp.exp(v - top_v[0]) for v in top_v]
    den = functools.reduce(jnp.add, es)
    chosen = functools.reduce(jnp.logical_or, hots)
    before = (lax.broadcasted_iota(I32, (tm, tm), 0) < lax.broadcasted_iota(I32, (tm, tm), 1))
    prefix = jnp.dot(chosen.astype(BF16), before.astype(BF16), preferred_element_type=F32) + base_ref[...]
    for k in range(TOP_K):
        idx_ref[k:k + 1, :] = top_i[k]
        gate_ref[k:k + 1, :] = es[k] / den
        rank_ref[k:k + 1, :] = jnp.sum(jnp.where(hots[k], prefix, 0.0), axis=0, keepdims=True).astype(I32)
    base_ref[...] = base_ref[...] + jnp.sum(chosen.astype(F32), axis=1, keepdims=True)
    cnt_ref[...] = base_ref[...].astype(I32)


def _outproj_ln_kernel(*refs, n_parts):
    a_refs, w_refs = refs[:n_parts], refs[n_parts:2 * n_parts]
    x_ref, g_ref, b_ref, wh_ref, wl_ref, rb_ref = refs[2 * n_parts:2 * n_parts + 6]
    o_ref, p_ref, idx_ref, gate_ref, rank_ref, cnt_ref, base_ref = refs[2 * n_parts + 6:]

    @pl.when(pl.program_id(0) == 0)
    def _():
        base_ref[...] = jnp.zeros_like(base_ref)

    y = functools.reduce(jnp.add, [jnp.dot(a[...], w[...], preferred_element_type=F32)
                                   for a, w in zip(a_refs, w_refs)])
    out = _layer_norm_rows(DN_ALPHA * x_ref[...] + y, g_ref[...], b_ref[...])
    o_ref[...] = out
    p_ref[...] = _pack_bf16_pairs(out)
    _route_tile(out, wh_ref, wl_ref, rb_ref, base_ref, idx_ref, gate_ref, rank_ref, cnt_ref)


def _outproj_ln_route(parts, x, g, b, w_r, b_r):
    n, d = x.shape
    tm = TOKEN_TILE
    row = lambda i: (i, 0)
    fixed = lambda i: (0, 0)
    tok = pl.BlockSpec((TOP_K, tm), lambda i: (0, i))
    wr_t = w_r.T
    wr_hi = wr_t.astype(BF16)
    wr_lo = (wr_t - wr_hi.astype(F32)).astype(BF16)
    expert_vec = pl.BlockSpec((N_EXPERTS, 1), fixed)
    return pl.pallas_call(
        functools.partial(_outproj_ln_kernel, n_parts=len(parts)),
        grid=(n // tm,),
        in_specs=([pl.BlockSpec((tm, a.shape[1]), row) for a, _ in parts]
                  + [pl.BlockSpec(w.shape, fixed) for _, w in parts]
                  + [pl.BlockSpec((tm, d), row), pl.BlockSpec((1, d), fixed), pl.BlockSpec((1, d), fixed),
                     pl.BlockSpec((N_EXPERTS, d), fixed), pl.BlockSpec((N_EXPERTS, d), fixed), expert_vec]),
        out_specs=[pl.BlockSpec((tm, d), row), pl.BlockSpec((tm, d // 2), row), tok, tok, tok, expert_vec],
        out_shape=[jax.ShapeDtypeStruct((n, d), F32), jax.ShapeDtypeStruct((n, d // 2), jnp.uint32),
                   jax.ShapeDtypeStruct((TOP_K, n), I32), jax.ShapeDtypeStruct((TOP_K, n), F32),
                   jax.ShapeDtypeStruct((TOP_K, n), I32), jax.ShapeDtypeStruct((N_EXPERTS, 1), I32)],
        scratch_shapes=[pltpu.VMEM((N_EXPERTS, 1), F32)],
        compiler_params=_params("arbitrary"),
        name="out_proj_ln_route",
    )(*[a for a, _ in parts], *[w for _, w in parts], x, g.reshape(1, d), b.reshape(1, d), wr_hi, wr_lo,
      b_r.reshape(N_EXPERTS, 1).astype(F32))


def _na_kernel(q_ref, k_ref, v_ref, cc_ref, o_ref, *, rows):
    i = pl.program_id(2)
    low = _lane_is_low()

    for g0 in range(0, NA_ROW_BLOCK, NA_ROW_GROUP):
        wins, scores = [], []
        for rr in range(g0, g0 + NA_ROW_GROUP):
            r = i * NA_ROW_BLOCK + rr
            rs = jnp.clip(r - NA_ROWS // 2, 0, rows - NA_ROWS)
            var = r - rs
            q = q_ref[rr * GRID_W:(rr + 1) * GRID_W, :]
            kstart = pl.multiple_of(rs * GRID_W, GRID_W)
            kwin = k_ref[pl.ds(kstart, NA_ROWS * GRID_W), :]
            wins.append(kstart)
            for half in range(2):
                qm = jnp.where(low if half == 0 else jnp.logical_not(low), q, jnp.zeros_like(q))
                s = lax.dot_general(qm, kwin, (((1,), (1,)), ((), ())), preferred_element_type=F32)
                scores.append(s + cc_ref[half, var])
        probs = []
        for s in scores:
            p = jnp.exp2(s - jnp.max(s, axis=-1, keepdims=True))
            probs.append((p.astype(BF16), jnp.sum(p, axis=-1, keepdims=True)))
        for u, rr in enumerate(range(g0, g0 + NA_ROW_GROUP)):
            vwin = v_ref[pl.ds(wins[u], NA_ROWS * GRID_W), :]
            outs = [jnp.dot(p, vwin, preferred_element_type=F32) / l for p, l in probs[2 * u:2 * u + 2]]
            o_ref[rr * GRID_W:(rr + 1) * GRID_W, :] = jnp.where(low, outs[0], outs[1]).astype(o_ref.dtype)


def _na_attention(proj, cc, row0, b, s):
    rows = s // GRID_W
    assert rows >= NA_ROWS and rows % NA_ROW_BLOCK == 0 and row0 % s == 0
    tq = NA_ROW_BLOCK * GRID_W
    n_slab = NA_HEADS * HEAD_DIM // LANES
    qb0, sb0 = row0 // tq, row0 // s
    return pl.pallas_call(
        functools.partial(_na_kernel, rows=rows),
        grid=(b, n_slab, s // tq),
        in_specs=[pl.BlockSpec((tq, LANES), lambda bi, j, i: (qb0 + bi * (s // tq) + i, j)),
                  pl.BlockSpec((s, LANES), lambda bi, j, i: (sb0 + bi, n_slab + j)),
                  pl.BlockSpec((s, LANES), lambda bi, j, i: (sb0 + bi, 2 * n_slab + j)),
                  pl.BlockSpec((2, NA_ROWS, GRID_W, NA_ROWS * GRID_W), lambda bi, j, i: (j, 0, 0, 0))],
        out_specs=pl.BlockSpec((tq, LANES), lambda bi, j, i: (bi * (s // tq) + i, j)),
        out_shape=jax.ShapeDtypeStruct((b * s, n_slab * LANES), BF16),
        compiler_params=_params("parallel", "parallel", "parallel"),
        name="na_attn",
    )(proj, proj, proj, cc)


def _na_bias_table(rpb):
    var = np.arange(NA_ROWS)[:, None]
    j = np.arange(NA_ROWS)[None, :]
    qc = np.arange(GRID_W)[:, None]
    kc = np.arange(GRID_W)[None, :]
    row_sel = ((j - var + NA_ROWS - 1)[..., None] == np.arange(2 * NA_ROWS - 1)).astype(np.float32)
    col_sel = (np.clip(kc - qc + NA_COLS - 1, 0, 2 * NA_COLS - 2)[..., None]
               == np.arange(2 * NA_COLS - 1)).astype(np.float32)
    win = np.clip(qc - NA_COLS // 2, 0, GRID_W - NA_COLS)
    valid = (kc >= win) & (kc < win + NA_COLS)
    t = jnp.einsum("hab,vja,qkb->hvqjk", rpb.astype(F32), row_sel, col_sel, precision=lax.Precision.HIGHEST)
    t = jnp.where(valid[None, None, :, None, :], t * LOG2E, NEG)
    return t.reshape(rpb.shape[0], NA_ROWS, GRID_W, NA_ROWS * GRID_W)


def _gqa_kernel(q_ref, k_ref, vt_ref, o_ref, acc_ref, st_ref, *, n_kt):
    assert n_kt % 2 == 0
    low = _lane_is_low()
    q = q_ref[...]
    tq = q.shape[0]
    zero = jnp.zeros_like(q)
    qs = jnp.concatenate([jnp.where(low, q, zero), jnp.where(low, zero, q)], axis=0)
    acc_ref[...] = jnp.zeros_like(acc_ref)

    def scores(kt, slot):
        start = pl.multiple_of(kt * GQA_K_TILE, GQA_K_TILE)
        kb = k_ref[pl.ds(start, GQA_K_TILE), :]
        st_ref[slot] = lax.dot_general(kb, qs, (((1,), (1,)), ((), ())), preferred_element_type=F32)

    def softmax_pv(kt, slot, m_old):
        st = st_ref[slot]
        m_new = jnp.maximum(m_old, jnp.max(st, axis=0, keepdims=True))
        alpha = jnp.exp2(m_old - m_new)
        pt = jnp.exp2(st - m_new).astype(BF16)
        acc_ref[...] = alpha * acc_ref[...] + jnp.dot(vt_ref[kt], pt, preferred_element_type=F32)
        return m_new

    def step(i, m):
        kt = 2 * i
        scores(kt + 1, 1)
        m = softmax_pv(kt, 0, m)
        scores(jnp.minimum(kt + 2, n_kt - 1), 0)
        return softmax_pv(kt + 1, 1, m)

    scores(0, 0)
    lax.fori_loop(0, n_kt // 2, step, jnp.full((1, 2 * tq), -jnp.inf, F32))
    out_t = acc_ref[:LANES, :] / acc_ref[LANES:LANES + 1, :]
    o_t = jnp.concatenate([out_t[:HEAD_DIM, :tq], out_t[HEAD_DIM:, tq:]], axis=0)
    o_ref[...] = o_t.T.astype(o_ref.dtype)


def _gqa_attention(proj, row0, b, s):
    tq, tk = GQA_Q_TILE, GQA_K_TILE
    assert s % tk == 0 and row0 % s == 0
    n_slab = GQA_Q_HEADS * HEAD_DIM // LANES
    q_col = 3 * NA_HEADS * HEAD_DIM // LANES
    k_col = q_col + n_slab
    v0 = (k_col + 1) * LANES
    qb0, sb0 = row0 // tq, row0 // s
    v_t = proj[row0:row0 + b * s, v0:v0 + LANES].reshape(b, s // tk, tk, LANES).transpose(0, 1, 3, 2)
    ones_rows = jnp.zeros((b, s // tk, GQA_DEN_ROWS, tk), BF16).at[:, :, 0, :].set(1.0)
    v_t = jnp.concatenate([v_t, ones_rows], axis=2)
    vt_rows = LANES + GQA_DEN_ROWS
    return pl.pallas_call(
        functools.partial(_gqa_kernel, n_kt=s // tk),
        grid=(b, n_slab, s // tq),
        in_specs=[pl.BlockSpec((tq, LANES), lambda bi, j, i: (qb0 + bi * (s // tq) + i, q_col + j)),
                  pl.BlockSpec((s, LANES), lambda bi, j, i: (sb0 + bi, k_col)),
                  pl.BlockSpec((None, s // tk, vt_rows, tk), lambda bi, j, i: (bi, 0, 0, 0))],
        out_specs=pl.BlockSpec((tq, LANES), lambda bi, j, i: (bi * (s // tq) + i, j)),
        out_shape=jax.ShapeDtypeStruct((b * s, n_slab * LANES), BF16),
        scratch_shapes=[pltpu.VMEM((vt_rows, 2 * tq), F32), pltpu.VMEM((2, tk, 2 * tq), F32)],
        compiler_params=_params("parallel", "parallel", "parallel"),
        name="gqa_attn",
    )(proj, proj, v_t)


def _dil_kernel(q_ref, k_ref, v_ref, o_ref, m_ref, l_ref, acc_ref, *, seq):
    tile = q_ref.shape[0]
    t0 = pl.program_id(2) * tile
    low = _lane_is_low()
    sub = WIN_Q_TILE
    n_sub = tile // sub

    for bi, (_, r) in enumerate(DIL_BRANCHES):
        length = seq // r
        win = min(sub + 2 * WIN_RADIUS, length)
        offset = lax.broadcasted_iota(I32, (sub, win), 1) - lax.broadcasted_iota(I32, (sub, win), 0)

        def sub_blocks(g, carry, r=r, length=length, win=win, first=(bi == 0), offset=offset):
            rows, scores, stats = [], [], []
            for u in range(DIL_GROUP):
                n = g * DIL_GROUP + u
                c = n % r
                blk = n // r
                u0 = t0 // r + blk * sub
                ks = jnp.clip(u0 - WIN_RADIUS, 0, length - win)
                if r == 1:
                    q_rows = pl.ds(pl.multiple_of(n * sub, sub), sub)
                    k_rows = pl.ds(pl.multiple_of(ks, WIN_RADIUS), win)
                else:
                    q_rows = pl.ds(c + r * blk * sub, sub, stride=r)
                    k_rows = pl.ds(c + r * ks, win, stride=r)
                rows.append((q_rows, k_rows))
                q = q_ref[q_rows, :].astype(BF16)
                kwin = k_ref[k_rows, :].astype(BF16)
                valid = jnp.abs(offset + (ks - u0)) <= WIN_RADIUS
                for half in range(2):
                    qm = jnp.where(low if half == 0 else jnp.logical_not(low), q, jnp.zeros_like(q))
                    s = lax.dot_general(qm, kwin, (((1,), (1,)), ((), ())), preferred_element_type=F32)
                    scores.append(jnp.where(valid, s, NEG))
            for s in scores:
                m = jnp.max(s, axis=-1, keepdims=True)
                p = jnp.exp2(s - m)
                stats.append((m, jnp.sum(p, axis=-1, keepdims=True), p.astype(BF16)))
            for u, (q_rows, k_rows) in enumerate(rows):
                vwin = v_ref[k_rows, :].astype(BF16)
                (m0, l0, p0), (m1, l1, p1) = stats[2 * u:2 * u + 2]
                m_b = jnp.where(low, m0, m1)
                l_b = jnp.where(low, l0, l1)
                pv_b = jnp.where(low, jnp.dot(p0, vwin, preferred_element_type=F32),
                                 jnp.dot(p1, vwin, preferred_element_type=F32))
                if first:
                    m_ref[q_rows, :] = m_b
                    l_ref[q_rows, :] = l_b
                    acc_ref[q_rows, :] = pv_b
                else:
                    m_old = m_ref[q_rows, :]
                    m_new = jnp.maximum(m_old, m_b)
                    a_old = jnp.exp2(m_old - m_new)
                    a_b = jnp.exp2(m_b - m_new)
                    l_ref[q_rows, :] = a_old * l_ref[q_rows, :] + a_b * l_b
                    acc_ref[q_rows, :] = a_old * acc_ref[q_rows, :] + a_b * pv_b
                    m_ref[q_rows, :] = m_new
            return carry

        lax.fori_loop(0, n_sub // DIL_GROUP, sub_blocks, 0)

    o_ref[...] = (acc_ref[...] / l_ref[...]).astype(o_ref.dtype)


def _dilated_attention(qkv, row0, b, s):
    tile = 16 * WIN_Q_TILE
    assert s % tile == 0 and row0 % s == 0
    n_slab = DIL_HEADS * HEAD_DIM // LANES
    qb0, sb0 = row0 // tile, row0 // s
    state = pltpu.VMEM((tile, LANES), F32)
    return pl.pallas_call(
        functools.partial(_dil_kernel, seq=s),
        grid=(b, n_slab, s // tile),
        in_specs=[pl.BlockSpec((tile, LANES), lambda bi, j, i: (qb0 + bi * (s // tile) + i, j)),
                  pl.BlockSpec((s, LANES), lambda bi, j, i: (sb0 + bi, n_slab + j)),
                  pl.BlockSpec((s, LANES), lambda bi, j, i: (sb0 + bi, 2 * n_slab + j))],
        out_specs=pl.BlockSpec((tile, LANES), lambda bi, j, i: (bi * (s // tile) + i, j)),
        out_shape=jax.ShapeDtypeStruct((b * s, n_slab * LANES), BF16),
        scratch_shapes=[state, state, state],
        compiler_params=_params("parallel", "parallel", "parallel"),
        name="dil_attn",
    )(qkv, qkv, qkv)


def _expert_kernel(blk_exp_ref, n_used_ref, x_ref, w1_ref, b1_ref, w2_ref, b2_ref, o_ref):
    blk = pl.program_id(0)

    @pl.when(blk < n_used_ref[0])
    def _():
        x_lo, x_hi = _unpack_bf16_pairs(x_ref[...])
        x = jnp.concatenate([x_lo.astype(BF16), x_hi.astype(BF16)], axis=1)
        h = jnp.dot(x, w1_ref[...], preferred_element_type=F32) + b1_ref[...]
        g = jnp.minimum(h[:, :D_FF], SWIGLU_LIMIT)
        u = jnp.clip(h[:, D_FF:], -SWIGLU_LIMIT, SWIGLU_LIMIT)
        act = g * jax.nn.sigmoid(SWIGLU_ALPHA * g) * (u + 1.0)
        y = jnp.dot(act.astype(BF16), w2_ref[...], preferred_element_type=F32) + b2_ref[...]
        o_ref[...] = _pack_bf16_pairs(y)

    @pl.when(blk >= n_used_ref[0])
    def _():
        o_ref[...] = jnp.zeros_like(o_ref)


def _experts(xs, blk_exp, n_used, layer, w1, b1, w2, b2):
    n_rows = xs.shape[0]
    d = D_MODEL
    bm = EXPERT_ROWS
    f2 = w1.shape[3]
    wmap = lambda i, be, nu: (layer, be[i], 0, 0)
    return pl.pallas_call(
        _expert_kernel,
        grid_spec=pltpu.PrefetchScalarGridSpec(
            num_scalar_prefetch=2,
            grid=(n_rows // bm,),
            in_specs=[pl.BlockSpec((bm, d // 2), lambda i, be, nu: (i, 0)),
                      pl.BlockSpec((None, None, d, f2), wmap),
                      pl.BlockSpec((None, None, 1, f2), wmap),
                      pl.BlockSpec((None, None, f2 // 2, d), wmap),
                      pl.BlockSpec((None, None, 1, d), wmap)],
            out_specs=pl.BlockSpec((bm, d // 2), lambda i, be, nu: (i, 0)),
        ),
        out_shape=jax.ShapeDtypeStruct((n_rows, d // 2), jnp.uint32),
        compiler_params=_params("arbitrary"),
        name="experts",
    )(blk_exp, n_used, xs, w1, b1, w2, b2)


def _combine_ln_kernel(y_ref, gate_ref, x_ref, g_ref, b_ref, o_ref):
    gate = gate_ref[...]
    halves = [_unpack_bf16_pairs(y_ref[k]) for k in range(TOP_K)]
    y = jnp.concatenate([functools.reduce(jnp.add, [gate[:, k:k + 1] * halves[k][h] for k in range(TOP_K)])
                         for h in range(2)], axis=1)
    o_ref[...] = _layer_norm_rows(DN_ALPHA * x_ref[...] + y, g_ref[...], b_ref[...])


def _combine_ln(yg, gate, x, g, b, row0=0, n_rows=None):
    n, d = x.shape
    n_rows = n if n_rows is None else n_rows
    tm = COMBINE_TILE
    assert row0 % tm == 0 and n_rows % tm == 0
    blk0 = row0 // tm
    row = lambda i: (blk0 + i, 0)
    fixed = lambda i: (0, 0)
    return pl.pallas_call(
        _combine_ln_kernel,
        grid=(n_rows // tm,),
        in_specs=[pl.BlockSpec((TOP_K, tm, d // 2), lambda i: (0, blk0 + i, 0)), pl.BlockSpec((tm, TOP_K), row),
                  pl.BlockSpec((tm, d), row), pl.BlockSpec((1, d), fixed), pl.BlockSpec((1, d), fixed)],
        out_specs=pl.BlockSpec((tm, d), lambda i: (i, 0)),
        out_shape=jax.ShapeDtypeStruct((n_rows, d), F32),
        compiler_params=_params("parallel"),
        name="combine_ln",
    )(yg, gate, x, g.reshape(1, d), b.reshape(1, d))


def _sc_mesh():
    return plsc.VectorSubcoreMesh(core_axis_name="core", subcore_axis_name="subcore")


def _sc_scatter_rows(x, dest, n_out):
    n, d = x.shape
    mesh = _sc_mesh()
    per_worker = n // (mesh.num_cores * mesh.num_subcores)
    n_sub = SC_INDEX_WINDOW // SC_ROW_WINDOW
    assert per_worker % SC_INDEX_WINDOW == 0 and n_sub >= 2

    @functools.partial(
        pl.kernel, out_type=jax.ShapeDtypeStruct((n_out, d), x.dtype), mesh=mesh,
        scratch_types=[pltpu.VMEM((TOP_K, SC_INDEX_WINDOW), I32), pltpu.VMEM((2, SC_ROW_WINDOW, d), x.dtype),
                       pltpu.SemaphoreType.DMA((2,)), pltpu.SemaphoreType.DMA((2,))])
    def scatter(x_hbm, i_hbm, o_hbm, idx_v, buf, sem_r, sem_s):
        wid = lax.axis_index("core") * mesh.num_subcores + lax.axis_index("subcore")

        @pl.loop(0, per_worker // SC_INDEX_WINDOW)
        def _(it):
            base = wid * per_worker + it * SC_INDEX_WINDOW
            for k in range(TOP_K):
                pltpu.sync_copy(i_hbm.at[k, pl.ds(base, SC_INDEX_WINDOW)], idx_v.at[k])

            def read(j):
                rows = pl.ds(base + j * SC_ROW_WINDOW, SC_ROW_WINDOW)
                return pltpu.make_async_copy(x_hbm.at[rows], buf.at[j % 2], sem_r.at[j % 2])

            def send(j, k):
                rows = idx_v.at[k, pl.ds(j * SC_ROW_WINDOW, SC_ROW_WINDOW)]
                return pltpu.make_async_copy(buf.at[j % 2], o_hbm.at[rows], sem_s.at[j % 2])

            read(0).start()
            for j in range(n_sub):
                read(j).wait()
                for k in range(TOP_K):
                    send(j, k).start()
                if j + 1 < n_sub:
                    if j >= 1:
                        for k in range(TOP_K):
                            send(j - 1, k).wait()
                    read(j + 1).start()
            for j in (n_sub - 2, n_sub - 1):
                for k in range(TOP_K):
                    send(j, k).wait()

    return scatter(x, dest)


def _sc_gather_rows(table, indices):
    num = indices.shape[0]
    d = table.shape[1]
    mesh = _sc_mesh()
    per_worker = num // (mesh.num_cores * mesh.num_subcores)
    n_sub = SC_INDEX_WINDOW // SC_ROW_WINDOW
    assert per_worker % SC_INDEX_WINDOW == 0 and n_sub >= 2

    @functools.partial(
        pl.kernel, out_type=jax.ShapeDtypeStruct((num, d), table.dtype), mesh=mesh,
        scratch_types=[pltpu.VMEM((SC_INDEX_WINDOW,), I32), pltpu.VMEM((2, SC_ROW_WINDOW, d), table.dtype),
                       pltpu.SemaphoreType.DMA((2,)), pltpu.SemaphoreType.DMA((2,))])
    def gather(x_hbm, i_hbm, o_hbm, idx_v, buf, sem_g, sem_w):
        wid = lax.axis_index("core") * mesh.num_subcores + lax.axis_index("subcore")

        @pl.loop(0, per_worker // SC_INDEX_WINDOW)
        def _(it):
            base = wid * per_worker + it * SC_INDEX_WINDOW
            pltpu.sync_copy(i_hbm.at[pl.ds(base, SC_INDEX_WINDOW)], idx_v)

            def fetch(j):
                rows = idx_v.at[pl.ds(j * SC_ROW_WINDOW, SC_ROW_WINDOW)]
                return pltpu.make_async_copy(x_hbm.at[rows], buf.at[j % 2], sem_g.at[j % 2])

            def write(j):
                rows = pl.ds(base + j * SC_ROW_WINDOW, SC_ROW_WINDOW)
                return pltpu.make_async_copy(buf.at[j % 2], o_hbm.at[rows], sem_w.at[j % 2])

            fetch(0).start()
            for j in range(n_sub):
                fetch(j).wait()
                write(j).start()
                if j + 1 < n_sub:
                    if j >= 1:
                        write(j - 1).wait()
                    fetch(j + 1).start()
            write(n_sub - 2).wait()
            write(n_sub - 1).wait()

    return gather(table, indices)


def _moe(x_packed, idx, gate, rank, counts, layer, w1, b1, w2, b2):
    n = x_packed.shape[0]
    d = D_MODEL
    bm = EXPERT_ROWS
    counts = counts[:, 0]
    padded = (counts + bm - 1) // bm * bm
    pad_ends = jnp.cumsum(padded)
    pad_starts = pad_ends - padded
    hot = idx[:, None, :] == jnp.arange(N_EXPERTS, dtype=I32)[None, :, None]
    dest = jnp.sum(jnp.where(hot, pad_starts[None, :, None], 0), axis=1) + rank
    n_blocks = n * TOP_K // bm + N_EXPERTS
    blk_start = jnp.arange(n_blocks, dtype=I32) * bm
    blk_exp = jnp.minimum(jnp.sum(blk_start[:, None] >= pad_ends[None, :], axis=1), N_EXPERTS - 1).astype(I32)
    n_used = (pad_ends[-1:] // bm).astype(I32)
    xs = _sc_scatter_rows(x_packed, dest, n_blocks * bm)
    ys = _experts(xs, blk_exp, n_used, layer, w1, b1, w2, b2)
    yg = _sc_gather_rows(ys, dest.reshape(-1)).reshape(TOP_K, n, d // 2)
    return yg, gate.T


def _positions(groups):
    return jnp.concatenate([jnp.tile(jnp.arange(s), b) for b, s in groups])


def _axial_tables(groups):
    t = _positions(groups)
    n = HEAD_DIM // 4
    inv = AXIAL_THETA ** (-jnp.arange(n, dtype=F32) / n)
    ar = (t // GRID_W).astype(F32)[:, None] * inv
    ac = (t % GRID_W).astype(F32)[:, None] * inv
    z = jnp.zeros_like(ar)
    cr, sr, cc, sc = jnp.cos(ar), jnp.sin(ar), jnp.cos(ac), jnp.sin(ac)
    c = jnp.concatenate([cr, cr, cc, cc], axis=-1)
    s1 = jnp.concatenate([z, sr, z, sc], axis=-1)
    s2 = jnp.concatenate([-sr, z, -sc, z], axis=-1)
    return tuple(jnp.tile(a, (1, 2)) for a in (c, s1, s2))


def _rope_tables(groups):
    t = _positions(groups)
    n = ROPE_DIMS // 2
    inv = ROPE_THETA ** (-jnp.arange(n, dtype=F32) / n)
    ang = t.astype(F32)[:, None] * inv
    c, s = jnp.cos(ang), jnp.sin(ang)
    z = jnp.zeros_like(c)
    rest = HEAD_DIM - ROPE_DIMS
    pad1 = jnp.ones((t.shape[0], rest), F32)
    pad0 = jnp.zeros((t.shape[0], rest), F32)
    cc = jnp.concatenate([c, c, pad1], axis=-1)
    s1 = jnp.concatenate([z, s, pad0], axis=-1)
    s2 = jnp.concatenate([-s, z, pad0], axis=-1)
    return tuple(jnp.tile(a, (1, 2)) for a in (cc, s1, s2))


def _gqa_head_order():
    g = GQA_Q_HEADS // GQA_KV_HEADS
    return [h for j in range(g) for h in (j, g + j)]


def _mixer_even(x, groups, tabs, w_in, rpb, q_gain, k_gain, w_out):
    hd = HEAD_DIM
    na_w = NA_HEADS * hd
    order = _gqa_head_order()
    q0 = 3 * na_w
    q_cols = np.concatenate([q0 + h * hd + np.arange(hd) for h in order])
    w_in_p = jnp.concatenate([w_in[:, :q0], w_in[:, q_cols], w_in[:, q0 + GQA_Q_HEADS * hd:]], axis=1).astype(BF16)
    out_rows = np.concatenate([na_w + h * hd + np.arange(hd) for h in order])
    w_out_p = jnp.concatenate([w_out[:na_w], w_out[out_rows]], axis=0).astype(BF16)
    n_na = na_w // LANES
    n_q = GQA_Q_HEADS * hd // LANES
    modes = ([("plain", Q_SCALE, 0)] * n_na + [("plain", 1.0, 0)] * (2 * n_na)
             + [("norm_rope", Q_SCALE, 0)] * n_q + [("norm_rope", 1.0, 1)] + [("plain", 1.0, 0)])
    gains = jnp.stack([jnp.tile(q_gain.astype(F32), 2), jnp.tile(k_gain.astype(F32), 2)])
    proj = _project(x, w_in_p, tabs, gains, modes, HEAD_DIM // 4)
    cc = _na_bias_table(rpb)
    ya, yb = [], []
    row0 = 0
    for b, s in groups:
        ya.append(_na_attention(proj, cc, row0, b, s))
        yb.append(_gqa_attention(proj, row0, b, s))
        row0 += b * s
    return [(jnp.concatenate(ya, axis=0), w_out_p[:na_w]), (jnp.concatenate(yb, axis=0), w_out_p[na_w:])]


def _mixer_odd(x, groups, tabs, w_in, w_out):
    n_slab = DIL_HEADS * HEAD_DIM // LANES
    modes = [("rope", Q_SCALE, 0)] * n_slab + [("rope", 1.0, 0)] * n_slab + [("plain", 1.0, 0)] * n_slab
    gains = jnp.ones((1, LANES), F32)
    qkv = _project(x, w_in.astype(BF16), tabs, gains, modes, ROPE_DIMS // 2, out_dtype=F32)
    parts, row0 = [], 0
    for b, s in groups:
        parts.append(_dilated_attention(qkv, row0, b, s))
        row0 += b * s
    return [(jnp.concatenate(parts, axis=0), w_out.astype(BF16))]


def _trunk(xs, w_in_even, rpb_a, q_gain_b, k_gain_b, w_out_even, w_in_odd, w_out_odd,
           ln1_g, ln1_b, ln2_g, ln2_b, router_w, router_b, moe_w1, moe_b1, moe_w2, moe_b2):
    groups = [(x.shape[0], x.shape[1]) for x in xs]
    x = jnp.concatenate([t.reshape(-1, D_MODEL) for t in xs], axis=0).astype(F32)
    tabs_even = _axial_tables(groups)
    tabs_odd = _rope_tables(groups)
    b1 = moe_b1.astype(F32)[:, :, None, :]
    b2 = moe_b2.astype(F32)[:, :, None, :]
    for l in range(DEPTH):
        i = l // 2
        if l % 2 == 0:
            parts = _mixer_even(x, groups, tabs_even, w_in_even[i], rpb_a[i], q_gain_b[i], k_gain_b[i],
                                w_out_even[i])
        else:
            parts = _mixer_odd(x, groups, tabs_odd, w_in_odd[i], w_out_odd[i])
        x, x_packed, idx, gate, rank, counts = _outproj_ln_route(parts, x, ln1_g[l], ln1_b[l], router_w[l],
                                                                 router_b[l])
        yg, gate = _moe(x_packed, idx, gate, rank, counts, l, moe_w1, b1, moe_w2, b2)
        if l + 1 < DEPTH:
            x = _combine_ln(yg, gate, x, ln2_g[l], ln2_b[l])
    outs, row0 = [], 0
    for t in xs:
        n = t.shape[0] * t.shape[1]
        outs.append(_combine_ln(yg, gate, x, ln2_g[-1], ln2_b[-1], row0, n).reshape(t.shape))
        row0 += n
    return tuple(outs)


def kernel(x_prompt, x_sample, w_in_even, rpb_a, q_gain_b, k_gain_b, w_out_even, w_in_odd, w_out_odd, ln1_g, ln1_b,
           ln2_g, ln2_b, router_w, router_b, moe_w1, moe_b1, moe_w2, moe_b2):
    params = (w_in_even, rpb_a, q_gain_b, k_gain_b, w_out_even, w_in_odd, w_out_odd,
              ln1_g, ln1_b, ln2_g, ln2_b, router_w, router_b, moe_w1.astype(BF16), moe_b1, moe_w2.astype(BF16), moe_b2)
    (y_prompt,) = _trunk((x_prompt,), *params)
    (y_sample,) = _trunk((x_sample,), *params)
    return y_prompt, y_sample
```

```python
import functools

import jax
import jax.numpy as jnp
import numpy as np
from jax import lax
from jax.experimental import pallas as pl
from jax.experimental.pallas import tpu as pltpu
from jax.experimental.pallas import tpu_sc as plsc

F32 = jnp.float32
BF16 = jnp.bfloat16
I32 = jnp.int32

D_MODEL = 1024
DEPTH = 4
HEAD_DIM = 64
GRID_W = 64
NA_HEADS = 8
NA_ROWS = 8
NA_COLS = 16
GQA_Q_HEADS = 8
GQA_KV_HEADS = 2
AXIAL_THETA = 10000.0
QK_NORM_EPS = 1e-6
DIL_HEADS = 16
DIL_BRANCHES = ((128, 1), (512, 4), (2048, 16))
ROPE_THETA = 500000.0
ROPE_DIMS = HEAD_DIM // 4
N_EXPERTS = 32
TOP_K = 4
D_FF = D_MODEL
SWIGLU_LIMIT = 7.0
SWIGLU_ALPHA = 1.702
DN_ALPHA = (2 * DEPTH) ** 0.25
LN_EPS = 1e-5
LOG2E = 1.4426950408889634
Q_SCALE = HEAD_DIM ** -0.5 * LOG2E

LANES = 128
NEG = -1e30
VMEM_LIMIT = 56 * 1024 * 1024
TOKEN_TILE = 512
COMBINE_TILE = 1024
EXPERT_ROWS = 512
NA_ROW_BLOCK = 8
NA_ROW_GROUP = 8
GQA_Q_TILE = 512
GQA_K_TILE = 512
GQA_DEN_ROWS = 16
WIN_Q_TILE = 128
WIN_RADIUS = 64
DIL_GROUP = 8
SC_INDEX_WINDOW = 128
SC_ROW_WINDOW = 64

EVEN_IN = 3 * NA_HEADS * HEAD_DIM + GQA_Q_HEADS * HEAD_DIM + 2 * GQA_KV_HEADS * HEAD_DIM


def _params(*sem):
    return pltpu.CompilerParams(dimension_semantics=sem, vmem_limit_bytes=VMEM_LIMIT)


def _lane_is_low():
    return lax.broadcasted_iota(I32, (1, LANES), 1) < HEAD_DIM


def _proj_kernel(x_ref, w_ref, c_ref, s1_ref, s2_ref, gain_ref, o_ref, *, slab_modes, shift):
    x = x_ref[...].astype(BF16)
    n_out = o_ref.shape[1]
    chunk = 2 * LANES
    if any(m[0] == "norm_rope" for m in slab_modes):
        r = lax.broadcasted_iota(I32, (LANES, LANES), 0) // HEAD_DIM
        c = lax.broadcasted_iota(I32, (LANES, LANES), 1) // HEAD_DIM
        head_mean = jnp.where(r == c, 1.0 / HEAD_DIM, 0.0).astype(BF16)
    for c0 in range(0, n_out, chunk):
        acc = jnp.dot(x, w_ref[:, c0:c0 + chunk], preferred_element_type=F32)
        for s in range(chunk // LANES):
            slab = c0 // LANES + s
            mode, scale, gidx = slab_modes[slab]
            y = acc[:, s * LANES:(s + 1) * LANES]
            if mode == "norm_rope":
                sq = y * y
                hi = sq.astype(BF16)
                lo = (sq - hi.astype(F32)).astype(BF16)
                ms = (jnp.dot(hi, head_mean, preferred_element_type=F32)
                      + jnp.dot(lo, head_mean, preferred_element_type=F32))
                y = y * lax.rsqrt(ms + QK_NORM_EPS) * gain_ref[gidx:gidx + 1, :]
            if mode in ("rope", "norm_rope"):
                y = (y * c_ref[...] + pltpu.roll(y, LANES - shift, 1) * s2_ref[...]
                     + pltpu.roll(y, shift, 1) * s1_ref[...])
            if scale != 1.0:
                y = y * scale
            o_ref[:, slab * LANES:(slab + 1) * LANES] = y.astype(o_ref.dtype)


def _project(x, w, tabs, gains, slab_modes, shift, out_dtype=BF16):
    n, d = x.shape
    m = w.shape[1]
    tm = TOKEN_TILE
    tab_spec = pl.BlockSpec((tm, LANES), lambda i: (i, 0))
    return pl.pallas_call(
        functools.partial(_proj_kernel, slab_modes=tuple(slab_modes), shift=shift),
        grid=(n // tm,),
        in_specs=[pl.BlockSpec((tm, d), lambda i: (i, 0)),
                  pl.BlockSpec((d, m), lambda i: (0, 0)),
                  tab_spec, tab_spec, tab_spec,
                  pl.BlockSpec(gains.shape, lambda i: (0, 0))],
        out_specs=pl.BlockSpec((tm, m), lambda i: (i, 0)),
        out_shape=jax.ShapeDtypeStruct((n, m), out_dtype),
        compiler_params=_params("parallel"),
        name="in_proj",
    )(x, w, tabs[0], tabs[1], tabs[2], gains)


def _layer_norm_rows(z, g, b):
    mu = jnp.mean(z, axis=-1, keepdims=True)
    zc = z - mu
    var = jnp.mean(zc * zc, axis=-1, keepdims=True)
    return zc * lax.rsqrt(var + LN_EPS) * g + b


def _pack_bf16_pairs(v):
    half = v.shape[1] // 2
    bits = pltpu.bitcast(v.astype(BF16).astype(F32), jnp.uint32)
    return (bits[:, :half] >> 16) | bits[:, half:]


def _unpack_bf16_pairs(w):
    lo = pltpu.bitcast(w << 16, F32)
    hi = pltpu.bitcast(w & jnp.uint32(0xFFFF0000), F32)
    return lo, hi


def _route_tile(x, wh_ref, wl_ref, rb_ref, base_ref, idx_ref, gate_ref, rank_ref, cnt_ref):
    xh = x.astype(BF16)
    xl = (x - xh.astype(F32)).astype(BF16)
    nt = (((1,), (1,)), ((), ()))
    logits = (lax.dot_general(wh_ref[...], xh, nt, preferred_element_type=F32)
              + lax.dot_general(wl_ref[...], xh, nt, preferred_element_type=F32)
              + lax.dot_general(wh_ref[...], xl, nt, preferred_element_type=F32)) + rb_ref[...]
    tm = x.shape[0]
    eid = lax.broadcasted_iota(I32, (N_EXPERTS, tm), 0)
    vals = logits
    top_v, top_i, hots = [], [], []
    for _ in range(TOP_K):
        m = jnp.max(vals, axis=0, keepdims=True)
        idx = jnp.min(jnp.where(vals == m, eid, N_EXPERTS), axis=0, keepdims=True)
        hot = eid == idx
        top_v.append(m)
        top_i.append(idx)
        hots.append(hot)
        vals = jnp.where(hot, -jnp.inf, vals)
    es = [jnp.exp(v - top_v[0]) for v in top_v]
    den = functools.reduce(jnp.add, es)
    chosen = functools.reduce(jnp.logical_or, hots)
    before = (lax.broadcasted_iota(I32, (tm, tm), 0) < lax.broadcasted_iota(I32, (tm, tm), 1))
    prefix = jnp.dot(chosen.astype(BF16), before.astype(BF16), preferred_element_type=F32) + base_ref[...]
    for k in range(TOP_K):
        idx_ref[k:k + 1, :] = top_i[k]
        gate_ref[k:k + 1, :] = es[k] / den
        rank_ref[k:k + 1, :] = jnp.sum(jnp.where(hots[k], prefix, 0.0), axis=0, keepdims=True).astype(I32)
    base_ref[...] = base_ref[...] + jnp.sum(chosen.astype(F32), axis=1, keepdims=True)
    cnt_ref[...] = base_ref[...].astype(I32)


def _outproj_ln_kernel(*refs, n_parts):
    a_refs, w_refs = refs[:n_parts], refs[n_parts:2 * n_parts]
    x_ref, g_ref, b_ref, wh_ref, wl_ref, rb_ref = refs[2 * n_parts:2 * n_parts + 6]
    o_ref, p_ref, idx_ref, gate_ref, rank_ref, cnt_ref, base_ref = refs[2 * n_parts + 6:]

    @pl.when(pl.program_id(0) == 0)
    def _():
        base_ref[...] = jnp.zeros_like(base_ref)

    y = functools.reduce(jnp.add, [jnp.dot(a[...], w[...], preferred_element_type=F32)
                                   for a, w in zip(a_refs, w_refs)])
    out = _layer_norm_rows(DN_ALPHA * x_ref[...] + y, g_ref[...], b_ref[...])
    o_ref[...] = out
    p_ref[...] = _pack_bf16_pairs(out)
    _route_tile(out, wh_ref, wl_ref, rb_ref, base_ref, idx_ref, gate_ref, rank_ref, cnt_ref)


def _outproj_ln_route(parts, x, g, b, w_r, b_r):
    n, d = x.shape
    tm = TOKEN_TILE
    row = lambda i: (i, 0)
    fixed = lambda i: (0, 0)
    tok = pl.BlockSpec((TOP_K, tm), lambda i: (0, i))
    wr_t = w_r.T
    wr_hi = wr_t.astype(BF16)
    wr_lo = (wr_t - wr_hi.astype(F32)).astype(BF16)
    expert_vec = pl.BlockSpec((N_EXPERTS, 1), fixed)
    return pl.pallas_call(
        functools.partial(_outproj_ln_kernel, n_parts=len(parts)),
        grid=(n // tm,),
        in_specs=([pl.BlockSpec((tm, a.shape[1]), row) for a, _ in parts]
                  + [pl.BlockSpec(w.shape, fixed) for _, w in parts]
                  + [pl.BlockSpec((tm, d), row), pl.BlockSpec((1, d), fixed), pl.BlockSpec((1, d), fixed),
                     pl.BlockSpec((N_EXPERTS, d), fixed), pl.BlockSpec((N_EXPERTS, d), fixed), expert_vec]),
        out_specs=[pl.BlockSpec((tm, d), row), pl.BlockSpec((tm, d // 2), row), tok, tok, tok, expert_vec],
        out_shape=[jax.ShapeDtypeStruct((n, d), F32), jax.ShapeDtypeStruct((n, d // 2), jnp.uint32),
                   jax.ShapeDtypeStruct((TOP_K, n), I32), jax.ShapeDtypeStruct((TOP_K, n), F32),
                   jax.ShapeDtypeStruct((TOP_K, n), I32), jax.ShapeDtypeStruct((N_EXPERTS, 1), I32)],
        scratch_shapes=[pltpu.VMEM((N_EXPERTS, 1), F32)],
        compiler_params=_params("arbitrary"),
        name="out_proj_ln_route",
    )(*[a for a, _ in parts], *[w for _, w in parts], x, g.reshape(1, d), b.reshape(1, d), wr_hi, wr_lo,
      b_r.reshape(N_EXPERTS, 1).astype(F32))


def _na_kernel(q_ref, k_ref, v_ref, cc_ref, o_ref, *, rows):
    i = pl.program_id(2)
    low = _lane_is_low()

    for g0 in range(0, NA_ROW_BLOCK, NA_ROW_GROUP):
        wins, scores = [], []
        for rr in range(g0, g0 + NA_ROW_GROUP):
            r = i * NA_ROW_BLOCK + rr
            rs = jnp.clip(r - NA_ROWS // 2, 0, rows - NA_ROWS)
            var = r - rs
            q = q_ref[rr * GRID_W:(rr + 1) * GRID_W, :]
            kstart = pl.multiple_of(rs * GRID_W, GRID_W)
            kwin = k_ref[pl.ds(kstart, NA_ROWS * GRID_W), :]
            wins.append(kstart)
            for half in range(2):
                qm = jnp.where(low if half == 0 else jnp.logical_not(low), q, jnp.zeros_like(q))
                s = lax.dot_general(qm, kwin, (((1,), (1,)), ((), ())), preferred_element_type=F32)
                scores.append(s + cc_ref[half, var])
        probs = []
        for s in scores:
            p = jnp.exp2(s - jnp.max(s, axis=-1, keepdims=True))
            probs.append((p.astype(BF16), jnp.sum(p, axis=-1, keepdims=True)))
        for u, rr in enumerate(range(g0, g0 + NA_ROW_GROUP)):
            vwin = v_ref[pl.ds(wins[u], NA_ROWS * GRID_W), :]
            outs = [jnp.dot(p, vwin, preferred_element_type=F32) / l for p, l in probs[2 * u:2 * u + 2]]
            o_ref[rr * GRID_W:(rr + 1) * GRID_W, :] = jnp.where(low, outs[0], outs[1]).astype(o_ref.dtype)


def _na_attention(proj, cc, row0, b, s):
    rows = s // GRID_W
    assert rows >= NA_ROWS and rows % NA_ROW_BLOCK == 0 and row0 % s == 0
    tq = NA_ROW_BLOCK * GRID_W
    n_slab = NA_HEADS * HEAD_DIM // LANES
    qb0, sb0 = row0 // tq, row0 // s
    return pl.pallas_call(
        functools.partial(_na_kernel, rows=rows),
        grid=(b, n_slab, s // tq),
        in_specs=[pl.BlockSpec((tq, LANES), lambda bi, j, i: (qb0 + bi * (s // tq) + i, j)),
                  pl.BlockSpec((s, LANES), lambda bi, j, i: (sb0 + bi, n_slab + j)),
                  pl.BlockSpec((s, LANES), lambda bi, j, i: (sb0 + bi, 2 * n_slab + j)),
                  pl.BlockSpec((2, NA_ROWS, GRID_W, NA_ROWS * GRID_W), lambda bi, j, i: (j, 0, 0, 0))],
        out_specs=pl.BlockSpec((tq, LANES), lambda bi, j, i: (bi * (s // tq) + i, j)),
        out_shape=jax.ShapeDtypeStruct((b * s, n_slab * LANES), BF16),
        compiler_params=_params("parallel", "parallel", "parallel"),
        name="na_attn",
    )(proj, proj, proj, cc)


def _na_bias_table(rpb):
    var = np.arange(NA_ROWS)[:, None]
    j = np.arange(NA_ROWS)[None, :]
    qc = np.arange(GRID_W)[:, None]
    kc = np.arange(GRID_W)[None, :]
    row_sel = ((j - var + NA_ROWS - 1)[..., None] == np.arange(2 * NA_ROWS - 1)).astype(np.float32)
    col_sel = (np.clip(kc - qc + NA_COLS - 1, 0, 2 * NA_COLS - 2)[..., None]
               == np.arange(2 * NA_COLS - 1)).astype(np.float32)
    win = np.clip(qc - NA_COLS // 2, 0, GRID_W - NA_COLS)
    valid = (kc >= win) & (kc < win + NA_COLS)
    t = jnp.einsum("hab,vja,qkb->hvqjk", rpb.astype(F32), row_sel, col_sel, precision=lax.Precision.HIGHEST)
    t = jnp.where(valid[None, None, :, None, :], t * LOG2E, NEG)
    return t.reshape(rpb.shape[0], NA_ROWS, GRID_W, NA_ROWS * GRID_W)


def _gqa_kernel(q_ref, k_ref, vt_ref, o_ref, acc_ref, st_ref, *, n_kt):
    assert n_kt % 2 == 0
    low = _lane_is_low()
    q = q_ref[...]
    tq = q.shape[0]
    zero = jnp.zeros_like(q)
    qs = jnp.concatenate([jnp.where(low, q, zero), jnp.where(low, zero, q)], axis=0)
    acc_ref[...] = jnp.zeros_like(acc_ref)

    def scores(kt, slot):
        start = pl.multiple_of(kt * GQA_K_TILE, GQA_K_TILE)
        kb = k_ref[pl.ds(start, GQA_K_TILE), :]
        st_ref[slot] = lax.dot_general(kb, qs, (((1,), (1,)), ((), ())), preferred_element_type=F32)

    def softmax_pv(kt, slot, m_old):
        st = st_ref[slot]
        m_new = jnp.maximum(m_old, jnp.max(st, axis=0, keepdims=True))
        alpha = jnp.exp2(m_old - m_new)
        pt = jnp.exp2(st - m_new).astype(BF16)
        acc_ref[...] = alpha * acc_ref[...] + jnp.dot(vt_ref[kt], pt, preferred_element_type=F32)
        return m_new

    def step(i, m):
        kt = 2 * i
        scores(kt + 1, 1)
        m = softmax_pv(kt, 0, m)
        scores(jnp.minimum(kt + 2, n_kt - 1), 0)
        return softmax_pv(kt + 1, 1, m)

    scores(0, 0)
    lax.fori_loop(0, n_kt // 2, step, jnp.full((1, 2 * tq), -jnp.inf, F32))
    out_t = acc_ref[:LANES, :] / acc_ref[LANES:LANES + 1, :]
    o_t = jnp.concatenate([out_t[:HEAD_DIM, :tq], out_t[HEAD_DIM:, tq:]], axis=0)
    o_ref[...] = o_t.T.astype(o_ref.dtype)


def _gqa_attention(proj, row0, b, s):
    tq, tk = GQA_Q_TILE, GQA_K_TILE
    assert s % tk == 0 and row0 % s == 0
    n_slab = GQA_Q_HEADS * HEAD_DIM // LANES
    q_col = 3 * NA_HEADS * HEAD_DIM // LANES
    k_col = q_col + n_slab
    v0 = (k_col + 1) * LANES
    qb0, sb0 = row0 // tq, row0 // s
    v_t = proj[row0:row0 + b * s, v0:v0 + LANES].reshape(b, s // tk, tk, LANES).transpose(0, 1, 3, 2)
    ones_rows = jnp.zeros((b, s // tk, GQA_DEN_ROWS, tk), BF16).at[:, :, 0, :].set(1.0)
    v_t = jnp.concatenate([v_t, ones_rows], axis=2)
    vt_rows = LANES + GQA_DEN_ROWS
    return pl.pallas_call(
        functools.partial(_gqa_kernel, n_kt=s // tk),
        grid=(b, n_slab, s // tq),
        in_specs=[pl.BlockSpec((tq, LANES), lambda bi, j, i: (qb0 + bi * (s // tq) + i, q_col + j)),
                  pl.BlockSpec((s, LANES), lambda bi, j, i: (sb0 + bi, k_col)),
                  pl.BlockSpec((None, s // tk, vt_rows, tk), lambda bi, j, i: (bi, 0, 0, 0))],
        out_specs=pl.BlockSpec((tq, LANES), lambda bi, j, i: (bi * (s // tq) + i, j)),
        out_shape=jax.ShapeDtypeStruct((b * s, n_slab * LANES), BF16),
        scratch_shapes=[pltpu.VMEM((vt_rows, 2 * tq), F32), pltpu.VMEM((2, tk, 2 * tq), F32)],
        compiler_params=_params("parallel", "parallel", "parallel"),
        name="gqa_attn",
    )(proj, proj, v_t)


def _dil_kernel(q_ref, k_ref, v_ref, o_ref, m_ref, l_ref, acc_ref, *kv_refs, seq):
    tile = q_ref.shape[0]
    t0 = pl.program_id(2) * tile
    low = _lane_is_low()
    sub = WIN_Q_TILE
    n_sub = tile // sub

    @pl.when(pl.program_id(2) == 0)
    def _():
        for bi, (_, r) in enumerate(DIL_BRANCHES):
            for c in range(r):
                rows = pl.ds(c, seq // r, stride=r) if r > 1 else pl.ds(0, seq)
                kv_refs[2 * bi][c] = k_ref[rows, :].astype(BF16)
                kv_refs[2 * bi + 1][c] = v_ref[rows, :].astype(BF16)

    for bi, (_, r) in enumerate(DIL_BRANCHES):
        length = seq // r
        win = min(sub + 2 * WIN_RADIUS, length)
        offset = lax.broadcasted_iota(I32, (sub, win), 1) - lax.broadcasted_iota(I32, (sub, win), 0)
        kc_ref, vc_ref = kv_refs[2 * bi], kv_refs[2 * bi + 1]

        def sub_blocks(g, carry, r=r, length=length, win=win, first=(bi == 0), offset=offset, kc_ref=kc_ref,
                       vc_ref=vc_ref):
            rows, scores, stats = [], [], []
            for u in range(DIL_GROUP):
                n = g * DIL_GROUP + u
                c = n % r
                blk = n // r
                u0 = t0 // r + blk * sub
                ks = pl.multiple_of(jnp.clip(u0 - WIN_RADIUS, 0, length - win), WIN_RADIUS)
                if r == 1:
                    q_rows = pl.ds(pl.multiple_of(n * sub, sub), sub)
                else:
                    q_rows = pl.ds(c + r * blk * sub, sub, stride=r)
                rows.append((q_rows, c, ks))
                q = q_ref[q_rows, :].astype(BF16)
                kwin = kc_ref[c, pl.ds(ks, win), :]
                valid = jnp.abs(offset + (ks - u0)) <= WIN_RADIUS
                for half in range(2):
                    qm = jnp.where(low if half == 0 else jnp.logical_not(low), q, jnp.zeros_like(q))
                    s = lax.dot_general(qm, kwin, (((1,), (1,)), ((), ())), preferred_element_type=F32)
                    scores.append(jnp.where(valid, s, NEG))
            for s in scores:
                m = jnp.max(s, axis=-1, keepdims=True)
                p = jnp.exp2(s - m)
                stats.append((m, jnp.sum(p, axis=-1, keepdims=True), p.astype(BF16)))
            for u, (q_rows, c, ks) in enumerate(rows):
                vwin = vc_ref[c, pl.ds(ks, win), :]
                (m0, l0, p0), (m1, l1, p1) = stats[2 * u:2 * u + 2]
                m_b = jnp.where(low, m0, m1)
                l_b = jnp.where(low, l0, l1)
                pv_b = jnp.where(low, jnp.dot(p0, vwin, preferred_element_type=F32),
                                 jnp.dot(p1, vwin, preferred_element_type=F32))
                if first:
                    m_ref[q_rows, :] = m_b
                    l_ref[q_rows, :] = l_b
                    acc_ref[q_rows, :] = pv_b
                else:
                    m_old = m_ref[q_rows, :]
                    m_new = jnp.maximum(m_old, m_b)
                    a_old = jnp.exp2(m_old - m_new)
                    a_b = jnp.exp2(m_b - m_new)
                    l_ref[q_rows, :] = a_old * l_ref[q_rows, :] + a_b * l_b
                    acc_ref[q_rows, :] = a_old * acc_ref[q_rows, :] + a_b * pv_b
                    m_ref[q_rows, :] = m_new
            return carry

        lax.fori_loop(0, n_sub // DIL_GROUP, sub_blocks, 0)

    o_ref[...] = (acc_ref[...] / l_ref[...]).astype(o_ref.dtype)


def _dilated_attention(qkv, row0, b, s):
    tile = 16 * WIN_Q_TILE
    assert s % tile == 0 and row0 % s == 0
    n_slab = DIL_HEADS * HEAD_DIM // LANES
    qb0, sb0 = row0 // tile, row0 // s
    state = pltpu.VMEM((tile, LANES), F32)
    by_class = [pltpu.VMEM((r, s // r, LANES), BF16) for _, r in DIL_BRANCHES for _ in ("k", "v")]
    return pl.pallas_call(
        functools.partial(_dil_kernel, seq=s),
        grid=(b, n_slab, s // tile),
        in_specs=[pl.BlockSpec((tile, LANES), lambda bi, j, i: (qb0 + bi * (s // tile) + i, j)),
                  pl.BlockSpec((s, LANES), lambda bi, j, i: (sb0 + bi, n_slab + j)),
                  pl.BlockSpec((s, LANES), lambda bi, j, i: (sb0 + bi, 2 * n_slab + j))],
        out_specs=pl.BlockSpec((tile, LANES), lambda bi, j, i: (bi * (s // tile) + i, j)),
        out_shape=jax.ShapeDtypeStruct((b * s, n_slab * LANES), BF16),
        scratch_shapes=[state, state, state] + by_class,
        compiler_params=_params("parallel", "parallel", "arbitrary"),
        name="dil_attn",
    )(qkv, qkv, qkv)


def _expert_kernel(blk_exp_ref, n_used_ref, x_ref, w1_ref, b1_ref, w2_ref, b2_ref, o_ref):
    blk = pl.program_id(0)

    @pl.when(blk < n_used_ref[0])
    def _():
        x_lo, x_hi = _unpack_bf16_pairs(x_ref[...])
        x = jnp.concatenate([x_lo.astype(BF16), x_hi.astype(BF16)], axis=1)
        h = jnp.dot(x, w1_ref[...], preferred_element_type=F32) + b1_ref[...]
        g = jnp.minimum(h[:, :D_FF], SWIGLU_LIMIT)
        u = jnp.clip(h[:, D_FF:], -SWIGLU_LIMIT, SWIGLU_LIMIT)
        act = g * jax.nn.sigmoid(SWIGLU_ALPHA * g) * (u + 1.0)
        y = jnp.dot(act.astype(BF16), w2_ref[...], preferred_element_type=F32) + b2_ref[...]
        o_ref[...] = _pack_bf16_pairs(y)

    @pl.when(blk >= n_used_ref[0])
    def _():
        o_ref[...] = jnp.zeros_like(o_ref)


def _experts(xs, blk_exp, n_used, layer, w1, b1, w2, b2):
    n_rows = xs.shape[0]
    d = D_MODEL
    bm = EXPERT_ROWS
    f2 = w1.shape[3]
    wmap = lambda i, be, nu: (layer, be[i], 0, 0)
    return pl.pallas_call(
        _expert_kernel,
        grid_spec=pltpu.PrefetchScalarGridSpec(
            num_scalar_prefetch=2,
            grid=(n_rows // bm,),
            in_specs=[pl.BlockSpec((bm, d // 2), lambda i, be, nu: (i, 0)),
                      pl.BlockSpec((None, None, d, f2), wmap),
                      pl.BlockSpec((None, None, 1, f2), wmap),
                      pl.BlockSpec((None, None, f2 // 2, d), wmap),
                      pl.BlockSpec((None, None, 1, d), wmap)],
            out_specs=pl.BlockSpec((bm, d // 2), lambda i, be, nu: (i, 0)),
        ),
        out_shape=jax.ShapeDtypeStruct((n_rows, d // 2), jnp.uint32),
        compiler_params=_params("arbitrary"),
        name="experts",
    )(blk_exp, n_used, xs, w1, b1, w2, b2)


def _combine_ln_kernel(y_ref, gate_ref, x_ref, g_ref, b_ref, o_ref):
    gate = gate_ref[...]
    halves = [_unpack_bf16_pairs(y_ref[k]) for k in range(TOP_K)]
    y = jnp.concatenate([functools.reduce(jnp.add, [gate[:, k:k + 1] * halves[k][h] for k in range(TOP_K)])
                         for h in range(2)], axis=1)
    o_ref[...] = _layer_norm_rows(DN_ALPHA * x_ref[...] + y, g_ref[...], b_ref[...])


def _combine_ln(yg, gate, x, g, b, row0=0, n_rows=None):
    n, d = x.shape
    n_rows = n if n_rows is None else n_rows
    tm = COMBINE_TILE
    assert row0 % tm == 0 and n_rows % tm == 0
    blk0 = row0 // tm
    row = lambda i: (blk0 + i, 0)
    fixed = lambda i: (0, 0)
    return pl.pallas_call(
        _combine_ln_kernel,
        grid=(n_rows // tm,),
        in_specs=[pl.BlockSpec((TOP_K, tm, d // 2), lambda i: (0, blk0 + i, 0)), pl.BlockSpec((tm, TOP_K), row),
                  pl.BlockSpec((tm, d), row), pl.BlockSpec((1, d), fixed), pl.BlockSpec((1, d), fixed)],
        out_specs=pl.BlockSpec((tm, d), lambda i: (i, 0)),
        out_shape=jax.ShapeDtypeStruct((n_rows, d), F32),
        compiler_params=_params("parallel"),
        name="combine_ln",
    )(yg, gate, x, g.reshape(1, d), b.reshape(1, d))


def _sc_mesh():
    return plsc.VectorSubcoreMesh(core_axis_name="core", subcore_axis_name="subcore")


def _sc_scatter_rows(x, dest, n_out):
    n, d = x.shape
    mesh = _sc_mesh()
    per_worker = n // (mesh.num_cores * mesh.num_subcores)
    n_sub = SC_INDEX_WINDOW // SC_ROW_WINDOW
    assert per_worker % SC_INDEX_WINDOW == 0 and n_sub >= 2

    @functools.partial(
        pl.kernel, out_type=jax.ShapeDtypeStruct((n_out, d), x.dtype), mesh=mesh,
        scratch_types=[pltpu.VMEM((TOP_K, SC_INDEX_WINDOW), I32), pltpu.VMEM((2, SC_ROW_WINDOW, d), x.dtype),
                       pltpu.SemaphoreType.DMA((2,)), pltpu.SemaphoreType.DMA((2,))])
    def scatter(x_hbm, i_hbm, o_hbm, idx_v, buf, sem_r, sem_s):
        wid = lax.axis_index("core") * mesh.num_subcores + lax.axis_index("subcore")

        @pl.loop(0, per_worker // SC_INDEX_WINDOW)
        def _(it):
            base = wid * per_worker + it * SC_INDEX_WINDOW
            for k in range(TOP_K):
                pltpu.sync_copy(i_hbm.at[k, pl.ds(base, SC_INDEX_WINDOW)], idx_v.at[k])

            def read(j):
                rows = pl.ds(base + j * SC_ROW_WINDOW, SC_ROW_WINDOW)
                return pltpu.make_async_copy(x_hbm.at[rows], buf.at[j % 2], sem_r.at[j % 2])

            def send(j, k):
                rows = idx_v.at[k, pl.ds(j * SC_ROW_WINDOW, SC_ROW_WINDOW)]
                return pltpu.make_async_copy(buf.at[j % 2], o_hbm.at[rows], sem_s.at[j % 2])

            read(0).start()
            for j in range(n_sub):
                read(j).wait()
                for k in range(TOP_K):
                    send(j, k).start()
                if j + 1 < n_sub:
                    if j >= 1:
                        for k in range(TOP_K):
                            send(j - 1, k).wait()
                    read(j + 1).start()
            for j in (n_sub - 2, n_sub - 1):
                for k in range(TOP_K):
                    send(j, k).wait()

    return scatter(x, dest)


def _sc_gather_rows(table, indices):
    num = indices.shape[0]
    d = table.shape[1]
    mesh = _sc_mesh()
    per_worker = num // (mesh.num_cores * mesh.num_subcores)
    n_sub = SC_INDEX_WINDOW // SC_ROW_WINDOW
    assert per_worker % SC_INDEX_WINDOW == 0 and n_sub >= 2

    @functools.partial(
        pl.kernel, out_type=jax.ShapeDtypeStruct((num, d), table.dtype), mesh=mesh,
        scratch_types=[pltpu.VMEM((SC_INDEX_WINDOW,), I32), pltpu.VMEM((2, SC_ROW_WINDOW, d), table.dtype),
                       pltpu.SemaphoreType.DMA((2,)), pltpu.SemaphoreType.DMA((2,))])
    def gather(x_hbm, i_hbm, o_hbm, idx_v, buf, sem_g, sem_w):
        wid = lax.axis_index("core") * mesh.num_subcores + lax.axis_index("subcore")

        @pl.loop(0, per_worker // SC_INDEX_WINDOW)
        def _(it):
            base = wid * per_worker + it * SC_INDEX_WINDOW
            pltpu.sync_copy(i_hbm.at[pl.ds(base, SC_INDEX_WINDOW)], idx_v)

            def fetch(j):
                rows = idx_v.at[pl.ds(j * SC_ROW_WINDOW, SC_ROW_WINDOW)]
                return pltpu.make_async_copy(x_hbm.at[rows], buf.at[j % 2], sem_g.at[j % 2])

            def write(j):
                rows = pl.ds(base + j * SC_ROW_WINDOW, SC_ROW_WINDOW)
                return pltpu.make_async_copy(buf.at[j % 2], o_hbm.at[rows], sem_w.at[j % 2])

            fetch(0).start()
            for j in range(n_sub):
                fetch(j).wait()
                write(j).start()
                if j + 1 < n_sub:
                    if j >= 1:
                        write(j - 1).wait()
                    fetch(j + 1).start()
            write(n_sub - 2).wait()
            write(n_sub - 1).wait()

    return gather(table, indices)


def _moe(x_packed, idx, gate, rank, counts, layer, w1, b1, w2, b2):
    n = x_packed.shape[0]
    d = D_MODEL
    bm = EXPERT_ROWS
    counts = counts[:, 0]
    padded = (counts + bm - 1) // bm * bm
    pad_ends = jnp.cumsum(padded)
    pad_starts = pad_ends - padded
    hot = idx[:, None, :] == jnp.arange(N_EXPERTS, dtype=I32)[None, :, None]
    dest = jnp.sum(jnp.where(hot, pad_starts[None, :, None], 0), axis=1) + rank
    n_blocks = n * TOP_K // bm + N_EXPERTS
    blk_start = jnp.arange(n_blocks, dtype=I32) * bm
    blk_exp = jnp.minimum(jnp.sum(blk_start[:, None] >= pad_ends[None, :], axis=1), N_EXPERTS - 1).astype(I32)
    n_used = (pad_ends[-1:] // bm).astype(I32)
    xs = _sc_scatter_rows(x_packed, dest, n_blocks * bm)
    ys = _experts(xs, blk_exp, n_used, layer, w1, b1, w2, b2)
    yg = _sc_gather_rows(ys, dest.reshape(-1)).reshape(TOP_K, n, d // 2)
    return yg, gate.T


def _positions(groups):
    return jnp.concatenate([jnp.tile(jnp.arange(s), b) for b, s in groups])


def _axial_tables(groups):
    t = _positions(groups)
    n = HEAD_DIM // 4
    inv = AXIAL_THETA ** (-jnp.arange(n, dtype=F32) / n)
    ar = (t // GRID_W).astype(F32)[:, None] * inv
    ac = (t % GRID_W).astype(F32)[:, None] * inv
    z = jnp.zeros_like(ar)
    cr, sr, cc, sc = jnp.cos(ar), jnp.sin(ar), jnp.cos(ac), jnp.sin(ac)
    c = jnp.concatenate([cr, cr, cc, cc], axis=-1)
    s1 = jnp.concatenate([z, sr, z, sc], axis=-1)
    s2 = jnp.concatenate([-sr, z, -sc, z], axis=-1)
    return tuple(jnp.tile(a, (1, 2)) for a in (c, s1, s2))


def _rope_tables(groups):
    t = _positions(groups)
    n = ROPE_DIMS // 2
    inv = ROPE_THETA ** (-jnp.arange(n, dtype=F32) / n)
    ang = t.astype(F32)[:, None] * inv
    c, s = jnp.cos(ang), jnp.sin(ang)
    z = jnp.zeros_like(c)
    rest = HEAD_DIM - ROPE_DIMS
    pad1 = jnp.ones((t.shape[0], rest), F32)
    pad0 = jnp.zeros((t.shape[0], rest), F32)
    cc = jnp.concatenate([c, c, pad1], axis=-1)
    s1 = jnp.concatenate([z, s, pad0], axis=-1)
    s2 = jnp.concatenate([-s, z, pad0], axis=-1)
    return tuple(jnp.tile(a, (1, 2)) for a in (cc, s1, s2))


def _gqa_head_order():
    g = GQA_Q_HEADS // GQA_KV_HEADS
    return [h for j in range(g) for h in (j, g + j)]


def _mixer_even(x, groups, tabs, w_in, rpb, q_gain, k_gain, w_out):
    hd = HEAD_DIM
    na_w = NA_HEADS * hd
    order = _gqa_head_order()
    q0 = 3 * na_w
    q_cols = np.concatenate([q0 + h * hd + np.arange(hd) for h in order])
    w_in_p = jnp.concatenate([w_in[:, :q0], w_in[:, q_cols], w_in[:, q0 + GQA_Q_HEADS * hd:]], axis=1).astype(BF16)
    out_rows = np.concatenate([na_w + h * hd + np.arange(hd) for h in order])
    w_out_p = jnp.concatenate([w_out[:na_w], w_out[out_rows]], axis=0).astype(BF16)
    n_na = na_w // LANES
    n_q = GQA_Q_HEADS * hd // LANES
    modes = ([("plain", Q_SCALE, 0)] * n_na + [("plain", 1.0, 0)] * (2 * n_na)
             + [("norm_rope", Q_SCALE, 0)] * n_q + [("norm_rope", 1.0, 1)] + [("plain", 1.0, 0)])
    gains = jnp.stack([jnp.tile(q_gain.astype(F32), 2), jnp.tile(k_gain.astype(F32), 2)])
    proj = _project(x, w_in_p, tabs, gains, modes, HEAD_DIM // 4)
    cc = _na_bias_table(rpb)
    ya, yb = [], []
    row0 = 0
    for b, s in groups:
        ya.append(_na_attention(proj, cc, row0, b, s))
        yb.append(_gqa_attention(proj, row0, b, s))
        row0 += b * s
    return [(jnp.concatenate(ya, axis=0), w_out_p[:na_w]), (jnp.concatenate(yb, axis=0), w_out_p[na_w:])]


def _mixer_odd(x, groups, tabs, w_in, w_out):
    n_slab = DIL_HEADS * HEAD_DIM // LANES
    modes = [("rope", Q_SCALE, 0)] * n_slab + [("rope", 1.0, 0)] * n_slab + [("plain", 1.0, 0)] * n_slab
    gains = jnp.ones((1, LANES), F32)
    qkv = _project(x, w_in.astype(BF16), tabs, gains, modes, ROPE_DIMS // 2, out_dtype=F32)
    parts, row0 = [], 0
    for b, s in groups:
        parts.append(_dilated_attention(qkv, row0, b, s))
        row0 += b * s
    return [(jnp.concatenate(parts, axis=0), w_out.astype(BF16))]


def _trunk(xs, w_in_even, rpb_a, q_gain_b, k_gain_b, w_out_even, w_in_odd, w_out_odd,
           ln1_g, ln1_b, ln2_g, ln2_b, router_w, router_b, moe_w1, moe_b1, moe_w2, moe_b2):
    groups = [(x.shape[0], x.shape[1]) for x in xs]
    x = jnp.concatenate([t.reshape(-1, D_MODEL) for t in xs], axis=0).astype(F32)
    tabs_even = _axial_tables(groups)
    tabs_odd = _rope_tables(groups)
    b1 = moe_b1.astype(F32)[:, :, None, :]
    b2 = moe_b2.astype(F32)[:, :, None, :]
    for l in range(DEPTH):
        i = l // 2
        if l % 2 == 0:
            parts = _mixer_even(x, groups, tabs_even, w_in_even[i], rpb_a[i], q_gain_b[i], k_gain_b[i],
                                w_out_even[i])
        else:
            parts = _mixer_odd(x, groups, tabs_odd, w_in_odd[i], w_out_odd[i])
        x, x_packed, idx, gate, rank, counts = _outproj_ln_route(parts, x, ln1_g[l], ln1_b[l], router_w[l],
                                                                 router_b[l])
        yg, gate = _moe(x_packed, idx, gate, rank, counts, l, moe_w1, b1, moe_w2, b2)
        if l + 1 < DEPTH:
            x = _combine_ln(yg, gate, x, ln2_g[l], ln2_b[l])
    outs, row0 = [], 0
    for t in xs:
        n = t.shape[0] * t.shape[1]
        outs.append(_combine_ln(yg, gate, x, ln2_g[-1], ln2_b[-1], row0, n).reshape(t.shape))
        row0 += n
    return tuple(outs)


def kernel(x_prompt, x_sample, w_in_even, rpb_a, q_gain_b, k_gain_b, w_out_even, w_in_odd, w_out_odd, ln1_g, ln1_b,
           ln2_g, ln2_b, router_w, router_b, moe_w1, moe_b1, moe_w2, moe_b2):
    params = (w_in_even, rpb_a, q_gain_b, k_gain_b, w_out_even, w_in_odd, w_out_odd,
              ln1_g, ln1_b, ln2_g, ln2_b, router_w, router_b, moe_w1.astype(BF16), moe_b1, moe_w2.astype(BF16), moe_b2)
    (y_prompt,) = _trunk((x_prompt,), *params)
    (y_sample,) = _trunk((x_sample,), *params)
    return y_prompt, y_sample
```

```python
import functools

import jax
import jax.numpy as jnp
import numpy as np
from jax import lax
from jax.experimental import pallas as pl
from jax.experimental.pallas import tpu as pltpu
from jax.experimental.pallas import tpu_sc as plsc

F32 = jnp.float32
BF16 = jnp.bfloat16
I32 = jnp.int32

D_MODEL = 1024
DEPTH = 4
HEAD_DIM = 64
GRID_W = 64
NA_HEADS = 8
NA_ROWS = 8
NA_COLS = 16
GQA_Q_HEADS = 8
GQA_KV_HEADS = 2
AXIAL_THETA = 10000.0
QK_NORM_EPS = 1e-6
DIL_HEADS = 16
DIL_BRANCHES = ((128, 1), (512, 4), (2048, 16))
ROPE_THETA = 500000.0
ROPE_DIMS = HEAD_DIM // 4
N_EXPERTS = 32
TOP_K = 4
D_FF = D_MODEL
SWIGLU_LIMIT = 7.0
SWIGLU_ALPHA = 1.702
DN_ALPHA = (2 * DEPTH) ** 0.25
LN_EPS = 1e-5
LOG2E = 1.4426950408889634
Q_SCALE = HEAD_DIM ** -0.5 * LOG2E

LANES = 128
NEG = -1e30
VMEM_LIMIT = 56 * 1024 * 1024
TOKEN_TILE = 512
COMBINE_TILE = 1024
EXPERT_ROWS = 512
NA_ROW_BLOCK = 16
NA_ROW_GROUP = 8
GQA_Q_TILE = 512
GQA_K_TILE = 512
GQA_DEN_ROWS = 16
WIN_Q_TILE = 128
WIN_RADIUS = 64
DIL_GROUP = 8
SC_INDEX_WINDOW = 128
SC_ROW_WINDOW = 64

EVEN_IN = 3 * NA_HEADS * HEAD_DIM + GQA_Q_HEADS * HEAD_DIM + 2 * GQA_KV_HEADS * HEAD_DIM


def _params(*sem):
    return pltpu.CompilerParams(dimension_semantics=sem, vmem_limit_bytes=VMEM_LIMIT)


def _lane_is_low():
    return lax.broadcasted_iota(I32, (1, LANES), 1) < HEAD_DIM


def _proj_kernel(x_ref, w_ref, c_ref, s1_ref, s2_ref, gain_ref, o_ref, *, slab_modes, shift):
    x = x_ref[...].astype(BF16)
    n_out = o_ref.shape[1]
    chunk = 2 * LANES
    if any(m[0] == "norm_rope" for m in slab_modes):
        r = lax.broadcasted_iota(I32, (LANES, LANES), 0) // HEAD_DIM
        c = lax.broadcasted_iota(I32, (LANES, LANES), 1) // HEAD_DIM
        head_mean = jnp.where(r == c, 1.0 / HEAD_DIM, 0.0).astype(BF16)
    for c0 in range(0, n_out, chunk):
        acc = jnp.dot(x, w_ref[:, c0:c0 + chunk], preferred_element_type=F32)
        for s in range(chunk // LANES):
            slab = c0 // LANES + s
            mode, scale, gidx = slab_modes[slab]
            y = acc[:, s * LANES:(s + 1) * LANES]
            if mode == "norm_rope":
                sq = y * y
                hi = sq.astype(BF16)
                lo = (sq - hi.astype(F32)).astype(BF16)
                ms = (jnp.dot(hi, head_mean, preferred_element_type=F32)
                      + jnp.dot(lo, head_mean, preferred_element_type=F32))
                y = y * lax.rsqrt(ms + QK_NORM_EPS) * gain_ref[gidx:gidx + 1, :]
            if mode in ("rope", "norm_rope"):
                y = (y * c_ref[...] + pltpu.roll(y, LANES - shift, 1) * s2_ref[...]
                     + pltpu.roll(y, shift, 1) * s1_ref[...])
            if scale != 1.0:
                y = y * scale
            o_ref[:, slab * LANES:(slab + 1) * LANES] = y.astype(o_ref.dtype)


def _project(x, w, tabs, gains, slab_modes, shift, out_dtype=BF16):
    n, d = x.shape
    m = w.shape[1]
    tm = TOKEN_TILE
    tab_blocks = tabs[0].shape[0] // tm
    tab_spec = pl.BlockSpec((tm, LANES), lambda i: (i % tab_blocks, 0))
    return pl.pallas_call(
        functools.partial(_proj_kernel, slab_modes=tuple(slab_modes), shift=shift),
        grid=(n // tm,),
        in_specs=[pl.BlockSpec((tm, d), lambda i: (i, 0)),
                  pl.BlockSpec((d, m), lambda i: (0, 0)),
                  tab_spec, tab_spec, tab_spec,
                  pl.BlockSpec(gains.shape, lambda i: (0, 0))],
        out_specs=pl.BlockSpec((tm, m), lambda i: (i, 0)),
        out_shape=jax.ShapeDtypeStruct((n, m), out_dtype),
        compiler_params=_params("parallel"),
        name="in_proj",
    )(x, w, tabs[0], tabs[1], tabs[2], gains)


def _layer_norm_rows(z, g, b):
    mu = jnp.mean(z, axis=-1, keepdims=True)
    zc = z - mu
    var = jnp.mean(zc * zc, axis=-1, keepdims=True)
    return zc * lax.rsqrt(var + LN_EPS) * g + b


def _pack_bf16_pairs(v):
    half = v.shape[1] // 2
    bits = pltpu.bitcast(v.astype(BF16).astype(F32), jnp.uint32)
    return (bits[:, :half] >> 16) | bits[:, half:]


def _unpack_bf16_pairs(w):
    lo = pltpu.bitcast(w << 16, F32)
    hi = pltpu.bitcast(w & jnp.uint32(0xFFFF0000), F32)
    return lo, hi


def _route_tile(x, wh_ref, wl_ref, rb_ref, base_ref, idx_ref, gate_ref, rank_ref, cnt_ref):
    xh = x.astype(BF16)
    xl = (x - xh.astype(F32)).astype(BF16)
    nt = (((1,), (1,)), ((), ()))
    logits = (lax.dot_general(wh_ref[...], xh, nt, preferred_element_type=F32)
              + lax.dot_general(wl_ref[...], xh, nt, preferred_element_type=F32)
              + lax.dot_general(wh_ref[...], xl, nt, preferred_element_type=F32)) + rb_ref[...]
    tm = x.shape[0]
    eid = lax.broadcasted_iota(I32, (N_EXPERTS, tm), 0)
    vals = logits
    top_v, top_i, hots = [], [], []
    for _ in range(TOP_K):
        m = jnp.max(vals, axis=0, keepdims=True)
        idx = jnp.min(jnp.where(vals == m, eid, N_EXPERTS), axis=0, keepdims=True)
        hot = eid == idx
        top_v.append(m)
        top_i.append(idx)
        hots.append(hot)
        vals = jnp.where(hot, -jnp.inf, vals)
    es = [jnp.exp(v - top_v[0]) for v in top_v]
    den = functools.reduce(jnp.add, es)
    chosen = functools.reduce(jnp.logical_or, hots)
    before = (lax.broadcasted_iota(I32, (tm, tm), 0) < lax.broadcasted_iota(I32, (tm, tm), 1))
    prefix = jnp.dot(chosen.astype(BF16), before.astype(BF16), preferred_element_type=F32) + base_ref[...]
    for k in range(TOP_K):
        idx_ref[k:k + 1, :] = top_i[k]
        gate_ref[k:k + 1, :] = es[k] / den
        rank_ref[k:k + 1, :] = jnp.sum(jnp.where(hots[k], prefix, 0.0), axis=0, keepdims=True).astype(I32)
    base_ref[...] = base_ref[...] + jnp.sum(chosen.astype(F32), axis=1, keepdims=True)
    cnt_ref[...] = base_ref[...].astype(I32)


def _outproj_ln_kernel(*refs, n_parts):
    a_refs, w_refs = refs[:n_parts], refs[n_parts:2 * n_parts]
    x_ref, g_ref, b_ref, wh_ref, wl_ref, rb_ref = refs[2 * n_parts:2 * n_parts + 6]
    o_ref, p_ref, idx_ref, gate_ref, rank_ref, cnt_ref, base_ref = refs[2 * n_parts + 6:]

    @pl.when(pl.program_id(0) == 0)
    def _():
        base_ref[...] = jnp.zeros_like(base_ref)

    y = functools.reduce(jnp.add, [jnp.dot(a[...], w[...], preferred_element_type=F32)
                                   for a, w in zip(a_refs, w_refs)])
    out = _layer_norm_rows(DN_ALPHA * x_ref[...] + y, g_ref[...], b_ref[...])
    o_ref[...] = out
    p_ref[...] = _pack_bf16_pairs(out)
    _route_tile(out, wh_ref, wl_ref, rb_ref, base_ref, idx_ref, gate_ref, rank_ref, cnt_ref)


def _outproj_ln_route(parts, x, g, b, w_r, b_r):
    n, d = x.shape
    tm = TOKEN_TILE
    row = lambda i: (i, 0)
    fixed = lambda i: (0, 0)
    tok = pl.BlockSpec((TOP_K, tm), lambda i: (0, i))
    wr_t = w_r.T
    wr_hi = wr_t.astype(BF16)
    wr_lo = (wr_t - wr_hi.astype(F32)).astype(BF16)
    expert_vec = pl.BlockSpec((N_EXPERTS, 1), fixed)
    return pl.pallas_call(
        functools.partial(_outproj_ln_kernel, n_parts=len(parts)),
        grid=(n // tm,),
        in_specs=([pl.BlockSpec((tm, a.shape[1]), row) for a, _ in parts]
                  + [pl.BlockSpec(w.shape, fixed) for _, w in parts]
                  + [pl.BlockSpec((tm, d), row), pl.BlockSpec((1, d), fixed), pl.BlockSpec((1, d), fixed),
                     pl.BlockSpec((N_EXPERTS, d), fixed), pl.BlockSpec((N_EXPERTS, d), fixed), expert_vec]),
        out_specs=[pl.BlockSpec((tm, d), row), pl.BlockSpec((tm, d // 2), row), tok, tok, tok, expert_vec],
        out_shape=[jax.ShapeDtypeStruct((n, d), F32), jax.ShapeDtypeStruct((n, d // 2), jnp.uint32),
                   jax.ShapeDtypeStruct((TOP_K, n), I32), jax.ShapeDtypeStruct((TOP_K, n), F32),
                   jax.ShapeDtypeStruct((TOP_K, n), I32), jax.ShapeDtypeStruct((N_EXPERTS, 1), I32)],
        scratch_shapes=[pltpu.VMEM((N_EXPERTS, 1), F32)],
        compiler_params=_params("arbitrary"),
        name="out_proj_ln_route",
    )(*[a for a, _ in parts], *[w for _, w in parts], x, g.reshape(1, d), b.reshape(1, d), wr_hi, wr_lo,
      b_r.reshape(N_EXPERTS, 1).astype(F32))


def _na_kernel(q_ref, k_ref, v_ref, cc_ref, o_ref, *, rows):
    i = pl.program_id(2)
    low = _lane_is_low()

    for g0 in range(0, NA_ROW_BLOCK, NA_ROW_GROUP):
        wins, scores = [], []
        for rr in range(g0, g0 + NA_ROW_GROUP):
            r = i * NA_ROW_BLOCK + rr
            rs = jnp.clip(r - NA_ROWS // 2, 0, rows - NA_ROWS)
            var = r - rs
            q = q_ref[rr * GRID_W:(rr + 1) * GRID_W, :]
            kstart = pl.multiple_of(rs * GRID_W, GRID_W)
            kwin = k_ref[pl.ds(kstart, NA_ROWS * GRID_W), :]
            wins.append(kstart)
            for half in range(2):
                qm = jnp.where(low if half == 0 else jnp.logical_not(low), q, jnp.zeros_like(q))
                s = lax.dot_general(qm, kwin, (((1,), (1,)), ((), ())), preferred_element_type=F32)
                scores.append(s + cc_ref[half, var])
        probs = []
        for s in scores:
            p = jnp.exp2(s - jnp.max(s, axis=-1, keepdims=True))
            probs.append((p.astype(BF16), jnp.sum(p, axis=-1, keepdims=True)))
        for u, rr in enumerate(range(g0, g0 + NA_ROW_GROUP)):
            vwin = v_ref[pl.ds(wins[u], NA_ROWS * GRID_W), :]
            outs = [jnp.dot(p, vwin, preferred_element_type=F32) / l for p, l in probs[2 * u:2 * u + 2]]
            o_ref[rr * GRID_W:(rr + 1) * GRID_W, :] = jnp.where(low, outs[0], outs[1]).astype(o_ref.dtype)


def _na_attention(proj, cc, row0, b, s):
    rows = s // GRID_W
    assert rows >= NA_ROWS and rows % NA_ROW_BLOCK == 0 and row0 % s == 0
    tq = NA_ROW_BLOCK * GRID_W
    n_slab = NA_HEADS * HEAD_DIM // LANES
    qb0, sb0 = row0 // tq, row0 // s
    return pl.pallas_call(
        functools.partial(_na_kernel, rows=rows),
        grid=(b, n_slab, s // tq),
        in_specs=[pl.BlockSpec((tq, LANES), lambda bi, j, i: (qb0 + bi * (s // tq) + i, j)),
                  pl.BlockSpec((s, LANES), lambda bi, j, i: (sb0 + bi, n_slab + j)),
                  pl.BlockSpec((s, LANES), lambda bi, j, i: (sb0 + bi, 2 * n_slab + j)),
                  pl.BlockSpec((2, NA_ROWS, GRID_W, NA_ROWS * GRID_W), lambda bi, j, i: (j, 0, 0, 0))],
        out_specs=pl.BlockSpec((tq, LANES), lambda bi, j, i: (bi * (s // tq) + i, j)),
        out_shape=jax.ShapeDtypeStruct((b * s, n_slab * LANES), BF16),
        compiler_params=_params("parallel", "parallel", "parallel"),
        name="na_attn",
    )(proj, proj, proj, cc)


def _na_bias_table(rpb):
    var = np.arange(NA_ROWS)[:, None]
    j = np.arange(NA_ROWS)[None, :]
    qc = np.arange(GRID_W)[:, None]
    kc = np.arange(GRID_W)[None, :]
    row_sel = ((j - var + NA_ROWS - 1)[..., None] == np.arange(2 * NA_ROWS - 1)).astype(np.float32)
    col_sel = (np.clip(kc - qc + NA_COLS - 1, 0, 2 * NA_COLS - 2)[..., None]
               == np.arange(2 * NA_COLS - 1)).astype(np.float32)
    win = np.clip(qc - NA_COLS // 2, 0, GRID_W - NA_COLS)
    valid = (kc >= win) & (kc < win + NA_COLS)
    t = jnp.einsum("hab,vja,qkb->hvqjk", rpb.astype(F32), row_sel, col_sel, precision=lax.Precision.HIGHEST)
    t = jnp.where(valid[None, None, :, None, :], t * LOG2E, NEG)
    return t.reshape(rpb.shape[0], NA_ROWS, GRID_W, NA_ROWS * GRID_W)


def _gqa_kernel(q_ref, k_ref, vt_ref, o_ref, acc_ref, st_ref, *, n_kt):
    assert n_kt % 2 == 0
    low = _lane_is_low()
    q = q_ref[...]
    tq = q.shape[0]
    zero = jnp.zeros_like(q)
    qs = jnp.concatenate([jnp.where(low, q, zero), jnp.where(low, zero, q)], axis=0)
    acc_ref[...] = jnp.zeros_like(acc_ref)

    def scores(kt, slot):
        start = pl.multiple_of(kt * GQA_K_TILE, GQA_K_TILE)
        kb = k_ref[pl.ds(start, GQA_K_TILE), :]
        st_ref[slot] = lax.dot_general(kb, qs, (((1,), (1,)), ((), ())), preferred_element_type=F32)

    def softmax_pv(kt, slot, m_old):
        st = st_ref[slot]
        m_new = jnp.maximum(m_old, jnp.max(st, axis=0, keepdims=True))
        alpha = jnp.exp2(m_old - m_new)
        pt = jnp.exp2(st - m_new).astype(BF16)
        acc_ref[...] = alpha * acc_ref[...] + jnp.dot(vt_ref[kt], pt, preferred_element_type=F32)
        return m_new

    def step(i, m):
        kt = 2 * i
        scores(kt + 1, 1)
        m = softmax_pv(kt, 0, m)
        scores(jnp.minimum(kt + 2, n_kt - 1), 0)
        return softmax_pv(kt + 1, 1, m)

    scores(0, 0)
    lax.fori_loop(0, n_kt // 2, step, jnp.full((1, 2 * tq), -jnp.inf, F32))
    out_t = acc_ref[:LANES, :] / acc_ref[LANES:LANES + 1, :]
    o_t = jnp.concatenate([out_t[:HEAD_DIM, :tq], out_t[HEAD_DIM:, tq:]], axis=0)
    o_ref[...] = o_t.T.astype(o_ref.dtype)


def _gqa_attention(proj, row0, b, s):
    tq, tk = GQA_Q_TILE, GQA_K_TILE
    assert s % tk == 0 and row0 % s == 0
    n_slab = GQA_Q_HEADS * HEAD_DIM // LANES
    q_col = 3 * NA_HEADS * HEAD_DIM // LANES
    k_col = q_col + n_slab
    v0 = (k_col + 1) * LANES
    qb0, sb0 = row0 // tq, row0 // s
    v_t = proj[row0:row0 + b * s, v0:v0 + LANES].reshape(b, s // tk, tk, LANES).transpose(0, 1, 3, 2)
    ones_rows = jnp.zeros((b, s // tk, GQA_DEN_ROWS, tk), BF16).at[:, :, 0, :].set(1.0)
    v_t = jnp.concatenate([v_t, ones_rows], axis=2)
    vt_rows = LANES + GQA_DEN_ROWS
    return pl.pallas_call(
        functools.partial(_gqa_kernel, n_kt=s // tk),
        grid=(b, n_slab, s // tq),
        in_specs=[pl.BlockSpec((tq, LANES), lambda bi, j, i: (qb0 + bi * (s // tq) + i, q_col + j)),
                  pl.BlockSpec((s, LANES), lambda bi, j, i: (sb0 + bi, k_col)),
                  pl.BlockSpec((None, s // tk, vt_rows, tk), lambda bi, j, i: (bi, 0, 0, 0))],
        out_specs=pl.BlockSpec((tq, LANES), lambda bi, j, i: (bi * (s // tq) + i, j)),
        out_shape=jax.ShapeDtypeStruct((b * s, n_slab * LANES), BF16),
        scratch_shapes=[pltpu.VMEM((vt_rows, 2 * tq), F32), pltpu.VMEM((2, tk, 2 * tq), F32)],
        compiler_params=_params("parallel", "parallel", "parallel"),
        name="gqa_attn",
    )(proj, proj, v_t)


def _dil_kernel(q_ref, k_ref, v_ref, o_ref, m_ref, l_ref, acc_ref, *kv_refs, seq):
    tile = q_ref.shape[0]
    t0 = pl.program_id(2) * tile
    low = _lane_is_low()
    sub = WIN_Q_TILE
    n_sub = tile // sub

    @pl.when(pl.program_id(2) == 0)
    def _():
        for bi, (_, r) in enumerate(DIL_BRANCHES):
            for c in range(r):
                rows = pl.ds(c, seq // r, stride=r) if r > 1 else pl.ds(0, seq)
                kv_refs[2 * bi][c] = k_ref[rows, :].astype(BF16)
                kv_refs[2 * bi + 1][c] = v_ref[rows, :].astype(BF16)

    for bi, (_, r) in enumerate(DIL_BRANCHES):
        length = seq // r
        win = min(sub + 2 * WIN_RADIUS, length)
        offset = lax.broadcasted_iota(I32, (sub, win), 1) - lax.broadcasted_iota(I32, (sub, win), 0)
        kc_ref, vc_ref = kv_refs[2 * bi], kv_refs[2 * bi + 1]

        def sub_blocks(g, carry, r=r, length=length, win=win, first=(bi == 0), offset=offset, kc_ref=kc_ref,
                       vc_ref=vc_ref):
            rows, scores, stats = [], [], []
            for u in range(DIL_GROUP):
                n = g * DIL_GROUP + u
                c = n % r
                blk = n // r
                u0 = t0 // r + blk * sub
                ks = pl.multiple_of(jnp.clip(u0 - WIN_RADIUS, 0, length - win), WIN_RADIUS)
                if r == 1:
                    q_rows = pl.ds(pl.multiple_of(n * sub, sub), sub)
                else:
                    q_rows = pl.ds(c + r * blk * sub, sub, stride=r)
                rows.append((q_rows, c, ks))
                q = q_ref[q_rows, :].astype(BF16)
                kwin = kc_ref[c, pl.ds(ks, win), :]
                valid = jnp.abs(offset + (ks - u0)) <= WIN_RADIUS
                for half in range(2):
                    qm = jnp.where(low if half == 0 else jnp.logical_not(low), q, jnp.zeros_like(q))
                    s = lax.dot_general(qm, kwin, (((1,), (1,)), ((), ())), preferred_element_type=F32)
                    scores.append(jnp.where(valid, s, NEG))
            for s in scores:
                m = jnp.max(s, axis=-1, keepdims=True)
                p = jnp.exp2(s - m)
                stats.append((m, jnp.sum(p, axis=-1, keepdims=True), p.astype(BF16)))
            for u, (q_rows, c, ks) in enumerate(rows):
                vwin = vc_ref[c, pl.ds(ks, win), :]
                (m0, l0, p0), (m1, l1, p1) = stats[2 * u:2 * u + 2]
                m_b = jnp.where(low, m0, m1)
                l_b = jnp.where(low, l0, l1)
                pv_b = jnp.where(low, jnp.dot(p0, vwin, preferred_element_type=F32),
                                 jnp.dot(p1, vwin, preferred_element_type=F32))
                if first:
                    m_ref[q_rows, :] = m_b
                    l_ref[q_rows, :] = l_b
                    acc_ref[q_rows, :] = pv_b
                else:
                    m_old = m_ref[q_rows, :]
                    m_new = jnp.maximum(m_old, m_b)
                    a_old = jnp.exp2(m_old - m_new)
                    a_b = jnp.exp2(m_b - m_new)
                    l_ref[q_rows, :] = a_old * l_ref[q_rows, :] + a_b * l_b
                    acc_ref[q_rows, :] = a_old * acc_ref[q_rows, :] + a_b * pv_b
                    m_ref[q_rows, :] = m_new
            return carry

        lax.fori_loop(0, n_sub // DIL_GROUP, sub_blocks, 0)

    o_ref[...] = (acc_ref[...] / l_ref[...]).astype(o_ref.dtype)


def _dilated_attention(qkv, row0, b, s):
    tile = 16 * WIN_Q_TILE
    assert s % tile == 0 and row0 % s == 0
    n_slab = DIL_HEADS * HEAD_DIM // LANES
    qb0, sb0 = row0 // tile, row0 // s
    state = pltpu.VMEM((tile, LANES), F32)
    by_class = [pltpu.VMEM((r, s // r, LANES), BF16) for _, r in DIL_BRANCHES for _ in ("k", "v")]
    return pl.pallas_call(
        functools.partial(_dil_kernel, seq=s),
        grid=(b, n_slab, s // tile),
        in_specs=[pl.BlockSpec((tile, LANES), lambda bi, j, i: (qb0 + bi * (s // tile) + i, j)),
                  pl.BlockSpec((s, LANES), lambda bi, j, i: (sb0 + bi, n_slab + j)),
                  pl.BlockSpec((s, LANES), lambda bi, j, i: (sb0 + bi, 2 * n_slab + j))],
        out_specs=pl.BlockSpec((tile, LANES), lambda bi, j, i: (bi * (s // tile) + i, j)),
        out_shape=jax.ShapeDtypeStruct((b * s, n_slab * LANES), BF16),
        scratch_shapes=[state, state, state] + by_class,
        compiler_params=_params("parallel", "parallel", "arbitrary"),
        name="dil_attn",
    )(qkv, qkv, qkv)


def _expert_kernel(blk_exp_ref, n_used_ref, x_ref, w1_ref, b1_ref, w2_ref, b2_ref, o_ref):
    blk = pl.program_id(0)

    @pl.when(blk < n_used_ref[0])
    def _():
        x_lo, x_hi = _unpack_bf16_pairs(x_ref[...])
        x = jnp.concatenate([x_lo.astype(BF16), x_hi.astype(BF16)], axis=1)
        h = jnp.dot(x, w1_ref[...], preferred_element_type=F32) + b1_ref[...]
        g = jnp.minimum(h[:, :D_FF], SWIGLU_LIMIT)
        u = jnp.clip(h[:, D_FF:], -SWIGLU_LIMIT, SWIGLU_LIMIT)
        act = g * jax.nn.sigmoid(SWIGLU_ALPHA * g) * (u + 1.0)
        y = jnp.dot(act.astype(BF16), w2_ref[...], preferred_element_type=F32) + b2_ref[...]
        o_ref[...] = _pack_bf16_pairs(y)

    @pl.when(blk >= n_used_ref[0])
    def _():
        o_ref[...] = jnp.zeros_like(o_ref)


def _experts(xs, blk_exp, n_used, layer, w1, b1, w2, b2):
    n_rows = xs.shape[0]
    d = D_MODEL
    bm = EXPERT_ROWS
    f2 = w1.shape[3]
    wmap = lambda i, be, nu: (layer, be[i], 0, 0)
    return pl.pallas_call(
        _expert_kernel,
        grid_spec=pltpu.PrefetchScalarGridSpec(
            num_scalar_prefetch=2,
            grid=(n_rows // bm,),
            in_specs=[pl.BlockSpec((bm, d // 2), lambda i, be, nu: (i, 0)),
                      pl.BlockSpec((None, None, d, f2), wmap),
                      pl.BlockSpec((None, None, 1, f2), wmap),
                      pl.BlockSpec((None, None, f2 // 2, d), wmap),
                      pl.BlockSpec((None, None, 1, d), wmap)],
            out_specs=pl.BlockSpec((bm, d // 2), lambda i, be, nu: (i, 0)),
        ),
        out_shape=jax.ShapeDtypeStruct((n_rows, d // 2), jnp.uint32),
        compiler_params=_params("arbitrary"),
        name="experts",
    )(blk_exp, n_used, xs, w1, b1, w2, b2)


def _combine_ln_kernel(y_ref, gate_ref, x_ref, g_ref, b_ref, o_ref):
    gate = gate_ref[...]
    halves = [_unpack_bf16_pairs(y_ref[k]) for k in range(TOP_K)]
    y = jnp.concatenate([functools.reduce(jnp.add, [gate[:, k:k + 1] * halves[k][h] for k in range(TOP_K)])
                         for h in range(2)], axis=1)
    o_ref[...] = _layer_norm_rows(DN_ALPHA * x_ref[...] + y, g_ref[...], b_ref[...])


def _combine_ln(yg, gate, x, g, b, row0=0, n_rows=None):
    n, d = x.shape
    n_rows = n if n_rows is None else n_rows
    tm = COMBINE_TILE
    assert row0 % tm == 0 and n_rows % tm == 0
    blk0 = row0 // tm
    row = lambda i: (blk0 + i, 0)
    fixed = lambda i: (0, 0)
    return pl.pallas_call(
        _combine_ln_kernel,
        grid=(n_rows // tm,),
        in_specs=[pl.BlockSpec((TOP_K, tm, d // 2), lambda i: (0, blk0 + i, 0)), pl.BlockSpec((tm, TOP_K), row),
                  pl.BlockSpec((tm, d), row), pl.BlockSpec((1, d), fixed), pl.BlockSpec((1, d), fixed)],
        out_specs=pl.BlockSpec((tm, d), lambda i: (i, 0)),
        out_shape=jax.ShapeDtypeStruct((n_rows, d), F32),
        compiler_params=_params("parallel"),
        name="combine_ln",
    )(yg, gate, x, g.reshape(1, d), b.reshape(1, d))


def _sc_mesh():
    return plsc.VectorSubcoreMesh(core_axis_name="core", subcore_axis_name="subcore")


def _sc_scatter_rows(x, dest, n_out):
    n, d = x.shape
    mesh = _sc_mesh()
    per_worker = n // (mesh.num_cores * mesh.num_subcores)
    n_sub = SC_INDEX_WINDOW // SC_ROW_WINDOW
    assert per_worker % SC_INDEX_WINDOW == 0 and n_sub >= 2

    @functools.partial(
        pl.kernel, out_type=jax.ShapeDtypeStruct((n_out, d), x.dtype), mesh=mesh,
        scratch_types=[pltpu.VMEM((TOP_K, SC_INDEX_WINDOW), I32), pltpu.VMEM((2, SC_ROW_WINDOW, d), x.dtype),
                       pltpu.SemaphoreType.DMA((2,)), pltpu.SemaphoreType.DMA((2,))])
    def scatter(x_hbm, i_hbm, o_hbm, idx_v, buf, sem_r, sem_s):
        wid = lax.axis_index("core") * mesh.num_subcores + lax.axis_index("subcore")

        @pl.loop(0, per_worker // SC_INDEX_WINDOW)
        def _(it):
            base = wid * per_worker + it * SC_INDEX_WINDOW
            for k in range(TOP_K):
                pltpu.sync_copy(i_hbm.at[k, pl.ds(base, SC_INDEX_WINDOW)], idx_v.at[k])

            def read(j):
                rows = pl.ds(base + j * SC_ROW_WINDOW, SC_ROW_WINDOW)
                return pltpu.make_async_copy(x_hbm.at[rows], buf.at[j % 2], sem_r.at[j % 2])

            def send(j, k):
                rows = idx_v.at[k, pl.ds(j * SC_ROW_WINDOW, SC_ROW_WINDOW)]
                return pltpu.make_async_copy(buf.at[j % 2], o_hbm.at[rows], sem_s.at[j % 2])

            read(0).start()
            for j in range(n_sub):
                read(j).wait()
                for k in range(TOP_K):
                    send(j, k).start()
                if j + 1 < n_sub:
                    if j >= 1:
                        for k in range(TOP_K):
                            send(j - 1, k).wait()
                    read(j + 1).start()
            for j in (n_sub - 2, n_sub - 1):
                for k in range(TOP_K):
                    send(j, k).wait()

    return scatter(x, dest)


def _sc_gather_rows(table, indices):
    num = indices.shape[0]
    d = table.shape[1]
    mesh = _sc_mesh()
    per_worker = num // (mesh.num_cores * mesh.num_subcores)
    n_sub = SC_INDEX_WINDOW // SC_ROW_WINDOW
    assert per_worker % SC_INDEX_WINDOW == 0 and n_sub >= 2

    @functools.partial(
        pl.kernel, out_type=jax.ShapeDtypeStruct((num, d), table.dtype), mesh=mesh,
        scratch_types=[pltpu.VMEM((SC_INDEX_WINDOW,), I32), pltpu.VMEM((2, SC_ROW_WINDOW, d), table.dtype),
                       pltpu.SemaphoreType.DMA((2,)), pltpu.SemaphoreType.DMA((2,))])
    def gather(x_hbm, i_hbm, o_hbm, idx_v, buf, sem_g, sem_w):
        wid = lax.axis_index("core") * mesh.num_subcores + lax.axis_index("subcore")

        @pl.loop(0, per_worker // SC_INDEX_WINDOW)
        def _(it):
            base = wid * per_worker + it * SC_INDEX_WINDOW
            pltpu.sync_copy(i_hbm.at[pl.ds(base, SC_INDEX_WINDOW)], idx_v)

            def fetch(j):
                rows = idx_v.at[pl.ds(j * SC_ROW_WINDOW, SC_ROW_WINDOW)]
                return pltpu.make_async_copy(x_hbm.at[rows], buf.at[j % 2], sem_g.at[j % 2])

            def write(j):
                rows = pl.ds(base + j * SC_ROW_WINDOW, SC_ROW_WINDOW)
                return pltpu.make_async_copy(buf.at[j % 2], o_hbm.at[rows], sem_w.at[j % 2])

            fetch(0).start()
            for j in range(n_sub):
                fetch(j).wait()
                write(j).start()
                if j + 1 < n_sub:
                    if j >= 1:
                        write(j - 1).wait()
                    fetch(j + 1).start()
            write(n_sub - 2).wait()
            write(n_sub - 1).wait()

    return gather(table, indices)


def _moe(x_packed, idx, gate, rank, counts, layer, w1, b1, w2, b2):
    n = x_packed.shape[0]
    d = D_MODEL
    bm = EXPERT_ROWS
    counts = counts[:, 0]
    padded = (counts + bm - 1) // bm * bm
    pad_ends = jnp.cumsum(padded)
    pad_starts = pad_ends - padded
    hot = idx[:, None, :] == jnp.arange(N_EXPERTS, dtype=I32)[None, :, None]
    dest = jnp.sum(jnp.where(hot, pad_starts[None, :, None], 0), axis=1) + rank
    n_blocks = n * TOP_K // bm + N_EXPERTS
    blk_start = jnp.arange(n_blocks, dtype=I32) * bm
    blk_exp = jnp.minimum(jnp.sum(blk_start[:, None] >= pad_ends[None, :], axis=1), N_EXPERTS - 1).astype(I32)
    n_used = (pad_ends[-1:] // bm).astype(I32)
    xs = _sc_scatter_rows(x_packed, dest, n_blocks * bm)
    ys = _experts(xs, blk_exp, n_used, layer, w1, b1, w2, b2)
    yg = _sc_gather_rows(ys, dest.reshape(-1)).reshape(TOP_K, n, d // 2)
    return yg, gate.T


def _positions(groups):
    if len(groups) == 1:
        return jnp.arange(groups[0][1])
    return jnp.concatenate([jnp.tile(jnp.arange(s), b) for b, s in groups])


def _axial_tables(groups):
    t = _positions(groups)
    n = HEAD_DIM // 4
    inv = AXIAL_THETA ** (-jnp.arange(n, dtype=F32) / n)
    ar = (t // GRID_W).astype(F32)[:, None] * inv
    ac = (t % GRID_W).astype(F32)[:, None] * inv
    z = jnp.zeros_like(ar)
    cr, sr, cc, sc = jnp.cos(ar), jnp.sin(ar), jnp.cos(ac), jnp.sin(ac)
    c = jnp.concatenate([cr, cr, cc, cc], axis=-1)
    s1 = jnp.concatenate([z, sr, z, sc], axis=-1)
    s2 = jnp.concatenate([-sr, z, -sc, z], axis=-1)
    return tuple(jnp.tile(a, (1, 2)) for a in (c, s1, s2))


def _rope_tables(groups):
    t = _positions(groups)
    n = ROPE_DIMS // 2
    inv = ROPE_THETA ** (-jnp.arange(n, dtype=F32) / n)
    ang = t.astype(F32)[:, None] * inv
    c, s = jnp.cos(ang), jnp.sin(ang)
    z = jnp.zeros_like(c)
    rest = HEAD_DIM - ROPE_DIMS
    pad1 = jnp.ones((t.shape[0], rest), F32)
    pad0 = jnp.zeros((t.shape[0], rest), F32)
    cc = jnp.concatenate([c, c, pad1], axis=-1)
    s1 = jnp.concatenate([z, s, pad0], axis=-1)
    s2 = jnp.concatenate([-s, z, pad0], axis=-1)
    return tuple(jnp.tile(a, (1, 2)) for a in (cc, s1, s2))


def _gqa_head_order():
    g = GQA_Q_HEADS // GQA_KV_HEADS
    return [h for j in range(g) for h in (j, g + j)]


def _mixer_even(x, groups, tabs, w_in, rpb, q_gain, k_gain, w_out):
    hd = HEAD_DIM
    na_w = NA_HEADS * hd
    order = _gqa_head_order()
    q0 = 3 * na_w
    q_cols = np.concatenate([q0 + h * hd + np.arange(hd) for h in order])
    w_in_p = jnp.concatenate([w_in[:, :q0], w_in[:, q_cols], w_in[:, q0 + GQA_Q_HEADS * hd:]], axis=1).astype(BF16)
    out_rows = np.concatenate([na_w + h * hd + np.arange(hd) for h in order])
    w_out_p = jnp.concatenate([w_out[:na_w], w_out[out_rows]], axis=0).astype(BF16)
    n_na = na_w // LANES
    n_q = GQA_Q_HEADS * hd // LANES
    modes = ([("plain", Q_SCALE, 0)] * n_na + [("plain", 1.0, 0)] * (2 * n_na)
             + [("norm_rope", Q_SCALE, 0)] * n_q + [("norm_rope", 1.0, 1)] + [("plain", 1.0, 0)])
    gains = jnp.stack([jnp.tile(q_gain.astype(F32), 2), jnp.tile(k_gain.astype(F32), 2)])
    proj = _project(x, w_in_p, tabs, gains, modes, HEAD_DIM // 4)
    cc = _na_bias_table(rpb)
    ya, yb = [], []
    row0 = 0
    for b, s in groups:
        ya.append(_na_attention(proj, cc, row0, b, s))
        yb.append(_gqa_attention(proj, row0, b, s))
        row0 += b * s
    return [(jnp.concatenate(ya, axis=0), w_out_p[:na_w]), (jnp.concatenate(yb, axis=0), w_out_p[na_w:])]


def _mixer_odd(x, groups, tabs, w_in, w_out):
    n_slab = DIL_HEADS * HEAD_DIM // LANES
    modes = [("rope", Q_SCALE, 0)] * n_slab + [("rope", 1.0, 0)] * n_slab + [("plain", 1.0, 0)] * n_slab
    gains = jnp.ones((1, LANES), F32)
    qkv = _project(x, w_in.astype(BF16), tabs, gains, modes, ROPE_DIMS // 2, out_dtype=F32)
    parts, row0 = [], 0
    for b, s in groups:
        parts.append(_dilated_attention(qkv, row0, b, s))
        row0 += b * s
    return [(jnp.concatenate(parts, axis=0), w_out.astype(BF16))]


def _trunk(xs, w_in_even, rpb_a, q_gain_b, k_gain_b, w_out_even, w_in_odd, w_out_odd,
           ln1_g, ln1_b, ln2_g, ln2_b, router_w, router_b, moe_w1, moe_b1, moe_w2, moe_b2):
    groups = [(x.shape[0], x.shape[1]) for x in xs]
    x = jnp.concatenate([t.reshape(-1, D_MODEL) for t in xs], axis=0).astype(F32)
    tabs_even = _axial_tables(groups)
    tabs_odd = _rope_tables(groups)
    b1 = moe_b1.astype(F32)[:, :, None, :]
    b2 = moe_b2.astype(F32)[:, :, None, :]
    for l in range(DEPTH):
        i = l // 2
        if l % 2 == 0:
            parts = _mixer_even(x, groups, tabs_even, w_in_even[i], rpb_a[i], q_gain_b[i], k_gain_b[i],
                                w_out_even[i])
        else:
            parts = _mixer_odd(x, groups, tabs_odd, w_in_odd[i], w_out_odd[i])
        x, x_packed, idx, gate, rank, counts = _outproj_ln_route(parts, x, ln1_g[l], ln1_b[l], router_w[l],
                                                                 router_b[l])
        yg, gate = _moe(x_packed, idx, gate, rank, counts, l, moe_w1, b1, moe_w2, b2)
        if l + 1 < DEPTH:
            x = _combine_ln(yg, gate, x, ln2_g[l], ln2_b[l])
    outs, row0 = [], 0
    for t in xs:
        n = t.shape[0] * t.shape[1]
        outs.append(_combine_ln(yg, gate, x, ln2_g[-1], ln2_b[-1], row0, n).reshape(t.shape))
        row0 += n
    return tuple(outs)


def kernel(x_prompt, x_sample, w_in_even, rpb_a, q_gain_b, k_gain_b, w_out_even, w_in_odd, w_out_odd, ln1_g, ln1_b,
           ln2_g, ln2_b, router_w, router_b, moe_w1, moe_b1, moe_w2, moe_b2):
    params = (w_in_even, rpb_a, q_gain_b, k_gain_b, w_out_even, w_in_odd, w_out_odd,
              ln1_g, ln1_b, ln2_g, ln2_b, router_w, router_b, moe_w1.astype(BF16), moe_b1, moe_w2.astype(BF16), moe_b2)
    (y_prompt,) = _trunk((x_prompt,), *params)
    (y_sample,) = _trunk((x_sample,), *params)
    return y_prompt, y_sample
```

```python
import functools

import jax
import jax.numpy as jnp
import numpy as np
from jax import lax
from jax.experimental import pallas as pl
from jax.experimental.pallas import tpu as pltpu
from jax.experimental.pallas import tpu_sc as plsc

F32 = jnp.float32
BF16 = jnp.bfloat16
I32 = jnp.int32

D_MODEL = 1024
DEPTH = 4
HEAD_DIM = 64
GRID_W = 64
NA_HEADS = 8
NA_ROWS = 8
NA_COLS = 16
GQA_Q_HEADS = 8
GQA_KV_HEADS = 2
AXIAL_THETA = 10000.0
QK_NORM_EPS = 1e-6
DIL_HEADS = 16
DIL_BRANCHES = ((128, 1), (512, 4), (2048, 16))
ROPE_THETA = 500000.0
ROPE_DIMS = HEAD_DIM // 4
N_EXPERTS = 32
TOP_K = 4
D_FF = D_MODEL
SWIGLU_LIMIT = 7.0
SWIGLU_ALPHA = 1.702
DN_ALPHA = (2 * DEPTH) ** 0.25
LN_EPS = 1e-5
LOG2E = 1.4426950408889634
Q_SCALE = HEAD_DIM ** -0.5 * LOG2E

LANES = 128
NEG = -1e30
VMEM_LIMIT = 56 * 1024 * 1024
TOKEN_TILE = 512
COMBINE_TILE = 1024
EXPERT_ROWS = 512
NA_ROW_BLOCK = 16
NA_ROW_GROUP = 8
GQA_Q_TILE = 512
GQA_K_TILE = 512
GQA_DEN_ROWS = 16
WIN_Q_TILE = 128
WIN_RADIUS = 64
DIL_GROUP = 8
SC_INDEX_WINDOW = 128
SC_ROW_WINDOW = 64

EVEN_IN = 3 * NA_HEADS * HEAD_DIM + GQA_Q_HEADS * HEAD_DIM + 2 * GQA_KV_HEADS * HEAD_DIM


def _params(*sem):
    return pltpu.CompilerParams(dimension_semantics=sem, vmem_limit_bytes=VMEM_LIMIT)


def _lane_is_low():
    return lax.broadcasted_iota(I32, (1, LANES), 1) < HEAD_DIM


def _proj_kernel(x_ref, w_ref, c_ref, s1_ref, s2_ref, gain_ref, o_ref, *, slab_modes, shift):
    x = x_ref[...].astype(BF16)
    n_out = o_ref.shape[1]
    chunk = 2 * LANES
    if any(m[0] == "norm_rope" for m in slab_modes):
        r = lax.broadcasted_iota(I32, (LANES, LANES), 0) // HEAD_DIM
        c = lax.broadcasted_iota(I32, (LANES, LANES), 1) // HEAD_DIM
        head_mean = jnp.where(r == c, 1.0 / HEAD_DIM, 0.0).astype(BF16)
    for c0 in range(0, n_out, chunk):
        acc = jnp.dot(x, w_ref[:, c0:c0 + chunk], preferred_element_type=F32)
        for s in range(chunk // LANES):
            slab = c0 // LANES + s
            mode, scale, gidx = slab_modes[slab]
            y = acc[:, s * LANES:(s + 1) * LANES]
            if mode == "norm_rope":
                sq = y * y
                hi = sq.astype(BF16)
                lo = (sq - hi.astype(F32)).astype(BF16)
                ms = (jnp.dot(hi, head_mean, preferred_element_type=F32)
                      + jnp.dot(lo, head_mean, preferred_element_type=F32))
                y = y * lax.rsqrt(ms + QK_NORM_EPS) * gain_ref[gidx:gidx + 1, :]
            if mode in ("rope", "norm_rope"):
                y = (y * c_ref[...] + pltpu.roll(y, LANES - shift, 1) * s2_ref[...]
                     + pltpu.roll(y, shift, 1) * s1_ref[...])
            if scale != 1.0:
                y = y * scale
            o_ref[:, slab * LANES:(slab + 1) * LANES] = y.astype(o_ref.dtype)


def _project(x, w, tabs, gains, slab_modes, shift, out_dtype=BF16):
    n, d = x.shape
    m = w.shape[1]
    tm = TOKEN_TILE
    tab_blocks = tabs[0].shape[0] // tm
    tab_spec = pl.BlockSpec((tm, LANES), lambda i: (i % tab_blocks, 0))
    return pl.pallas_call(
        functools.partial(_proj_kernel, slab_modes=tuple(slab_modes), shift=shift),
        grid=(n // tm,),
        in_specs=[pl.BlockSpec((tm, d), lambda i: (i, 0)),
                  pl.BlockSpec((d, m), lambda i: (0, 0)),
                  tab_spec, tab_spec, tab_spec,
                  pl.BlockSpec(gains.shape, lambda i: (0, 0))],
        out_specs=pl.BlockSpec((tm, m), lambda i: (i, 0)),
        out_shape=jax.ShapeDtypeStruct((n, m), out_dtype),
        compiler_params=_params("parallel"),
        name="in_proj",
    )(x, w, tabs[0], tabs[1], tabs[2], gains)


def _layer_norm_rows(z, g, b):
    mu = jnp.mean(z, axis=-1, keepdims=True)
    zc = z - mu
    var = jnp.mean(zc * zc, axis=-1, keepdims=True)
    return zc * lax.rsqrt(var + LN_EPS) * g + b


def _pack_bf16_pairs(v):
    half = v.shape[1] // 2
    bits = pltpu.bitcast(v.astype(BF16).astype(F32), jnp.uint32)
    return (bits[:, :half] >> 16) | bits[:, half:]


def _unpack_bf16_pairs(w):
    lo = pltpu.bitcast(w << 16, F32)
    hi = pltpu.bitcast(w & jnp.uint32(0xFFFF0000), F32)
    return lo, hi


def _route_tile(x, wh_ref, wl_ref, rb_ref, base_ref, idx_ref, gate_ref, rank_ref, cnt_ref):
    xh = x.astype(BF16)
    xl = (x - xh.astype(F32)).astype(BF16)
    nt = (((1,), (1,)), ((), ()))
    logits = (lax.dot_general(wh_ref[...], xh, nt, preferred_element_type=F32)
              + lax.dot_general(wl_ref[...], xh, nt, preferred_element_type=F32)
              + lax.dot_general(wh_ref[...], xl, nt, preferred_element_type=F32)) + rb_ref[...]
    tm = x.shape[0]
    eid = lax.broadcasted_iota(I32, (N_EXPERTS, tm), 0)
    vals = logits
    top_v, top_i, hots = [], [], []
    for _ in range(TOP_K):
        m = jnp.max(vals, axis=0, keepdims=True)
        idx = jnp.min(jnp.where(vals == m, eid, N_EXPERTS), axis=0, keepdims=True)
        hot = eid == idx
        top_v.append(m)
        top_i.append(idx)
        hots.append(hot)
        vals = jnp.where(hot, -jnp.inf, vals)
    es = [jnp.exp(v - top_v[0]) for v in top_v]
    den = functools.reduce(jnp.add, es)
    chosen = functools.reduce(jnp.logical_or, hots)
    before = (lax.broadcasted_iota(I32, (tm, tm), 0) < lax.broadcasted_iota(I32, (tm, tm), 1))
    prefix = jnp.dot(chosen.astype(BF16), before.astype(BF16), preferred_element_type=F32) + base_ref[...]
    for k in range(TOP_K):
        idx_ref[k:k + 1, :] = top_i[k]
        gate_ref[k:k + 1, :] = es[k] / den
        rank_ref[k:k + 1, :] = jnp.sum(jnp.where(hots[k], prefix, 0.0), axis=0, keepdims=True).astype(I32)
    base_ref[...] = base_ref[...] + jnp.sum(chosen.astype(F32), axis=1, keepdims=True)
    cnt_ref[...] = base_ref[...].astype(I32)


def _outproj_ln_kernel(*refs, n_parts):
    a_refs, w_refs = refs[:n_parts], refs[n_parts:2 * n_parts]
    x_ref, g_ref, b_ref, wh_ref, wl_ref, rb_ref = refs[2 * n_parts:2 * n_parts + 6]
    o_ref, p_ref, idx_ref, gate_ref, rank_ref, cnt_ref, base_ref = refs[2 * n_parts + 6:]

    @pl.when(pl.program_id(0) == 0)
    def _():
        base_ref[...] = jnp.zeros_like(base_ref)

    y = functools.reduce(jnp.add, [jnp.dot(a[...], w[...], preferred_element_type=F32)
                                   for a, w in zip(a_refs, w_refs)])
    out = _layer_norm_rows(DN_ALPHA * x_ref[...] + y, g_ref[...], b_ref[...])
    o_ref[...] = out
    p_ref[...] = _pack_bf16_pairs(out)
    _route_tile(out, wh_ref, wl_ref, rb_ref, base_ref, idx_ref, gate_ref, rank_ref, cnt_ref)


def _outproj_ln_route(parts, x, g, b, w_r, b_r):
    n, d = x.shape
    tm = TOKEN_TILE
    row = lambda i: (i, 0)
    fixed = lambda i: (0, 0)
    tok = pl.BlockSpec((TOP_K, tm), lambda i: (0, i))
    wr_t = w_r.T
    wr_hi = wr_t.astype(BF16)
    wr_lo = (wr_t - wr_hi.astype(F32)).astype(BF16)
    expert_vec = pl.BlockSpec((N_EXPERTS, 1), fixed)
    return pl.pallas_call(
        functools.partial(_outproj_ln_kernel, n_parts=len(parts)),
        grid=(n // tm,),
        in_specs=([pl.BlockSpec((tm, a.shape[1]), row) for a, _ in parts]
                  + [pl.BlockSpec(w.shape, fixed) for _, w in parts]
                  + [pl.BlockSpec((tm, d), row), pl.BlockSpec((1, d), fixed), pl.BlockSpec((1, d), fixed),
                     pl.BlockSpec((N_EXPERTS, d), fixed), pl.BlockSpec((N_EXPERTS, d), fixed), expert_vec]),
        out_specs=[pl.BlockSpec((tm, d), row), pl.BlockSpec((tm, d // 2), row), tok, tok, tok, expert_vec],
        out_shape=[jax.ShapeDtypeStruct((n, d), F32), jax.ShapeDtypeStruct((n, d // 2), jnp.uint32),
                   jax.ShapeDtypeStruct((TOP_K, n), I32), jax.ShapeDtypeStruct((TOP_K, n), F32),
                   jax.ShapeDtypeStruct((TOP_K, n), I32), jax.ShapeDtypeStruct((N_EXPERTS, 1), I32)],
        scratch_shapes=[pltpu.VMEM((N_EXPERTS, 1), F32)],
        compiler_params=_params("arbitrary"),
        name="out_proj_ln_route",
    )(*[a for a, _ in parts], *[w for _, w in parts], x, g.reshape(1, d), b.reshape(1, d), wr_hi, wr_lo,
      b_r.reshape(N_EXPERTS, 1).astype(F32))


def _na_kernel(q_ref, k_ref, v_ref, cc_ref, o_ref, *, rows):
    i = pl.program_id(2)
    low = _lane_is_low()

    for g0 in range(0, NA_ROW_BLOCK, NA_ROW_GROUP):
        wins, scores = [], []
        for rr in range(g0, g0 + NA_ROW_GROUP):
            r = i * NA_ROW_BLOCK + rr
            rs = jnp.clip(r - NA_ROWS // 2, 0, rows - NA_ROWS)
            var = r - rs
            q = q_ref[rr * GRID_W:(rr + 1) * GRID_W, :]
            kstart = pl.multiple_of(rs * GRID_W, GRID_W)
            kwin = k_ref[pl.ds(kstart, NA_ROWS * GRID_W), :]
            wins.append(kstart)
            for half in range(2):
                qm = jnp.where(low if half == 0 else jnp.logical_not(low), q, jnp.zeros_like(q))
                s = lax.dot_general(qm, kwin, (((1,), (1,)), ((), ())), preferred_element_type=F32)
                scores.append(s + cc_ref[half, var])
        probs = []
        for s in scores:
            p = jnp.exp2(s - jnp.max(s, axis=-1, keepdims=True))
            probs.append((p.astype(BF16), jnp.sum(p, axis=-1, keepdims=True)))
        for u, rr in enumerate(range(g0, g0 + NA_ROW_GROUP)):
            vwin = v_ref[pl.ds(wins[u], NA_ROWS * GRID_W), :]
            outs = [jnp.dot(p, vwin, preferred_element_type=F32) / l for p, l in probs[2 * u:2 * u + 2]]
            o_ref[rr * GRID_W:(rr + 1) * GRID_W, :] = jnp.where(low, outs[0], outs[1]).astype(o_ref.dtype)


def _na_attention(proj, cc, row0, b, s):
    rows = s // GRID_W
    assert rows >= NA_ROWS and rows % NA_ROW_BLOCK == 0 and row0 % s == 0
    tq = NA_ROW_BLOCK * GRID_W
    n_slab = NA_HEADS * HEAD_DIM // LANES
    qb0, sb0 = row0 // tq, row0 // s
    return pl.pallas_call(
        functools.partial(_na_kernel, rows=rows),
        grid=(b, n_slab, s // tq),
        in_specs=[pl.BlockSpec((tq, LANES), lambda bi, j, i: (qb0 + bi * (s // tq) + i, j)),
                  pl.BlockSpec((s, LANES), lambda bi, j, i: (sb0 + bi, n_slab + j)),
                  pl.BlockSpec((s, LANES), lambda bi, j, i: (sb0 + bi, 2 * n_slab + j)),
                  pl.BlockSpec((2, NA_ROWS, GRID_W, NA_ROWS * GRID_W), lambda bi, j, i: (j, 0, 0, 0))],
        out_specs=pl.BlockSpec((tq, LANES), lambda bi, j, i: (bi * (s // tq) + i, j)),
        out_shape=jax.ShapeDtypeStruct((b * s, n_slab * LANES), BF16),
        compiler_params=_params("parallel", "parallel", "parallel"),
        name="na_attn",
    )(proj, proj, proj, cc)


def _na_bias_table(rpb):
    var = np.arange(NA_ROWS)[:, None]
    j = np.arange(NA_ROWS)[None, :]
    qc = np.arange(GRID_W)[:, None]
    kc = np.arange(GRID_W)[None, :]
    row_sel = ((j - var + NA_ROWS - 1)[..., None] == np.arange(2 * NA_ROWS - 1)).astype(np.float32)
    col_sel = (np.clip(kc - qc + NA_COLS - 1, 0, 2 * NA_COLS - 2)[..., None]
               == np.arange(2 * NA_COLS - 1)).astype(np.float32)
    win = np.clip(qc - NA_COLS // 2, 0, GRID_W - NA_COLS)
    valid = (kc >= win) & (kc < win + NA_COLS)
    t = jnp.einsum("hab,vja,qkb->hvqjk", rpb.astype(F32), row_sel, col_sel, precision=lax.Precision.HIGHEST)
    t = jnp.where(valid[None, None, :, None, :], t * LOG2E, NEG)
    return t.reshape(rpb.shape[0], NA_ROWS, GRID_W, NA_ROWS * GRID_W)


def _gqa_kernel(q_ref, k_ref, vt_ref, o_ref, acc_ref, st_ref, *, n_kt):
    assert n_kt % 2 == 0
    low = _lane_is_low()
    q = q_ref[...]
    tq = q.shape[0]
    zero = jnp.zeros_like(q)
    qs = jnp.concatenate([jnp.where(low, q, zero), jnp.where(low, zero, q)], axis=0)
    acc_ref[...] = jnp.zeros_like(acc_ref)

    def scores(kt, slot):
        start = pl.multiple_of(kt * GQA_K_TILE, GQA_K_TILE)
        kb = k_ref[pl.ds(start, GQA_K_TILE), :]
        st_ref[slot] = lax.dot_general(kb, qs, (((1,), (1,)), ((), ())), preferred_element_type=F32)

    def softmax_pv(kt, slot, m_old):
        st = st_ref[slot]
        m_new = jnp.maximum(m_old, jnp.max(st, axis=0, keepdims=True))
        alpha = jnp.exp2(m_old - m_new)
        pt = jnp.exp2(st - m_new).astype(BF16)
        acc_ref[...] = alpha * acc_ref[...] + jnp.dot(vt_ref[kt], pt, preferred_element_type=F32)
        return m_new

    def step(i, m):
        kt = 2 * i
        scores(kt + 1, 1)
        m = softmax_pv(kt, 0, m)
        scores(jnp.minimum(kt + 2, n_kt - 1), 0)
        return softmax_pv(kt + 1, 1, m)

    scores(0, 0)
    lax.fori_loop(0, n_kt // 2, step, jnp.full((1, 2 * tq), -jnp.inf, F32))
    out_t = acc_ref[:LANES, :] / acc_ref[LANES:LANES + 1, :]
    o_t = jnp.concatenate([out_t[:HEAD_DIM, :tq], out_t[HEAD_DIM:, tq:]], axis=0)
    o_ref[...] = o_t.T.astype(o_ref.dtype)


def _gqa_attention(proj, row0, b, s):
    tq, tk = GQA_Q_TILE, GQA_K_TILE
    assert s % tk == 0 and row0 % s == 0
    n_slab = GQA_Q_HEADS * HEAD_DIM // LANES
    q_col = 3 * NA_HEADS * HEAD_DIM // LANES
    k_col = q_col + n_slab
    v0 = (k_col + 1) * LANES
    qb0, sb0 = row0 // tq, row0 // s
    v_t = proj[row0:row0 + b * s, v0:v0 + LANES].reshape(b, s // tk, tk, LANES).transpose(0, 1, 3, 2)
    ones_rows = jnp.zeros((b, s // tk, GQA_DEN_ROWS, tk), BF16).at[:, :, 0, :].set(1.0)
    v_t = jnp.concatenate([v_t, ones_rows], axis=2)
    vt_rows = LANES + GQA_DEN_ROWS
    return pl.pallas_call(
        functools.partial(_gqa_kernel, n_kt=s // tk),
        grid=(b, n_slab, s // tq),
        in_specs=[pl.BlockSpec((tq, LANES), lambda bi, j, i: (qb0 + bi * (s // tq) + i, q_col + j)),
                  pl.BlockSpec((s, LANES), lambda bi, j, i: (sb0 + bi, k_col)),
                  pl.BlockSpec((None, s // tk, vt_rows, tk), lambda bi, j, i: (bi, 0, 0, 0))],
        out_specs=pl.BlockSpec((tq, LANES), lambda bi, j, i: (bi * (s // tq) + i, j)),
        out_shape=jax.ShapeDtypeStruct((b * s, n_slab * LANES), BF16),
        scratch_shapes=[pltpu.VMEM((vt_rows, 2 * tq), F32), pltpu.VMEM((2, tk, 2 * tq), F32)],
        compiler_params=_params("parallel", "parallel", "parallel"),
        name="gqa_attn",
    )(proj, proj, v_t)


def _dil_kernel(q_ref, k_ref, v_ref, o_ref, m_ref, l_ref, acc_ref, *kv_refs, seq):
    tile = q_ref.shape[0]
    t0 = pl.program_id(2) * tile
    low = _lane_is_low()
    sub = WIN_Q_TILE
    n_sub = tile // sub

    @pl.when(pl.program_id(2) == 0)
    def _():
        for bi, (_, r) in enumerate(DIL_BRANCHES):
            for c in range(r):
                rows = pl.ds(c, seq // r, stride=r) if r > 1 else pl.ds(0, seq)
                kv_refs[2 * bi][c] = k_ref[rows, :].astype(BF16)
                kv_refs[2 * bi + 1][c] = v_ref[rows, :].astype(BF16)

    for bi, (_, r) in enumerate(DIL_BRANCHES):
        length = seq // r
        win = min(sub + 2 * WIN_RADIUS, length)
        offset = lax.broadcasted_iota(I32, (sub, win), 1) - lax.broadcasted_iota(I32, (sub, win), 0)
        kc_ref, vc_ref = kv_refs[2 * bi], kv_refs[2 * bi + 1]

        def sub_blocks(g, carry, r=r, length=length, win=win, first=(bi == 0), offset=offset, kc_ref=kc_ref,
                       vc_ref=vc_ref):
            rows, scores, stats = [], [], []
            for u in range(DIL_GROUP):
                n = g * DIL_GROUP + u
                c = n % r
                blk = n // r
                u0 = t0 // r + blk * sub
                ks = pl.multiple_of(jnp.clip(u0 - WIN_RADIUS, 0, length - win), WIN_RADIUS)
                if r == 1:
                    q_rows = pl.ds(pl.multiple_of(n * sub, sub), sub)
                else:
                    q_rows = pl.ds(c + r * blk * sub, sub, stride=r)
                rows.append((q_rows, c, ks))
                q = q_ref[q_rows, :].astype(BF16)
                kwin = kc_ref[c, pl.ds(ks, win), :]
                valid = jnp.abs(offset + (ks - u0)) <= WIN_RADIUS
                for half in range(2):
                    qm = jnp.where(low if half == 0 else jnp.logical_not(low), q, jnp.zeros_like(q))
                    s = lax.dot_general(qm, kwin, (((1,), (1,)), ((), ())), preferred_element_type=F32)
                    scores.append(jnp.where(valid, s, NEG))
            for s in scores:
                m = jnp.max(s, axis=-1, keepdims=True)
                p = jnp.exp2(s - m)
                stats.append((m, jnp.sum(p, axis=-1, keepdims=True), p.astype(BF16)))
            for u, (q_rows, c, ks) in enumerate(rows):
                vwin = vc_ref[c, pl.ds(ks, win), :]
                (m0, l0, p0), (m1, l1, p1) = stats[2 * u:2 * u + 2]
                m_b = jnp.where(low, m0, m1)
                l_b = jnp.where(low, l0, l1)
                pv_b = jnp.where(low, jnp.dot(p0, vwin, preferred_element_type=F32),
                                 jnp.dot(p1, vwin, preferred_element_type=F32))
                if first:
                    m_ref[q_rows, :] = m_b
                    l_ref[q_rows, :] = l_b
                    acc_ref[q_rows, :] = pv_b
                else:
                    m_old = m_ref[q_rows, :]
                    m_new = jnp.maximum(m_old, m_b)
                    a_old = jnp.exp2(m_old - m_new)
                    a_b = jnp.exp2(m_b - m_new)
                    l_ref[q_rows, :] = a_old * l_ref[q_rows, :] + a_b * l_b
                    acc_ref[q_rows, :] = a_old * acc_ref[q_rows, :] + a_b * pv_b
                    m_ref[q_rows, :] = m_new
            return carry

        lax.fori_loop(0, n_sub // DIL_GROUP, sub_blocks, 0)

    o_ref[...] = (acc_ref[...] / l_ref[...]).astype(o_ref.dtype)


def _dilated_attention(qkv, row0, b, s):
    tile = 16 * WIN_Q_TILE
    assert s % tile == 0 and row0 % s == 0
    n_slab = DIL_HEADS * HEAD_DIM // LANES
    qb0, sb0 = row0 // tile, row0 // s
    state = pltpu.VMEM((tile, LANES), F32)
    by_class = [pltpu.VMEM((r, s // r, LANES), BF16) for _, r in DIL_BRANCHES for _ in ("k", "v")]
    return pl.pallas_call(
        functools.partial(_dil_kernel, seq=s),
        grid=(b, n_slab, s // tile),
        in_specs=[pl.BlockSpec((tile, LANES), lambda bi, j, i: (qb0 + bi * (s // tile) + i, j)),
                  pl.BlockSpec((s, LANES), lambda bi, j, i: (sb0 + bi, n_slab + j)),
                  pl.BlockSpec((s, LANES), lambda bi, j, i: (sb0 + bi, 2 * n_slab + j))],
        out_specs=pl.BlockSpec((tile, LANES), lambda bi, j, i: (bi * (s // tile) + i, j)),
        out_shape=jax.ShapeDtypeStruct((b * s, n_slab * LANES), BF16),
        scratch_shapes=[state, state, state] + by_class,
        compiler_params=_params("parallel", "parallel", "arbitrary"),
        name="dil_attn",
    )(qkv, qkv, qkv)


def _expert_kernel(blk_exp_ref, rows_ref, x_ref, w1_ref, b1_ref, w2_ref, b2_ref, o_ref):
    used = rows_ref[pl.program_id(0)]
    half = x_ref.shape[0] // 2

    def mlp(packed):
        x_lo, x_hi = _unpack_bf16_pairs(packed)
        x = jnp.concatenate([x_lo.astype(BF16), x_hi.astype(BF16)], axis=1)
        h = jnp.dot(x, w1_ref[...], preferred_element_type=F32) + b1_ref[...]
        g = jnp.minimum(h[:, :D_FF], SWIGLU_LIMIT)
        u = jnp.clip(h[:, D_FF:], -SWIGLU_LIMIT, SWIGLU_LIMIT)
        act = g * jax.nn.sigmoid(SWIGLU_ALPHA * g) * (u + 1.0)
        y = jnp.dot(act.astype(BF16), w2_ref[...], preferred_element_type=F32) + b2_ref[...]
        return _pack_bf16_pairs(y)

    @pl.when(used > half)
    def _():
        o_ref[...] = mlp(x_ref[...])

    @pl.when(jnp.logical_and(used > 0, used <= half))
    def _():
        o_ref[:half, :] = mlp(x_ref[:half, :])
        o_ref[half:, :] = jnp.zeros((half, o_ref.shape[1]), o_ref.dtype)

    @pl.when(used == 0)
    def _():
        o_ref[...] = jnp.zeros_like(o_ref)


def _experts(xs, blk_exp, blk_rows, layer, w1, b1, w2, b2):
    n_rows = xs.shape[0]
    d = D_MODEL
    bm = EXPERT_ROWS
    f2 = w1.shape[3]
    wmap = lambda i, be, nu: (layer, be[i], 0, 0)
    return pl.pallas_call(
        _expert_kernel,
        grid_spec=pltpu.PrefetchScalarGridSpec(
            num_scalar_prefetch=2,
            grid=(n_rows // bm,),
            in_specs=[pl.BlockSpec((bm, d // 2), lambda i, be, nu: (i, 0)),
                      pl.BlockSpec((None, None, d, f2), wmap),
                      pl.BlockSpec((None, None, 1, f2), wmap),
                      pl.BlockSpec((None, None, f2 // 2, d), wmap),
                      pl.BlockSpec((None, None, 1, d), wmap)],
            out_specs=pl.BlockSpec((bm, d // 2), lambda i, be, nu: (i, 0)),
        ),
        out_shape=jax.ShapeDtypeStruct((n_rows, d // 2), jnp.uint32),
        compiler_params=_params("arbitrary"),
        name="experts",
    )(blk_exp, blk_rows, xs, w1, b1, w2, b2)


def _combine_ln_kernel(y_ref, gate_ref, x_ref, g_ref, b_ref, o_ref):
    gate = gate_ref[...]
    halves = [_unpack_bf16_pairs(y_ref[k]) for k in range(TOP_K)]
    y = jnp.concatenate([functools.reduce(jnp.add, [gate[:, k:k + 1] * halves[k][h] for k in range(TOP_K)])
                         for h in range(2)], axis=1)
    o_ref[...] = _layer_norm_rows(DN_ALPHA * x_ref[...] + y, g_ref[...], b_ref[...])


def _combine_ln(yg, gate, x, g, b, row0=0, n_rows=None):
    n, d = x.shape
    n_rows = n if n_rows is None else n_rows
    tm = COMBINE_TILE
    assert row0 % tm == 0 and n_rows % tm == 0
    blk0 = row0 // tm
    row = lambda i: (blk0 + i, 0)
    fixed = lambda i: (0, 0)
    return pl.pallas_call(
        _combine_ln_kernel,
        grid=(n_rows // tm,),
        in_specs=[pl.BlockSpec((TOP_K, tm, d // 2), lambda i: (0, blk0 + i, 0)), pl.BlockSpec((tm, TOP_K), row),
                  pl.BlockSpec((tm, d), row), pl.BlockSpec((1, d), fixed), pl.BlockSpec((1, d), fixed)],
        out_specs=pl.BlockSpec((tm, d), lambda i: (i, 0)),
        out_shape=jax.ShapeDtypeStruct((n_rows, d), F32),
        compiler_params=_params("parallel"),
        name="combine_ln",
    )(yg, gate, x, g.reshape(1, d), b.reshape(1, d))


def _sc_mesh():
    return plsc.VectorSubcoreMesh(core_axis_name="core", subcore_axis_name="subcore")


def _sc_scatter_rows(x, dest, n_out):
    n, d = x.shape
    mesh = _sc_mesh()
    per_worker = n // (mesh.num_cores * mesh.num_subcores)
    n_sub = SC_INDEX_WINDOW // SC_ROW_WINDOW
    assert per_worker % SC_INDEX_WINDOW == 0 and n_sub >= 2

    @functools.partial(
        pl.kernel, out_type=jax.ShapeDtypeStruct((n_out, d), x.dtype), mesh=mesh,
        scratch_types=[pltpu.VMEM((TOP_K, SC_INDEX_WINDOW), I32), pltpu.VMEM((2, SC_ROW_WINDOW, d), x.dtype),
                       pltpu.SemaphoreType.DMA((2,)), pltpu.SemaphoreType.DMA((2,))])
    def scatter(x_hbm, i_hbm, o_hbm, idx_v, buf, sem_r, sem_s):
        wid = lax.axis_index("core") * mesh.num_subcores + lax.axis_index("subcore")

        @pl.loop(0, per_worker // SC_INDEX_WINDOW)
        def _(it):
            base = wid * per_worker + it * SC_INDEX_WINDOW
            for k in range(TOP_K):
                pltpu.sync_copy(i_hbm.at[k, pl.ds(base, SC_INDEX_WINDOW)], idx_v.at[k])

            def read(j):
                rows = pl.ds(base + j * SC_ROW_WINDOW, SC_ROW_WINDOW)
                return pltpu.make_async_copy(x_hbm.at[rows], buf.at[j % 2], sem_r.at[j % 2])

            def send(j, k):
                rows = idx_v.at[k, pl.ds(j * SC_ROW_WINDOW, SC_ROW_WINDOW)]
                return pltpu.make_async_copy(buf.at[j % 2], o_hbm.at[rows], sem_s.at[j % 2])

            read(0).start()
            for j in range(n_sub):
                read(j).wait()
                for k in range(TOP_K):
                    send(j, k).start()
                if j + 1 < n_sub:
                    if j >= 1:
                        for k in range(TOP_K):
                            send(j - 1, k).wait()
                    read(j + 1).start()
            for j in (n_sub - 2, n_sub - 1):
                for k in range(TOP_K):
                    send(j, k).wait()

    return scatter(x, dest)


def _sc_gather_rows(table, indices):
    num = indices.shape[0]
    d = table.shape[1]
    mesh = _sc_mesh()
    per_worker = num // (mesh.num_cores * mesh.num_subcores)
    n_sub = SC_INDEX_WINDOW // SC_ROW_WINDOW
    assert per_worker % SC_INDEX_WINDOW == 0 and n_sub >= 2

    @functools.partial(
        pl.kernel, out_type=jax.ShapeDtypeStruct((num, d), table.dtype), mesh=mesh,
        scratch_types=[pltpu.VMEM((SC_INDEX_WINDOW,), I32), pltpu.VMEM((2, SC_ROW_WINDOW, d), table.dtype),
                       pltpu.SemaphoreType.DMA((2,)), pltpu.SemaphoreType.DMA((2,))])
    def gather(x_hbm, i_hbm, o_hbm, idx_v, buf, sem_g, sem_w):
        wid = lax.axis_index("core") * mesh.num_subcores + lax.axis_index("subcore")

        @pl.loop(0, per_worker // SC_INDEX_WINDOW)
        def _(it):
            base = wid * per_worker + it * SC_INDEX_WINDOW
            pltpu.sync_copy(i_hbm.at[pl.ds(base, SC_INDEX_WINDOW)], idx_v)

            def fetch(j):
                rows = idx_v.at[pl.ds(j * SC_ROW_WINDOW, SC_ROW_WINDOW)]
                return pltpu.make_async_copy(x_hbm.at[rows], buf.at[j % 2], sem_g.at[j % 2])

            def write(j):
                rows = pl.ds(base + j * SC_ROW_WINDOW, SC_ROW_WINDOW)
                return pltpu.make_async_copy(buf.at[j % 2], o_hbm.at[rows], sem_w.at[j % 2])

            fetch(0).start()
            for j in range(n_sub):
                fetch(j).wait()
                write(j).start()
                if j + 1 < n_sub:
                    if j >= 1:
                        write(j - 1).wait()
                    fetch(j + 1).start()
            write(n_sub - 2).wait()
            write(n_sub - 1).wait()

    return gather(table, indices)


def _moe(x_packed, idx, gate, rank, counts, layer, w1, b1, w2, b2):
    n = x_packed.shape[0]
    d = D_MODEL
    bm = EXPERT_ROWS
    counts = counts[:, 0]
    padded = (counts + bm - 1) // bm * bm
    pad_ends = jnp.cumsum(padded)
    pad_starts = pad_ends - padded
    hot = idx[:, None, :] == jnp.arange(N_EXPERTS, dtype=I32)[None, :, None]
    dest = jnp.sum(jnp.where(hot, pad_starts[None, :, None], 0), axis=1) + rank
    n_blocks = n * TOP_K // bm + N_EXPERTS
    blk_start = jnp.arange(n_blocks, dtype=I32) * bm
    blk_exp = jnp.minimum(jnp.sum(blk_start[:, None] >= pad_ends[None, :], axis=1), N_EXPERTS - 1).astype(I32)
    mine = blk_exp[:, None] == jnp.arange(N_EXPERTS, dtype=I32)[None, :]
    into_expert = blk_start - jnp.sum(jnp.where(mine, pad_starts[None, :], 0), axis=1)
    blk_rows = jnp.clip(jnp.sum(jnp.where(mine, counts[None, :], 0), axis=1) - into_expert, 0, bm).astype(I32)
    xs = _sc_scatter_rows(x_packed, dest, n_blocks * bm)
    ys = _experts(xs, blk_exp, blk_rows, layer, w1, b1, w2, b2)
    yg = _sc_gather_rows(ys, dest.reshape(-1)).reshape(TOP_K, n, d // 2)
    return yg, gate.T


def _positions(groups):
    if len(groups) == 1:
        return jnp.arange(groups[0][1])
    return jnp.concatenate([jnp.tile(jnp.arange(s), b) for b, s in groups])


def _axial_tables(groups):
    t = _positions(groups)
    n = HEAD_DIM // 4
    inv = AXIAL_THETA ** (-jnp.arange(n, dtype=F32) / n)
    ar = (t // GRID_W).astype(F32)[:, None] * inv
    ac = (t % GRID_W).astype(F32)[:, None] * inv
    z = jnp.zeros_like(ar)
    cr, sr, cc, sc = jnp.cos(ar), jnp.sin(ar), jnp.cos(ac), jnp.sin(ac)
    c = jnp.concatenate([cr, cr, cc, cc], axis=-1)
    s1 = jnp.concatenate([z, sr, z, sc], axis=-1)
    s2 = jnp.concatenate([-sr, z, -sc, z], axis=-1)
    return tuple(jnp.tile(a, (1, 2)) for a in (c, s1, s2))


def _rope_tables(groups):
    t = _positions(groups)
    n = ROPE_DIMS // 2
    inv = ROPE_THETA ** (-jnp.arange(n, dtype=F32) / n)
    ang = t.astype(F32)[:, None] * inv
    c, s = jnp.cos(ang), jnp.sin(ang)
    z = jnp.zeros_like(c)
    rest = HEAD_DIM - ROPE_DIMS
    pad1 = jnp.ones((t.shape[0], rest), F32)
    pad0 = jnp.zeros((t.shape[0], rest), F32)
    cc = jnp.concatenate([c, c, pad1], axis=-1)
    s1 = jnp.concatenate([z, s, pad0], axis=-1)
    s2 = jnp.concatenate([-s, z, pad0], axis=-1)
    return tuple(jnp.tile(a, (1, 2)) for a in (cc, s1, s2))


def _gqa_head_order():
    g = GQA_Q_HEADS // GQA_KV_HEADS
    return [h for j in range(g) for h in (j, g + j)]


def _mixer_even(x, groups, tabs, w_in, rpb, q_gain, k_gain, w_out):
    hd = HEAD_DIM
    na_w = NA_HEADS * hd
    order = _gqa_head_order()
    q0 = 3 * na_w
    q_cols = np.concatenate([q0 + h * hd + np.arange(hd) for h in order])
    w_in_p = jnp.concatenate([w_in[:, :q0], w_in[:, q_cols], w_in[:, q0 + GQA_Q_HEADS * hd:]], axis=1).astype(BF16)
    out_rows = np.concatenate([na_w + h * hd + np.arange(hd) for h in order])
    w_out_p = jnp.concatenate([w_out[:na_w], w_out[out_rows]], axis=0).astype(BF16)
    n_na = na_w // LANES
    n_q = GQA_Q_HEADS * hd // LANES
    modes = ([("plain", Q_SCALE, 0)] * n_na + [("plain", 1.0, 0)] * (2 * n_na)
             + [("norm_rope", Q_SCALE, 0)] * n_q + [("norm_rope", 1.0, 1)] + [("plain", 1.0, 0)])
    gains = jnp.stack([jnp.tile(q_gain.astype(F32), 2), jnp.tile(k_gain.astype(F32), 2)])
    proj = _project(x, w_in_p, tabs, gains, modes, HEAD_DIM // 4)
    cc = _na_bias_table(rpb)
    ya, yb = [], []
    row0 = 0
    for b, s in groups:
        ya.append(_na_attention(proj, cc, row0, b, s))
        yb.append(_gqa_attention(proj, row0, b, s))
        row0 += b * s
    return [(jnp.concatenate(ya, axis=0), w_out_p[:na_w]), (jnp.concatenate(yb, axis=0), w_out_p[na_w:])]


def _mixer_odd(x, groups, tabs, w_in, w_out):
    n_slab = DIL_HEADS * HEAD_DIM // LANES
    modes = [("rope", Q_SCALE, 0)] * n_slab + [("rope", 1.0, 0)] * n_slab + [("plain", 1.0, 0)] * n_slab
    gains = jnp.ones((1, LANES), F32)
    qkv = _project(x, w_in.astype(BF16), tabs, gains, modes, ROPE_DIMS // 2, out_dtype=F32)
    parts, row0 = [], 0
    for b, s in groups:
        parts.append(_dilated_attention(qkv, row0, b, s))
        row0 += b * s
    return [(jnp.concatenate(parts, axis=0), w_out.astype(BF16))]


def _trunk(xs, w_in_even, rpb_a, q_gain_b, k_gain_b, w_out_even, w_in_odd, w_out_odd,
           ln1_g, ln1_b, ln2_g, ln2_b, router_w, router_b, moe_w1, moe_b1, moe_w2, moe_b2):
    groups = [(x.shape[0], x.shape[1]) for x in xs]
    x = jnp.concatenate([t.reshape(-1, D_MODEL) for t in xs], axis=0).astype(F32)
    tabs_even = _axial_tables(groups)
    tabs_odd = _rope_tables(groups)
    b1 = moe_b1.astype(F32)[:, :, None, :]
    b2 = moe_b2.astype(F32)[:, :, None, :]
    for l in range(DEPTH):
        i = l // 2
        if l % 2 == 0:
            parts = _mixer_even(x, groups, tabs_even, w_in_even[i], rpb_a[i], q_gain_b[i], k_gain_b[i],
                                w_out_even[i])
        else:
            parts = _mixer_odd(x, groups, tabs_odd, w_in_odd[i], w_out_odd[i])
        x, x_packed, idx, gate, rank, counts = _outproj_ln_route(parts, x, ln1_g[l], ln1_b[l], router_w[l],
                                                                 router_b[l])
        yg, gate = _moe(x_packed, idx, gate, rank, counts, l, moe_w1, b1, moe_w2, b2)
        if l + 1 < DEPTH:
            x = _combine_ln(yg, gate, x, ln2_g[l], ln2_b[l])
    outs, row0 = [], 0
    for t in xs:
        n = t.shape[0] * t.shape[1]
        outs.append(_combine_ln(yg, gate, x, ln2_g[-1], ln2_b[-1], row0, n).reshape(t.shape))
        row0 += n
    return tuple(outs)


def kernel(x_prompt, x_sample, w_in_even, rpb_a, q_gain_b, k_gain_b, w_out_even, w_in_odd, w_out_odd, ln1_g, ln1_b,
           ln2_g, ln2_b, router_w, router_b, moe_w1, moe_b1, moe_w2, moe_b2):
    params = (w_in_even, rpb_a, q_gain_b, k_gain_b, w_out_even, w_in_odd, w_out_odd,
              ln1_g, ln1_b, ln2_g, ln2_b, router_w, router_b, moe_w1.astype(BF16), moe_b1, moe_w2.astype(BF16), moe_b2)
    (y_prompt,) = _trunk((x_prompt,), *params)
    (y_sample,) = _trunk((x_sample,), *params)
    return y_prompt, y_sample
```
